```python
import math
import jax, jax.numpy as jnp
from jax import lax
import numpy as np

D_MODEL = 1024
BATCH = 8
SEQ = 2048
DEPTH = 4

N_MIXERS = 2
N_CONV_LAYERS = (DEPTH + N_MIXERS - 1) // N_MIXERS
N_ATTN_LAYERS = DEPTH // N_MIXERS
CONV_WIDTH = 3
N_HEADS = 16
KV_LATENT = 128
N_IDX_HEADS = 8
IDX_DIM = 64
TOPK_MAX = 256
Q_BLOCK = 128
ATTN_IN_WIDTH = N_HEADS * KV_LATENT + KV_LATENT + N_IDX_HEADS * IDX_DIM + IDX_DIM + N_IDX_HEADS
N_BUCKETS = 32
MAX_DISTANCE = 128
N_GROUPS = 8
EXPERTS_PER_GROUP = 8
N_EXPERTS = N_GROUPS * EXPERTS_PER_GROUP
TOPK_IN_GROUP = 2
D_EXPERT = 256
MOE_BLOCK = 128
LN_EPS = 1e-5
RMS_EPS = 1e-6
DEEPNORM_ALPHA = (2.0 * DEPTH) ** 0.25
DEEPNORM_BETA = (8.0 * DEPTH) ** -0.25

kernel_name = "hybrid_conv_dsa_hmoe_deepnorm"


def layer_norm(x, g, b):
    xf = x.astype(jnp.float32)
    mu = jnp.mean(xf, axis=-1, keepdims=True)
    var = jnp.mean(jnp.square(xf - mu), axis=-1, keepdims=True)
    y = (xf - mu) * lax.rsqrt(var + LN_EPS) * g.astype(jnp.float32) + b.astype(jnp.float32)
    return y.astype(x.dtype)


def rms_norm(x, g):
    xf = x.astype(jnp.float32)
    y = xf * lax.rsqrt(jnp.mean(jnp.square(xf), axis=-1, keepdims=True) + RMS_EPS) * g.astype(jnp.float32)
    return y.astype(x.dtype)


def t5_bucket(dist):
    max_exact = N_BUCKETS // 2
    d_f = jnp.maximum(dist, 1).astype(jnp.float32)
    large = max_exact + (jnp.log(d_f / max_exact) / math.log(MAX_DISTANCE / max_exact)
                         * (N_BUCKETS - max_exact)).astype(jnp.int32)
    large = jnp.minimum(large, N_BUCKETS - 1)
    return jnp.where(dist < max_exact, dist, large)


def short_conv_mixer(x, w_in, conv_k, w_out):
    S = x.shape[1]
    b_gate, c_gate, h = jnp.split(x @ w_in, 3, axis=-1)
    u = c_gate * h
    up = jnp.pad(u, ((0, 0), (CONV_WIDTH - 1, 0), (0, 0)))
    conv = up[:, 0:S] * conv_k[0]
    for j in range(1, CONV_WIDTH):
        conv = conv + up[:, j:j + S] * conv_k[j]
    return (b_gate * conv) @ w_out


def dsa_mixer(x, w_in, kv_norm_g, kidx_ln_g, kidx_ln_b, w_out, rel_bias):
    Bsz, S, _ = x.shape
    topk = min(TOPK_MAX, S // 4)
    hq = N_HEADS * KV_LATENT
    cuts = [hq, hq + KV_LATENT, hq + KV_LATENT + N_IDX_HEADS * IDX_DIM,
            hq + KV_LATENT + N_IDX_HEADS * IDX_DIM + IDX_DIM]
    q, c_kv, q_idx, k_idx, w_idx = jnp.split(x @ w_in, cuts, axis=-1)
    q = q.reshape(Bsz, S, N_HEADS, KV_LATENT)
    c_kv = rms_norm(c_kv, kv_norm_g)
    q_idx = q_idx.reshape(Bsz, S, N_IDX_HEADS, IDX_DIM)
    k_idx = layer_norm(k_idx, kidx_ln_g, kidx_ln_b)
    w_idx = w_idx * (N_IDX_HEADS ** -0.5)
    nb = S // Q_BLOCK
    key_pos = jnp.arange(S)

    def to_blocks(a):
        return jnp.moveaxis(a.reshape(Bsz, nb, Q_BLOCK, *a.shape[2:]), 1, 0)

    def block_fn(args):
        qb, qib, wb, blk = args
        t = blk * Q_BLOCK + jnp.arange(Q_BLOCK)
        s_idx = jnp.einsum('bthd,bsd->bths', qib, k_idx,
                           preferred_element_type=jnp.float32) * (IDX_DIM ** -0.5)
        score = jnp.einsum('bths,bth->bts', jax.nn.relu(s_idx), wb.astype(jnp.float32))
        causal = key_pos[None, :] <= t[:, None]
        score = jnp.where(causal[None], score, -jnp.inf)
        _, sel = lax.top_k(score, topk)
        valid = sel <= t[None, :, None]
        kv_sel = jax.vmap(lambda c, i: c[i])(c_kv, sel)
        logits = jnp.einsum('bthc,btjc->bthj', qb, kv_sel,
                            preferred_element_type=jnp.float32) * (KV_LATENT ** -0.5)
        bucket = t5_bucket(jnp.maximum(t[None, :, None] - sel, 0))
        bias = jnp.moveaxis(rel_bias[bucket], -1, 2)
        logits = logits + bias.astype(jnp.float32)
        logits = jnp.where(valid[:, :, None, :], logits, -jnp.inf)
        p = jax.nn.softmax(logits, axis=-1).astype(x.dtype)
        o = jnp.einsum('bthj,btjc->bthc', p, kv_sel)
        return o.reshape(Bsz, Q_BLOCK, hq)

    o = lax.map(block_fn, (to_blocks(q), to_blocks(q_idx), to_blocks(w_idx), jnp.arange(nb)))
    o = jnp.moveaxis(o, 0, 1).reshape(Bsz, S, hq)
    return o @ w_out


def hier_moe(x, wg, bg, we, be, w1, w3, w2):
    Bsz, S, D = x.shape
    xt = x.reshape(-1, D)
    T = xt.shape[0]
    g_logits = (xt @ wg + bg).astype(jnp.float32)
    g_prob = jax.nn.softmax(g_logits, axis=-1)
    g_top = jnp.argmax(g_logits, axis=-1)
    g_gate = jnp.take_along_axis(g_prob, g_top[:, None], axis=-1)
    e_logits = (xt @ we + be).astype(jnp.float32).reshape(T, N_GROUPS, EXPERTS_PER_GROUP)
    e_logits = jnp.take_along_axis(e_logits, g_top[:, None, None], axis=1)[:, 0]
    e_val, e_loc = lax.top_k(e_logits, TOPK_IN_GROUP)
    gate = jax.nn.softmax(e_val, axis=-1) * g_gate
    e_id = g_top[:, None] * EXPERTS_PER_GROUP + e_loc
    N = T * TOPK_IN_GROUP
    flat_e = e_id.reshape(-1).astype(jnp.int32)
    flat_tok = jnp.repeat(jnp.arange(T, dtype=jnp.int32), TOPK_IN_GROUP)
    flat_w = gate.reshape(-1)
    order = jnp.argsort(flat_e)
    e_s, tok_s, w_s = flat_e[order], flat_tok[order], flat_w[order]
    counts = jax.ops.segment_sum(jnp.ones_like(flat_e), flat_e, num_segments=N_EXPERTS)
    padded = (counts + MOE_BLOCK - 1) // MOE_BLOCK * MOE_BLOCK
    start = jnp.cumsum(counts) - counts
    pend = jnp.cumsum(padded)
    pstart = pend - padded
    dest = pstart[e_s] + jnp.arange(N, dtype=jnp.int32) - start[e_s]
    R = (-(-N // MOE_BLOCK) + N_EXPERTS) * MOE_BLOCK
    row_tok = jnp.zeros((R,), jnp.int32).at[dest].set(tok_s)
    row_w = jnp.zeros((R,), jnp.float32).at[dest].set(w_s)
    nblk = R // MOE_BLOCK
    blk_e = jnp.minimum(jnp.searchsorted(pend, jnp.arange(nblk, dtype=jnp.int32) * MOE_BLOCK,
                                         side='right'), N_EXPERTS - 1)
    xs = xt[row_tok].reshape(nblk, MOE_BLOCK, D)

    def expert_block(args):
        xb, e = args
        h = jax.nn.silu(xb @ w1[e]) * (xb @ w3[e])
        return h @ w2[e]

    ys = lax.map(expert_block, (xs, blk_e)).reshape(R, D)
    out = jnp.zeros_like(xt).at[row_tok].add(ys * row_w[:, None].astype(ys.dtype))
    return out.reshape(Bsz, S, D)


def setup_inputs(seed: int = 0) -> dict:
    key = jax.random.key(seed)
    ks = jax.random.split(key, 24)
    D = D_MODEL
    f32 = jnp.float32

    def nrm(k, shape, scale):
        return jax.random.normal(k, shape, f32) * scale

    return {
        "x": nrm(ks[0], (BATCH, SEQ, D), 1.0),
        "conv_w_in": nrm(ks[1], (N_CONV_LAYERS, D, 3 * D), D ** -0.5),
        "conv_k": nrm(ks[2], (N_CONV_LAYERS, CONV_WIDTH, D), CONV_WIDTH ** -0.5),
        "conv_w_out": nrm(ks[3], (N_CONV_LAYERS, D, D), DEEPNORM_BETA * D ** -0.5),
        "attn_w_in": nrm(ks[4], (N_ATTN_LAYERS, D, ATTN_IN_WIDTH), D ** -0.5),
        "kv_norm_g": 1.0 + nrm(ks[5], (N_ATTN_LAYERS, KV_LATENT), 0.02),
        "kidx_ln_g": 1.0 + nrm(ks[6], (N_ATTN_LAYERS, IDX_DIM), 0.02),
        "kidx_ln_b": nrm(ks[7], (N_ATTN_LAYERS, IDX_DIM), 0.02),
        "attn_w_out": nrm(ks[8], (N_ATTN_LAYERS, N_HEADS * KV_LATENT, D),
                          DEEPNORM_BETA * (N_HEADS * KV_LATENT) ** -0.5),
        "rel_bias": nrm(ks[9], (N_BUCKETS, N_HEADS), 0.5),
        "router_wg": nrm(ks[10], (DEPTH, D, N_GROUPS), D ** -0.5),
        "router_bg": nrm(ks[11], (DEPTH, N_GROUPS), 0.01),
        "router_we": nrm(ks[12], (DEPTH, D, N_EXPERTS), D ** -0.5),
        "router_be": nrm(ks[13], (DEPTH, N_EXPERTS), 0.01),
        "exp_w1": nrm(ks[14], (DEPTH, N_EXPERTS, D, D_EXPERT), D ** -0.5),
        "exp_w3": nrm(ks[15], (DEPTH, N_EXPERTS, D, D_EXPERT), D ** -0.5),
        "exp_w2": nrm(ks[16], (DEPTH, N_EXPERTS, D_EXPERT, D), DEEPNORM_BETA * D_EXPERT ** -0.5),
        "ln1_g": 1.0 + nrm(ks[17], (DEPTH, D), 0.02),
        "ln1_b": nrm(ks[18], (DEPTH, D), 0.02),
        "ln2_g": 1.0 + nrm(ks[19], (DEPTH, D), 0.02),
        "ln2_b": nrm(ks[20], (DEPTH, D), 0.02),
    }


def reference(x, conv_w_in, conv_k, conv_w_out, attn_w_in, kv_norm_g, kidx_ln_g, kidx_ln_b,
              attn_w_out, rel_bias, router_wg, router_bg, router_we, router_be,
              exp_w1, exp_w3, exp_w2, ln1_g, ln1_b, ln2_g, ln2_b):
    for i in range(DEPTH):
        j = i // N_MIXERS
        if i % N_MIXERS == 0:
            m = short_conv_mixer(x, conv_w_in[j], conv_k[j], conv_w_out[j])
        else:
            m = dsa_mixer(x, attn_w_in[j], kv_norm_g[j], kidx_ln_g[j], kidx_ln_b[j],
                          attn_w_out[j], rel_bias)
        x = layer_norm(DEEPNORM_ALPHA * x + m, ln1_g[i], ln1_b[i])
        f = hier_moe(x, router_wg[i], router_bg[i], router_we[i], router_be[i],
                     exp_w1[i], exp_w3[i], exp_w2[i])
        x = layer_norm(DEEPNORM_ALPHA * x + f, ln2_g[i], ln2_b[i])
    return x
```

```python
import functools
import math

import jax
import jax.numpy as jnp
from jax import lax
from jax.experimental import pallas as pl
from jax.experimental.pallas import tpu as pltpu

CONV_WIDTH = 3
N_HEADS = 16
KV_LATENT = 128
N_IDX_HEADS = 8
IDX_DIM = 64
TOPK_MAX = 256
Q_BLOCK = 128
N_BUCKETS = 32
MAX_DISTANCE = 128
N_GROUPS = 8
EXPERTS_PER_GROUP = 8
N_EXPERTS = N_GROUPS * EXPERTS_PER_GROUP
TOPK_IN_GROUP = 2
MOE_BLOCK = 128
LN_EPS = 1e-5
RMS_EPS = 1e-6

V7X_LANES = 128
V7X_SUBLANES = 8
V7X_VMEM_BYTES = 64 * 1024 * 1024

F32 = jnp.float32
BF16 = jnp.bfloat16
I32 = jnp.int32
NEG_INF = float("-inf")
INT32_MIN = -(2 ** 31)

_NT = (((1,), (1,)), ((), ()))


def _dot(a, b):
    return jnp.dot(a, b, preferred_element_type=F32)


def _dot_nt(a, b):
    return lax.dot_general(a, b, _NT, preferred_element_type=F32)


def _layer_norm(z, g, b):
    mu = jnp.mean(z, axis=-1, keepdims=True)
    zc = z - mu
    var = jnp.mean(zc * zc, axis=-1, keepdims=True)
    return zc * lax.rsqrt(var + LN_EPS) * g + b


def _cparams(semantics, vmem_mib):
    assert vmem_mib * 1024 * 1024 < V7X_VMEM_BYTES
    return pltpu.CompilerParams(dimension_semantics=semantics,
                                vmem_limit_bytes=vmem_mib * 1024 * 1024)


def _conv_kernel(x_ref, win_ref, ck_ref, wout_ref, g_ref, b_ref, o_ref, carry_ref, gbuf_ref,
                 *, ts, d, cw, alpha):
    @pl.when(pl.program_id(1) == 0)
    def _():
        carry_ref[...] = jnp.zeros_like(carry_ref)

    x = x_ref[...]
    xb = x.astype(BF16)
    row = lax.broadcasted_iota(I32, (ts, cw), 0)
    for c in range(d // cw):
        lo, hi = c * cw, (c + 1) * cw
        bg = _dot(xb, win_ref[:, lo:hi])
        cg = _dot(xb, win_ref[:, d + lo:d + hi])
        hh = _dot(xb, win_ref[:, 2 * d + lo:2 * d + hi])
        u = cg * hh
        prev = carry_ref[:, lo:hi]
        u1 = jnp.where(row == 0, prev[7:8], pltpu.roll(u, 1, 0))
        u2 = jnp.where(row == 0, prev[6:7], jnp.where(row == 1, prev[7:8], pltpu.roll(u, 2, 0)))
        k = ck_ref[:, lo:hi]
        conv = u2 * k[0:1] + u1 * k[1:2] + u * k[2:3]
        gbuf_ref[:, lo:hi] = (bg * conv).astype(BF16)
        carry_ref[:, lo:hi] = u[ts - V7X_SUBLANES:ts]
    y = _dot(gbuf_ref[...], wout_ref[...])
    o_ref[...] = _layer_norm(alpha * x + y, g_ref[...], b_ref[...])


def _conv_layer(x, w_in, conv_k, w_out, g, b, alpha):
    bsz, s, d = x.shape
    ts, cw = 512, 512
    kern = functools.partial(_conv_kernel, ts=ts, d=d, cw=cw, alpha=alpha)
    return pl.pallas_call(
        kern,
        out_shape=jax.ShapeDtypeStruct((bsz, s, d), F32),
        grid=(bsz, s // ts),
        in_specs=[
            pl.BlockSpec((None, ts, d), lambda i, j: (i, j, 0)),
            pl.BlockSpec((d, 3 * d), lambda i, j: (0, 0)),
            pl.BlockSpec((CONV_WIDTH, d), lambda i, j: (0, 0)),
            pl.BlockSpec((d, d), lambda i, j: (0, 0)),
            pl.BlockSpec((1, d), lambda i, j: (0, 0)),
            pl.BlockSpec((1, d), lambda i, j: (0, 0)),
        ],
        out_specs=pl.BlockSpec((None, ts, d), lambda i, j: (i, j, 0)),
        scratch_shapes=[pltpu.VMEM((V7X_SUBLANES, d), F32), pltpu.VMEM((ts, d), BF16)],
        compiler_params=_cparams(("arbitrary", "arbitrary"), 48),
        name="conv_mixer_ln",
    )(x, w_in.astype(BF16), conv_k, w_out.astype(BF16), g.reshape(1, d), b.reshape(1, d))


def _proj_kernel(x_ref, wq_ref, ws_ref, ww_ref, kvg_ref, lng_ref, lnb_ref,
                 q_ref, ckv_ref, ckvt_ref, qidx_ref, kidx_ref, widxt_ref, *, idx_scale):
    xb = x_ref[...].astype(BF16)
    q = _dot(xb, wq_ref[...])
    for h in range(N_HEADS):
        q_ref[h] = q[:, h * KV_LATENT:(h + 1) * KV_LATENT].astype(BF16)
    sm = _dot(xb, ws_ref[...])
    ckv = sm[:, :KV_LATENT]
    ckv = ckv * lax.rsqrt(jnp.mean(ckv * ckv, axis=-1, keepdims=True) + RMS_EPS) * kvg_ref[...]
    ckv_ref[...] = ckv.astype(BF16)
    ckvt_ref[...] = ckv.T.astype(BF16)
    nq = N_IDX_HEADS * IDX_DIM
    qidx_ref[...] = sm[:, KV_LATENT:KV_LATENT + nq].astype(BF16)
    kidx = sm[:, KV_LATENT + nq:KV_LATENT + nq + IDX_DIM]
    kidx_ref[...] = _layer_norm(kidx, lng_ref[...], lnb_ref[...]).astype(BF16)
    widxt_ref[...] = _dot_nt(ww_ref[...], xb) * idx_scale


def _attn_proj(x, w_in, kv_g, ln_g, ln_b):
    bsz, s, d = x.shape
    ts = 512
    hq = N_HEADS * KV_LATENT
    nq = N_IDX_HEADS * IDX_DIM
    small = KV_LATENT + nq + IDX_DIM
    small_pad = -(-small // V7X_LANES) * V7X_LANES
    wq = w_in[:, :hq].astype(BF16)
    ws = jnp.pad(w_in[:, hq:hq + small], ((0, 0), (0, small_pad - small))).astype(BF16)
    ww = w_in[:, hq + small:].T.astype(BF16)
    idx_scale = (N_IDX_HEADS ** -0.5) * (IDX_DIM ** -0.5)
    kern = functools.partial(_proj_kernel, idx_scale=idx_scale)
    return pl.pallas_call(
        kern,
        out_shape=(
            jax.ShapeDtypeStruct((bsz, N_HEADS, s, KV_LATENT), BF16),
            jax.ShapeDtypeStruct((bsz, s, KV_LATENT), BF16),
            jax.ShapeDtypeStruct((bsz, KV_LATENT, s), BF16),
            jax.ShapeDtypeStruct((bsz, s, nq), BF16),
            jax.ShapeDtypeStruct((bsz, s, IDX_DIM), BF16),
            jax.ShapeDtypeStruct((bsz, N_IDX_HEADS, s), F32),
        ),
        grid=(bsz, s // ts),
        in_specs=[
            pl.BlockSpec((None, ts, d), lambda i, j: (i, j, 0)),
            pl.BlockSpec((d, hq), lambda i, j: (0, 0)),
            pl.BlockSpec((d, small_pad), lambda i, j: (0, 0)),
            pl.BlockSpec((N_IDX_HEADS, d), lambda i, j: (0, 0)),
            pl.BlockSpec((1, KV_LATENT), lambda i, j: (0, 0)),
            pl.BlockSpec((1, IDX_DIM), lambda i, j: (0, 0)),
            pl.BlockSpec((1, IDX_DIM), lambda i, j: (0, 0)),
        ],
        out_specs=(
            pl.BlockSpec((None, N_HEADS, ts, KV_LATENT), lambda i, j: (i, 0, j, 0)),
            pl.BlockSpec((None, ts, KV_LATENT), lambda i, j: (i, j, 0)),
            pl.BlockSpec((None, KV_LATENT, ts), lambda i, j: (i, 0, j)),
            pl.BlockSpec((None, ts, nq), lambda i, j: (i, j, 0)),
            pl.BlockSpec((None, ts, IDX_DIM), lambda i, j: (i, j, 0)),
            pl.BlockSpec((None, N_IDX_HEADS, ts), lambda i, j: (i, 0, j)),
        ),
        compiler_params=_cparams(("arbitrary", "arbitrary"), 48),
        name="attn_proj",
    )(x, wq, ws, ww, kv_g.reshape(1, -1), ln_g.reshape(1, -1), ln_b.reshape(1, -1))


def _bias_kernel(rb_ref, o_ref):
    rows = o_ref.shape[1]
    j = lax.broadcasted_iota(I32, (rows, Q_BLOCK), 0)
    r = lax.broadcasted_iota(I32, (rows, Q_BLOCK), 1)
    dist = Q_BLOCK + r - j
    dpos = jnp.maximum(dist, 0)
    max_exact = N_BUCKETS // 2
    d_f = jnp.maximum(dpos, 1).astype(F32)
    large = max_exact + (jnp.log(d_f / max_exact) / math.log(MAX_DISTANCE / max_exact)
                         * (N_BUCKETS - max_exact)).astype(I32)
    large = jnp.minimum(large, N_BUCKETS - 1)
    bucket = jnp.where(dpos < max_exact, dpos, large)
    for h in range(N_HEADS):
        acc = jnp.zeros((rows, Q_BLOCK), F32)
        for bk in range(N_BUCKETS):
            acc = jnp.where(bucket == bk, rb_ref[bk, h], acc)
        o_ref[h] = jnp.where(dist >= 0, acc - rb_ref[N_BUCKETS - 1, h], 0.0)


def _bias_tiles(rel_bias):
    rows = 3 * Q_BLOCK
    return pl.pallas_call(
        _bias_kernel,
        out_shape=jax.ShapeDtypeStruct((N_HEADS, rows, Q_BLOCK), F32),
        in_specs=[pl.BlockSpec(memory_space=pltpu.SMEM)],
        out_specs=pl.BlockSpec(memory_space=pltpu.VMEM),
        name="rel_bias_tiles",
    )(rel_bias)


def _attn_body(qi, q_ref, qidx_ref, widxt_ref, kidx_ref, ckv_ref, ckvt_ref, bias_ref, x_ref,
               wout_ref, g_ref, b_ref, o_ref, key_ref, madd_ref, lg_ref, obuf_ref,
               *, sk, topk, alpha):
    t_abs = qi * Q_BLOCK + lax.broadcasted_iota(I32, (1, Q_BLOCK), 1)
    s_abs = lax.broadcasted_iota(I32, (sk, Q_BLOCK), 0)
    valid = s_abs <= t_abs

    kidx = kidx_ref[0:sk, :]
    score = jnp.zeros((sk, Q_BLOCK), F32)
    for h in range(N_IDX_HEADS):
        sh = _dot_nt(kidx, qidx_ref[:, h * IDX_DIM:(h + 1) * IDX_DIM])
        score = score + jnp.maximum(sh, 0.0) * widxt_ref[h:h + 1, :]
    score = jnp.where(score == 0.0, 0.0, score)
    score = jnp.where(valid, score, NEG_INF)
    bits = pltpu.bitcast(score, I32)
    key_ref[0:sk] = bits ^ ((bits >> 31) & 0x7FFFFFFF)

    def bit_body(it, thr):
        cand = thr + lax.shift_left(jnp.int32(1), 31 - it)
        cnt = jnp.sum(jnp.where(key_ref[0:sk] >= cand, 1.0, 0.0), axis=0, keepdims=True)
        return jnp.where(cnt >= topk, cand, thr)

    thr = lax.fori_loop(0, 32, bit_body, jnp.full((1, Q_BLOCK), INT32_MIN, I32))
    key = key_ref[0:sk]
    ge = key >= thr
    cnt_ge = jnp.sum(jnp.where(ge, 1.0, 0.0), axis=0, keepdims=True)
    madd_ref[0:sk] = jnp.where(ge & valid, 0.0, NEG_INF)

    tied = jnp.where((cnt_ge > topk) & (t_abs >= topk - 1), 1.0, 0.0)

    @pl.when(jnp.max(tied) > 0.0)
    def _():
        chunk = 256
        cnt_gt = jnp.sum(jnp.where(key_ref[0:sk] > thr, 1.0, 0.0), axis=0, keepdims=True)
        need = topk - cnt_gt
        tri = jnp.where(lax.broadcasted_iota(I32, (chunk, chunk), 0)
                        >= lax.broadcasted_iota(I32, (chunk, chunk), 1), 1.0, 0.0).astype(BF16)
        run = jnp.zeros((1, Q_BLOCK), F32)
        for c in range(sk // chunk):
            kc = key_ref[c * chunk:(c + 1) * chunk]
            tie = kc == thr
            pre = _dot(tri, jnp.where(tie, 1.0, 0.0).astype(BF16)) + run
            run = pre[chunk - 1:chunk]
            vc = (c * chunk + lax.broadcasted_iota(I32, (chunk, Q_BLOCK), 0)) <= t_abs
            sel = ((kc > thr) | (tie & (pre <= need))) & vc
            madd_ref[c * chunk:(c + 1) * chunk] = jnp.where(sel, 0.0, NEG_INF)

    win = 2 * Q_BLOCK
    s0 = pl.multiple_of(jnp.maximum(qi - 1, 0) * Q_BLOCK, Q_BLOCK)
    toff = pl.multiple_of(jnp.where(qi == 0, Q_BLOCK, 0), Q_BLOCK)
    scale = KV_LATENT ** -0.5

    def head_body(h, carry):
        lg_ref[0:sk] = _dot_nt(ckv_ref[0:sk, :], q_ref[h]) * scale + madd_ref[0:sk]
        lg_ref[pl.ds(s0, win), :] = lg_ref[pl.ds(s0, win), :] + bias_ref[h, pl.ds(toff, win), :]
        lt = lg_ref[0:sk]
        e = jnp.exp(lt - jnp.max(lt, axis=0, keepdims=True))
        ssum = jnp.sum(e, axis=0, keepdims=True)
        ot = _dot(ckvt_ref[:, 0:sk], e.astype(BF16)) / ssum
        obuf_ref[h] = ot.T.astype(BF16)
        return carry

    lax.fori_loop(0, N_HEADS, head_body, 0)
    acc = _dot(obuf_ref[0], wout_ref[0])
    for h in range(1, N_HEADS):
        acc = acc + _dot(obuf_ref[h], wout_ref[h])
    o_ref[...] = _layer_norm(alpha * x_ref[...] + acc, g_ref[...], b_ref[...])


def _attn_kernel(*refs, nq, nv, topk, alpha):
    qi = pl.program_id(1)
    per = nq // nv
    for v in range(nv):
        body = functools.partial(_attn_body, qi, *refs, sk=(v + 1) * per * Q_BLOCK, topk=topk,
                                 alpha=alpha)
        pl.when(qi // per == v)(body)


def _attn_layer(x, w_in, kv_g, ln_g, ln_b, w_out, bias_t, g, b, alpha):
    bsz, s, d = x.shape
    nq = s // Q_BLOCK
    nv = max(1, min(4, s // (2 * TOPK_MAX)))
    assert nq % nv == 0 and (nq // nv) * Q_BLOCK >= TOPK_MAX
    topk = min(TOPK_MAX, s // 4)
    q, ckv, ckvt, qidx, kidx, widxt = _attn_proj(x, w_in, kv_g, ln_g, ln_b)
    nqi = N_IDX_HEADS * IDX_DIM
    kern = functools.partial(_attn_kernel, nq=nq, nv=nv, topk=topk, alpha=alpha)
    return pl.pallas_call(
        kern,
        out_shape=jax.ShapeDtypeStruct((bsz, s, d), F32),
        grid=(bsz, nq),
        in_specs=[
            pl.BlockSpec((None, N_HEADS, Q_BLOCK, KV_LATENT), lambda i, j: (i, 0, j, 0)),
            pl.BlockSpec((None, Q_BLOCK, nqi), lambda i, j: (i, j, 0)),
            pl.BlockSpec((None, N_IDX_HEADS, Q_BLOCK), lambda i, j: (i, 0, j)),
            pl.BlockSpec((None, s, IDX_DIM), lambda i, j: (i, 0, 0)),
            pl.BlockSpec((None, s, KV_LATENT), lambda i, j: (i, 0, 0)),
            pl.BlockSpec((None, KV_LATENT, s), lambda i, j: (i, 0, 0)),
            pl.BlockSpec((N_HEADS, 3 * Q_BLOCK, Q_BLOCK), lambda i, j: (0, 0, 0)),
            pl.BlockSpec((None, Q_BLOCK, d), lambda i, j: (i, j, 0)),
            pl.BlockSpec((N_HEADS, KV_LATENT, d), lambda i, j: (0, 0, 0)),
            pl.BlockSpec((1, d), lambda i, j: (0, 0)),
            pl.BlockSpec((1, d), lambda i, j: (0, 0)),
        ],
        out_specs=pl.BlockSpec((None, Q_BLOCK, d), lambda i, j: (i, j, 0)),
        scratch_shapes=[
            pltpu.VMEM((s, Q_BLOCK), I32),
            pltpu.VMEM((s, Q_BLOCK), F32),
            pltpu.VMEM((s, Q_BLOCK), F32),
            pltpu.VMEM((N_HEADS, Q_BLOCK, KV_LATENT), BF16),
        ],
        compiler_params=_cparams(("arbitrary", "arbitrary"), 48),
        name="dsa_attention_ln",
    )(q, qidx, widxt, kidx, ckv, ckvt, bias_t, x,
      w_out.astype(BF16).reshape(N_HEADS, KV_LATENT, d), g.reshape(1, d), b.reshape(1, d))


def _split_bf16(a):
    hi = a.astype(BF16)
    lo = (a - hi.astype(F32)).astype(BF16)
    return hi, lo


def _router_kernel(x_ref, w_ref, b_ref, eid_ref, gate_ref, rank_ref, cnt_ref, base_ref, u_ref, *, tt):
    @pl.when(pl.program_id(0) == 0)
    def _():
        base_ref[...] = jnp.zeros_like(base_ref)
        u_ref[...] = jnp.where(lax.broadcasted_iota(I32, (tt, tt), 0)
                               < lax.broadcasted_iota(I32, (tt, tt), 1), 1.0, 0.0).astype(BF16)

    xh, xl = _split_bf16(x_ref[...])
    wh, wl = _split_bf16(w_ref[...])
    lt = _dot_nt(wh, xh) + (_dot_nt(wh, xl) + _dot_nt(wl, xh)) + b_ref[...]

    ng, ne = N_GROUPS, EXPERTS_PER_GROUP
    gl = lt[0:ng]
    iog = lax.broadcasted_iota(I32, (ng, tt), 0).astype(F32)
    gmax = jnp.max(gl, axis=0, keepdims=True)
    gidx = jnp.min(jnp.where(gl == gmax, iog, float(ng)), axis=0, keepdims=True)
    g_gate = 1.0 / jnp.sum(jnp.exp(gl - gmax), axis=0, keepdims=True)

    el = jnp.zeros((ne, tt), F32)
    for gi in range(ng):
        el = jnp.where(gidx == float(gi), lt[ng + gi * ne:ng + (gi + 1) * ne], el)
    ioe = lax.broadcasted_iota(I32, (ne, tt), 0).astype(F32)
    m1 = jnp.max(el, axis=0, keepdims=True)
    i1 = jnp.min(jnp.where(el == m1, ioe, float(ne)), axis=0, keepdims=True)
    el2 = jnp.where(ioe == i1, NEG_INF, el)
    m2 = jnp.max(el2, axis=0, keepdims=True)
    i2 = jnp.min(jnp.where(el2 == m2, ioe, float(ne)), axis=0, keepdims=True)
    ex = jnp.exp(m2 - m1)
    p1 = 1.0 / (1.0 + ex)
    gate_ref[0:1, :] = p1 * g_gate
    gate_ref[1:2, :] = ex * p1 * g_gate
    e1 = gidx * float(ne) + i1
    e2 = gidx * float(ne) + i2
    eid_ref[0:1, :] = e1.astype(I32)
    eid_ref[1:2, :] = e2.astype(I32)

    iox = lax.broadcasted_iota(I32, (N_EXPERTS, tt), 0).astype(F32)
    oh1 = jnp.where(iox == e1, 1.0, 0.0)
    oh2 = jnp.where(iox == e2, 1.0, 0.0)
    pre1 = _dot(oh1.astype(BF16), u_ref[...])
    pre2 = _dot(oh2.astype(BF16), u_ref[...])
    tot1 = jnp.sum(oh1, axis=1, keepdims=True)
    tot2 = jnp.sum(oh2, axis=1, keepdims=True)
    base = base_ref[...]
    rank_ref[0:1, :] = jnp.sum(oh1 * (base + pre1), axis=0, keepdims=True).astype(I32)
    rank_ref[1:2, :] = jnp.sum(oh2 * (base + tot1 + pre2), axis=0, keepdims=True).astype(I32)
    base = base + tot1 + tot2
    base_ref[...] = base
    cnt_ref[...] = jnp.broadcast_to(base, cnt_ref.shape).astype(I32)


def _router(xt, wg, bg, we, be):
    t, d = xt.shape
    tt = 512
    rows = V7X_LANES
    wcat = jnp.pad(jnp.concatenate([wg, we], axis=1).T, ((0, rows - N_GROUPS - N_EXPERTS), (0, 0)))
    bcat = jnp.pad(jnp.concatenate([bg, be]), (0, rows - N_GROUPS - N_EXPERTS)).reshape(rows, 1)
    kern = functools.partial(_router_kernel, tt=tt)
    return pl.pallas_call(
        kern,
        out_shape=(
            jax.ShapeDtypeStruct((TOPK_IN_GROUP, t), I32),
            jax.ShapeDtypeStruct((TOPK_IN_GROUP, t), F32),
            jax.ShapeDtypeStruct((TOPK_IN_GROUP, t), I32),
            jax.ShapeDtypeStruct((N_EXPERTS, V7X_LANES), I32),
        ),
        grid=(t // tt,),
        in_specs=[
            pl.BlockSpec((tt, d), lambda i: (i, 0)),
            pl.BlockSpec((rows, d), lambda i: (0, 0)),
            pl.BlockSpec((rows, 1), lambda i: (0, 0)),
        ],
        out_specs=(
            pl.BlockSpec((TOPK_IN_GROUP, tt), lambda i: (0, i)),
            pl.BlockSpec((TOPK_IN_GROUP, tt), lambda i: (0, i)),
            pl.BlockSpec((TOPK_IN_GROUP, tt), lambda i: (0, i)),
            pl.BlockSpec((N_EXPERTS, V7X_LANES), lambda i: (0, 0)),
        ),
        scratch_shapes=[pltpu.VMEM((N_EXPERTS, 1), F32), pltpu.VMEM((tt, tt), BF16)],
        compiler_params=_cparams(("arbitrary",), 32),
        name="moe_router",
    )(xt, wcat, bcat)


def _dest_kernel(cnt_ref, eid_ref, rank_ref, dest_ref, blke_ref, meta_ref, pstart_ref, *, nblk):
    shift = MOE_BLOCK.bit_length() - 1

    def expert_body(e, acc):
        nb = (cnt_ref[e] + (MOE_BLOCK - 1)) >> shift
        pstart_ref[e] = acc
        b0 = acc >> shift

        def blk_body(j, c):
            blke_ref[b0 + j] = e
            return c

        lax.fori_loop(0, nb, blk_body, 0)
        return acc + (nb << shift)

    total = lax.fori_loop(0, N_EXPERTS, expert_body, jnp.int32(0))
    nused = total >> shift
    meta_ref[0] = nused
    last_e = blke_ref[nused - 1]

    def tail_body(j, c):
        blke_ref[j] = last_e
        return c

    lax.fori_loop(nused, nblk, tail_body, 0)

    def dest_body(e, dest):
        return dest + jnp.where(eid_ref[...] == e, pstart_ref[e], 0)

    dest_ref[...] = lax.fori_loop(0, N_EXPERTS, dest_body, rank_ref[...])


def _dest(cnt, eid, rank, nblk):
    t = eid.shape[1]
    kern = functools.partial(_dest_kernel, nblk=nblk)
    return pl.pallas_call(
        kern,
        out_shape=(
            jax.ShapeDtypeStruct((TOPK_IN_GROUP, t), I32),
            jax.ShapeDtypeStruct((nblk,), I32),
            jax.ShapeDtypeStruct((1,), I32),
        ),
        in_specs=[
            pl.BlockSpec(memory_space=pltpu.SMEM),
            pl.BlockSpec(memory_space=pltpu.VMEM),
            pl.BlockSpec(memory_space=pltpu.VMEM),
        ],
        out_specs=(
            pl.BlockSpec(memory_space=pltpu.VMEM),
            pl.BlockSpec(memory_space=pltpu.SMEM),
            pl.BlockSpec(memory_space=pltpu.SMEM),
        ),
        scratch_shapes=[pltpu.SMEM((N_EXPERTS,), I32)],
        name="moe_dest",
    )(cnt, eid, rank)


def _row_copy(src_ref, src_row, dst_ref, dst_row, sem):
    return pltpu.make_async_copy(src_ref.at[pl.ds(src_row, 1)], dst_ref.at[pl.ds(dst_row, 1)], sem)


def _scatter_kernel(dest_ref, x_ref, xs_in_ref, xs_ref, sem, *, tr, t):
    del xs_in_ref
    base = pl.program_id(0) * tr

    def issue(r, c):
        for k in range(TOPK_IN_GROUP):
            _row_copy(x_ref, r, xs_ref, dest_ref[k * t + base + r], sem).start()
        return c

    lax.fori_loop(0, tr, issue, 0)

    def drain(r, c):
        for k in range(TOPK_IN_GROUP):
            _row_copy(x_ref, 0, xs_ref, 0, sem).wait()
        return c

    lax.fori_loop(0, tr, drain, 0)


def _scatter(dest_flat, xt, nrows):
    t, d = xt.shape
    tr = 256
    kern = functools.partial(_scatter_kernel, tr=tr, t=t)
    return pl.pallas_call(
        kern,
        out_shape=jax.ShapeDtypeStruct((nrows, d), F32),
        grid_spec=pltpu.PrefetchScalarGridSpec(
            num_scalar_prefetch=1,
            grid=(t // tr,),
            in_specs=[
                pl.BlockSpec((tr, d), lambda i, dest: (i, 0)),
                pl.BlockSpec(memory_space=pl.ANY),
            ],
            out_specs=pl.BlockSpec(memory_space=pl.ANY),
            scratch_shapes=[pltpu.SemaphoreType.DMA],
        ),
        input_output_aliases={2: 0},
        compiler_params=_cparams(("arbitrary",), 32),
        name="moe_scatter_rows",
    )(dest_flat, xt, jnp.zeros((nrows, d), F32))


def _gmm_kernel(blke_ref, meta_ref, xs_ref, w1_ref, w3_ref, w2_ref, ys_ref, w1b_ref, w3b_ref, w2b_ref):
    nb = pl.program_id(0)

    @pl.when(nb < meta_ref[0])
    def _():
        @pl.when((nb == 0) | (blke_ref[nb] != blke_ref[jnp.maximum(nb - 1, 0)]))
        def _():
            w1b_ref[...] = w1_ref[...].astype(BF16)
            w3b_ref[...] = w3_ref[...].astype(BF16)
            w2b_ref[...] = w2_ref[...].astype(BF16)

        xb = xs_ref[...].astype(BF16)
        h1 = _dot(xb, w1b_ref[...])
        h3 = _dot(xb, w3b_ref[...])
        hh = (h1 * jax.nn.sigmoid(h1) * h3).astype(BF16)
        ys_ref[...] = _dot(hh, w2b_ref[...])

    @pl.when(nb >= meta_ref[0])
    def _():
        ys_ref[...] = jnp.zeros_like(ys_ref)


def _gmm(blke, meta, xs, w1, w3, w2):
    nrows, d = xs.shape
    de = w1.shape[-1]
    nblk = nrows // MOE_BLOCK

    def row_map(i, blke, meta):
        return (jnp.minimum(i, meta[0] - 1), 0)

    def w_map(i, blke, meta):
        return (blke[i], 0, 0)

    return pl.pallas_call(
        _gmm_kernel,
        out_shape=jax.ShapeDtypeStruct((nrows, d), F32),
        grid_spec=pltpu.PrefetchScalarGridSpec(
            num_scalar_prefetch=2,
            grid=(nblk,),
            in_specs=[
                pl.BlockSpec((MOE_BLOCK, d), row_map),
                pl.BlockSpec((None, d, de), w_map),
                pl.BlockSpec((None, d, de), w_map),
                pl.BlockSpec((None, de, d), w_map),
            ],
            out_specs=pl.BlockSpec((MOE_BLOCK, d), lambda i, blke, meta: (i, 0)),
            scratch_shapes=[pltpu.VMEM((d, de), BF16), pltpu.VMEM((d, de), BF16),
                            pltpu.VMEM((de, d), BF16)],
        ),
        compiler_params=_cparams(("arbitrary",), 32),
        name="moe_experts",
    )(blke, meta, xs, w1, w3, w2)


def _combine_kernel(dest_ref, ys_ref, x_ref, gate_ref, g_ref, b_ref, o_ref, buf_ref, sem, *, tr, t, alpha):
    base = pl.program_id(0) * tr

    def issue(r, c):
        for k in range(TOPK_IN_GROUP):
            _row_copy(ys_ref, dest_ref[k * t + base + r], buf_ref.at[k], r, sem).start()
        return c

    lax.fori_loop(0, tr, issue, 0)

    def drain(r, c):
        for k in range(TOPK_IN_GROUP):
            _row_copy(ys_ref, 0, buf_ref.at[k], 0, sem).wait()
        return c

    lax.fori_loop(0, tr, drain, 0)
    gate = gate_ref[...]
    f = buf_ref[0] * gate[:, 0:1] + buf_ref[1] * gate[:, 1:2]
    o_ref[...] = _layer_norm(alpha * x_ref[...] + f, g_ref[...], b_ref[...])


def _combine(dest_flat, ys, xt, gate_t, g, b, alpha):
    t, d = xt.shape
    tr = 256
    kern = functools.partial(_combine_kernel, tr=tr, t=t, alpha=alpha)
    return pl.pallas_call(
        kern,
        out_shape=jax.ShapeDtypeStruct((t, d), F32),
        grid_spec=pltpu.PrefetchScalarGridSpec(
            num_scalar_prefetch=1,
            grid=(t // tr,),
            in_specs=[
                pl.BlockSpec(memory_space=pl.ANY),
                pl.BlockSpec((tr, d), lambda i, dest: (i, 0)),
                pl.BlockSpec((tr, TOPK_IN_GROUP), lambda i, dest: (i, 0)),
                pl.BlockSpec((1, d), lambda i, dest: (0, 0)),
                pl.BlockSpec((1, d), lambda i, dest: (0, 0)),
            ],
            out_specs=pl.BlockSpec((tr, d), lambda i, dest: (i, 0)),
            scratch_shapes=[pltpu.VMEM((TOPK_IN_GROUP, tr, d), F32), pltpu.SemaphoreType.DMA],
        ),
        compiler_params=_cparams(("arbitrary",), 32),
        name="moe_combine_ln",
    )(dest_flat, ys, xt, gate_t, g.reshape(1, d), b.reshape(1, d))


def _moe_layer(xt, wg, bg, we, be, w1, w3, w2, g, b, alpha):
    t, d = xt.shape
    nblk = -(-t * TOPK_IN_GROUP // MOE_BLOCK) + N_EXPERTS
    eid, gate, rank, cnt = _router(xt, wg, bg, we, be)
    dest, blke, meta = _dest(cnt[:, 0], eid, rank, nblk)
    dest_flat = dest.reshape(-1)
    xs = _scatter(dest_flat, xt, nblk * MOE_BLOCK)
    ys = _gmm(blke, meta, xs, w1, w3, w2)
    return _combine(dest_flat, ys, xt, gate.T, g, b, alpha)


def kernel(x, conv_w_in, conv_k, conv_w_out, attn_w_in, kv_norm_g, kidx_ln_g, kidx_ln_b, attn_w_out, rel_bias, router_wg, router_bg, router_we, router_be, exp_w1, exp_w3, exp_w2, ln1_g, ln1_b, ln2_g, ln2_b):
    bsz, s, d = x.shape
    depth = ln1_g.shape[0]
    n_mixers = 2
    alpha = (2.0 * depth) ** 0.25
    bias_t = _bias_tiles(rel_bias)
    for i in range(depth):
        j = i // n_mixers
        if i % n_mixers == 0:
            x = _conv_layer(x, conv_w_in[j], conv_k[j], conv_w_out[j], ln1_g[i], ln1_b[i], alpha)
        else:
            x = _attn_layer(x, attn_w_in[j], kv_norm_g[j], kidx_ln_g[j], kidx_ln_b[j],
                            attn_w_out[j], bias_t, ln1_g[i], ln1_b[i], alpha)
        xt = _moe_layer(x.reshape(bsz * s, d), router_wg[i], router_bg[i], router_we[i],
                        router_be[i], exp_w1[i], exp_w3[i], exp_w2[i], ln2_g[i], ln2_b[i], alpha)
        x = xt.reshape(bsz, s, d)
    return x
```

```python
import functools
import math

import jax
import jax.numpy as jnp
from jax import lax
from jax.experimental import pallas as pl
from jax.experimental.pallas import tpu as pltpu

CONV_WIDTH = 3
N_HEADS = 16
KV_LATENT = 128
N_IDX_HEADS = 8
IDX_DIM = 64
TOPK_MAX = 256
Q_BLOCK = 128
N_BUCKETS = 32
MAX_DISTANCE = 128
N_GROUPS = 8
EXPERTS_PER_GROUP = 8
N_EXPERTS = N_GROUPS * EXPERTS_PER_GROUP
TOPK_IN_GROUP = 2
MOE_BLOCK = 128
LN_EPS = 1e-5
RMS_EPS = 1e-6

V7X_LANES = 128
V7X_SUBLANES = 8
V7X_VMEM_BYTES = 64 * 1024 * 1024

F32 = jnp.float32
BF16 = jnp.bfloat16
I32 = jnp.int32
NEG_INF = float("-inf")
INT32_MIN = -(2 ** 31)
LOG2E = math.log2(math.e)
BIAS_ROWS = 4 * Q_BLOCK

_NT = (((1,), (1,)), ((), ()))


def _dot(a, b):
    return jnp.dot(a, b, preferred_element_type=F32)


def _dot_nt(a, b):
    return lax.dot_general(a, b, _NT, preferred_element_type=F32)


def _layer_norm(z, g, b):
    mu = jnp.mean(z, axis=-1, keepdims=True)
    zc = z - mu
    var = jnp.mean(zc * zc, axis=-1, keepdims=True)
    return zc * lax.rsqrt(var + LN_EPS) * g + b


def _cparams(semantics, vmem_mib):
    assert vmem_mib * 1024 * 1024 < V7X_VMEM_BYTES
    return pltpu.CompilerParams(dimension_semantics=semantics,
                                vmem_limit_bytes=vmem_mib * 1024 * 1024)


def _conv_kernel(x_ref, win_ref, ck_ref, wout_ref, g_ref, b_ref, o_ref, carry_ref, gbuf_ref,
                 *, ts, d, cw, alpha):
    @pl.when(pl.program_id(1) == 0)
    def _():
        carry_ref[...] = jnp.zeros_like(carry_ref)

    x = x_ref[...]
    xb = x.astype(BF16)
    row = lax.broadcasted_iota(I32, (ts, cw), 0)
    for c in range(d // cw):
        lo, hi = c * cw, (c + 1) * cw
        bg = _dot(xb, win_ref[:, lo:hi])
        cg = _dot(xb, win_ref[:, d + lo:d + hi])
        hh = _dot(xb, win_ref[:, 2 * d + lo:2 * d + hi])
        u = cg * hh
        prev = carry_ref[:, lo:hi]
        u1 = jnp.where(row == 0, prev[7:8], pltpu.roll(u, 1, 0))
        u2 = jnp.where(row == 0, prev[6:7], jnp.where(row == 1, prev[7:8], pltpu.roll(u, 2, 0)))
        k = ck_ref[:, lo:hi]
        conv = u2 * k[0:1] + u1 * k[1:2] + u * k[2:3]
        gbuf_ref[:, lo:hi] = (bg * conv).astype(BF16)
        carry_ref[:, lo:hi] = u[ts - V7X_SUBLANES:ts]
    y = _dot(gbuf_ref[...], wout_ref[...])
    o_ref[...] = _layer_norm(alpha * x + y, g_ref[...], b_ref[...])


def _conv_layer(x, w_in, conv_k, w_out, g, b, alpha):
    bsz, s, d = x.shape
    ts, cw = 512, 512
    kern = functools.partial(_conv_kernel, ts=ts, d=d, cw=cw, alpha=alpha)
    return pl.pallas_call(
        kern,
        out_shape=jax.ShapeDtypeStruct((bsz, s, d), F32),
        grid=(bsz, s // ts),
        in_specs=[
            pl.BlockSpec((None, ts, d), lambda i, j: (i, j, 0)),
            pl.BlockSpec((d, 3 * d), lambda i, j: (0, 0)),
            pl.BlockSpec((CONV_WIDTH, d), lambda i, j: (0, 0)),
            pl.BlockSpec((d, d), lambda i, j: (0, 0)),
            pl.BlockSpec((1, d), lambda i, j: (0, 0)),
            pl.BlockSpec((1, d), lambda i, j: (0, 0)),
        ],
        out_specs=pl.BlockSpec((None, ts, d), lambda i, j: (i, j, 0)),
        scratch_shapes=[pltpu.VMEM((V7X_SUBLANES, d), F32), pltpu.VMEM((ts, d), BF16)],
        compiler_params=_cparams(("arbitrary", "arbitrary"), 48),
        name="conv_mixer_ln",
    )(x, w_in.astype(BF16), conv_k, w_out.astype(BF16), g.reshape(1, d), b.reshape(1, d))


def _proj_kernel(x_ref, wq_ref, ws_ref, ww_ref, kvg_ref, lng_ref, lnb_ref,
                 q_ref, ckv_ref, ckvt_ref, qidx_ref, kidx_ref, widxt_ref, *, idx_scale):
    xb = x_ref[...].astype(BF16)
    q = _dot(xb, wq_ref[...])
    for h in range(N_HEADS):
        q_ref[h] = q[:, h * KV_LATENT:(h + 1) * KV_LATENT].astype(BF16)
    sm = _dot(xb, ws_ref[...])
    ckv = sm[:, :KV_LATENT]
    ckv = ckv * lax.rsqrt(jnp.mean(ckv * ckv, axis=-1, keepdims=True) + RMS_EPS) * kvg_ref[...]
    ckv_ref[...] = ckv.astype(BF16)
    ckvt_ref[...] = ckv.T.astype(BF16)
    nq = N_IDX_HEADS * IDX_DIM
    qidx_ref[...] = sm[:, KV_LATENT:KV_LATENT + nq].astype(BF16)
    kidx = sm[:, KV_LATENT + nq:KV_LATENT + nq + IDX_DIM]
    kidx_ref[...] = _layer_norm(kidx, lng_ref[...], lnb_ref[...]).astype(BF16)
    widxt_ref[...] = _dot_nt(ww_ref[...], xb) * idx_scale


def _attn_proj(x, w_in, kv_g, ln_g, ln_b):
    bsz, s, d = x.shape
    ts = 512
    hq = N_HEADS * KV_LATENT
    nq = N_IDX_HEADS * IDX_DIM
    small = KV_LATENT + nq + IDX_DIM
    small_pad = -(-small // V7X_LANES) * V7X_LANES
    wq = w_in[:, :hq].astype(BF16)
    ws = jnp.pad(w_in[:, hq:hq + small], ((0, 0), (0, small_pad - small))).astype(BF16)
    ww = w_in[:, hq + small:].T.astype(BF16)
    idx_scale = (N_IDX_HEADS ** -0.5) * (IDX_DIM ** -0.5)
    kern = functools.partial(_proj_kernel, idx_scale=idx_scale)
    return pl.pallas_call(
        kern,
        out_shape=(
            jax.ShapeDtypeStruct((bsz, N_HEADS, s, KV_LATENT), BF16),
            jax.ShapeDtypeStruct((bsz, s, KV_LATENT), BF16),
            jax.ShapeDtypeStruct((bsz, KV_LATENT, s), BF16),
            jax.ShapeDtypeStruct((bsz, s, nq), BF16),
            jax.ShapeDtypeStruct((bsz, s, IDX_DIM), BF16),
            jax.ShapeDtypeStruct((bsz, N_IDX_HEADS, s), F32),
        ),
        grid=(bsz, s // ts),
        in_specs=[
            pl.BlockSpec((None, ts, d), lambda i, j: (i, j, 0)),
            pl.BlockSpec((d, hq), lambda i, j: (0, 0)),
            pl.BlockSpec((d, small_pad), lambda i, j: (0, 0)),
            pl.BlockSpec((N_IDX_HEADS, d), lambda i, j: (0, 0)),
            pl.BlockSpec((1, KV_LATENT), lambda i, j: (0, 0)),
            pl.BlockSpec((1, IDX_DIM), lambda i, j: (0, 0)),
            pl.BlockSpec((1, IDX_DIM), lambda i, j: (0, 0)),
        ],
        out_specs=(
            pl.BlockSpec((None, N_HEADS, ts, KV_LATENT), lambda i, j: (i, 0, j, 0)),
            pl.BlockSpec((None, ts, KV_LATENT), lambda i, j: (i, j, 0)),
            pl.BlockSpec((None, KV_LATENT, ts), lambda i, j: (i, 0, j)),
            pl.BlockSpec((None, ts, nq), lambda i, j: (i, j, 0)),
            pl.BlockSpec((None, ts, IDX_DIM), lambda i, j: (i, j, 0)),
            pl.BlockSpec((None, N_IDX_HEADS, ts), lambda i, j: (i, 0, j)),
        ),
        compiler_params=_cparams(("arbitrary", "arbitrary"), 48),
        name="attn_proj",
    )(x, wq, ws, ww, kv_g.reshape(1, -1), ln_g.reshape(1, -1), ln_b.reshape(1, -1))


def _bias_kernel(rb_ref, o_ref):
    rows = o_ref.shape[1]
    j = lax.broadcasted_iota(I32, (rows, Q_BLOCK), 0)
    r = lax.broadcasted_iota(I32, (rows, Q_BLOCK), 1)
    dist = 2 * Q_BLOCK + r - j
    dpos = jnp.maximum(dist, 0)
    max_exact = N_BUCKETS // 2
    d_f = jnp.maximum(dpos, 1).astype(F32)
    large = max_exact + (jnp.log(d_f / max_exact) / math.log(MAX_DISTANCE / max_exact)
                         * (N_BUCKETS - max_exact)).astype(I32)
    large = jnp.minimum(large, N_BUCKETS - 1)
    bucket = jnp.where(dpos < max_exact, dpos, large)
    for h in range(N_HEADS):
        acc = jnp.zeros((rows, Q_BLOCK), F32)
        for bk in range(N_BUCKETS):
            acc = jnp.where(bucket == bk, rb_ref[bk, h], acc)
        o_ref[h] = jnp.where(dist >= 0, (acc - rb_ref[N_BUCKETS - 1, h]) * LOG2E, 0.0)


def _bias_tiles(rel_bias):
    return pl.pallas_call(
        _bias_kernel,
        out_shape=jax.ShapeDtypeStruct((N_HEADS, BIAS_ROWS, Q_BLOCK), F32),
        in_specs=[pl.BlockSpec(memory_space=pltpu.SMEM)],
        out_specs=pl.BlockSpec(memory_space=pltpu.VMEM),
        name="rel_bias_tiles",
    )(rel_bias)


def _rows_reduce(parts, op):
    accs = [None, None]
    for c, p in enumerate(parts):
        accs[c % 2] = p if accs[c % 2] is None else op(accs[c % 2], p)
    return accs[0] if accs[1] is None else op(accs[0], accs[1])


def _attn_body(qi, q_ref, qidx_ref, widxt_ref, kidx_ref, ckv_ref, ckvt_ref, bias_ref, x_ref,
               wout_ref, g_ref, b_ref, o_ref, key_ref, madd_ref, lg_ref, eb_ref, obuf_ref,
               *, sk, first_bias_chunk, topk, alpha):
    ch = Q_BLOCK
    n = sk // ch
    t_abs = qi * Q_BLOCK + lax.broadcasted_iota(I32, (1, Q_BLOCK), 1)
    s_abs = lax.broadcasted_iota(I32, (sk, Q_BLOCK), 0)
    valid = s_abs <= t_abs

    kidx = kidx_ref[0:sk, :]
    score = jnp.zeros((sk, Q_BLOCK), F32)
    for h in range(N_IDX_HEADS):
        sh = _dot_nt(kidx, qidx_ref[:, h * IDX_DIM:(h + 1) * IDX_DIM])
        score = score + jnp.maximum(sh, 0.0) * widxt_ref[h:h + 1, :]
    score = jnp.where(score == 0.0, 0.0, score)
    score = jnp.where(valid, score, NEG_INF)
    bits = pltpu.bitcast(score, I32)
    key_ref[0:sk] = bits ^ ((bits >> 31) & 0x7FFFFFFF)

    def count(pred):
        parts = [jnp.where(pred(key_ref[c * ch:(c + 1) * ch]), 1.0, 0.0) for c in range(n)]
        return jnp.sum(_rows_reduce(parts, jnp.add), axis=0, keepdims=True)

    def bit_body(it, thr):
        cand = thr + lax.shift_left(jnp.int32(1), 31 - it)
        return jnp.where(count(lambda k: k >= cand) >= topk, cand, thr)

    thr = lax.fori_loop(0, 32, bit_body, jnp.full((1, Q_BLOCK), INT32_MIN, I32))
    cnt_ge = count(lambda k: k >= thr)
    madd_ref[0:sk] = jnp.where((key_ref[0:sk] >= thr) & valid, 0.0, NEG_INF)

    tied = jnp.where((cnt_ge > topk) & (t_abs >= topk - 1), 1.0, 0.0)

    @pl.when(jnp.max(tied) > 0.0)
    def _():
        chunk = 2 * ch
        need = topk - count(lambda k: k > thr)
        tri = jnp.where(lax.broadcasted_iota(I32, (chunk, chunk), 0)
                        >= lax.broadcasted_iota(I32, (chunk, chunk), 1), 1.0, 0.0).astype(BF16)
        run = jnp.zeros((1, Q_BLOCK), F32)
        for c in range(sk // chunk):
            kc = key_ref[c * chunk:(c + 1) * chunk]
            tie = kc == thr
            pre = _dot(tri, jnp.where(tie, 1.0, 0.0).astype(BF16)) + run
            run = pre[chunk - 1:chunk]
            vc = (c * chunk + lax.broadcasted_iota(I32, (chunk, Q_BLOCK), 0)) <= t_abs
            sel = ((kc > thr) | (tie & (pre <= need))) & vc
            madd_ref[c * chunk:(c + 1) * chunk] = jnp.where(sel, 0.0, NEG_INF)

    scale2 = (KV_LATENT ** -0.5) * LOG2E
    last_tile = BIAS_ROWS // ch - 1
    cr = 64
    nsteps = sk // cr
    npairs = N_HEADS // 2

    def stage_steps(p_logits, slot_logits, p_soft, slot_soft, m_soft):
        lg_w = eb = macc = sacc = logit = None
        if p_logits is not None:
            q2 = q_ref[pl.ds(2 * p_logits, 2)].reshape(2 * Q_BLOCK, KV_LATENT)
            logit = _dot_nt(ckv_ref[0:sk, :], q2)
            lg_w = lg_ref.at[slot_logits]
        if p_soft is not None:
            lg_r = lg_ref.at[slot_soft]
            eb = eb_ref.at[slot_soft]
        for c in range(nsteps):
            rows = slice(c * cr, (c + 1) * cr)
            if p_logits is not None:
                madd = madd_ref[rows]
                v = logit[rows] * scale2 + jnp.concatenate([madd, madd], axis=1)
                blk = (c * cr) // ch
                if blk >= first_bias_chunk:
                    off = jnp.clip(2 - qi + blk, 0, last_tile) * ch + (c * cr) % ch
                    off = pl.multiple_of(off, cr)
                    v = v + jnp.concatenate([bias_ref[2 * p_logits, pl.ds(off, cr), :],
                                             bias_ref[2 * p_logits + 1, pl.ds(off, cr), :]], axis=1)
                lg_w[rows] = v
                macc = v if macc is None else jnp.maximum(macc, v)
            if p_soft is not None:
                e = jnp.exp2(lg_r[rows] - m_soft)
                eb[rows] = e.astype(BF16)
                sacc = e if sacc is None else sacc + e
        m_new = None
        if p_logits is not None:
            m_new = jnp.max(macc, axis=0, keepdims=True)
        if p_soft is not None:
            ssum = jnp.sum(sacc, axis=0, keepdims=True)
            ot = _dot(ckvt_ref[:, 0:sk], eb[0:sk]) / ssum
            obuf_ref[2 * p_soft] = ot[:, :Q_BLOCK].T.astype(BF16)
            obuf_ref[2 * p_soft + 1] = ot[:, Q_BLOCK:].T.astype(BF16)
        return m_new

    m0 = stage_steps(0, 0, None, None, None)

    def pipe_body(j, m_even):
        m_odd = stage_steps(2 * j + 1, 1, 2 * j, 0, m_even)
        return stage_steps(2 * j + 2, 0, 2 * j + 1, 1, m_odd)

    m_even = lax.fori_loop(0, npairs // 2 - 1, pipe_body, m0)
    m_odd = stage_steps(npairs - 1, 1, npairs - 2, 0, m_even)
    stage_steps(None, None, npairs - 1, 1, m_odd)
    acc = _dot(obuf_ref[0], wout_ref[0])
    for h in range(1, N_HEADS):
        acc = acc + _dot(obuf_ref[h], wout_ref[h])
    o_ref[...] = _layer_norm(alpha * x_ref[...] + acc, g_ref[...], b_ref[...])


def _attn_kernel(*refs, nv, per, topk, alpha):
    qi = pl.program_id(1)
    for v in range(nv):
        body = functools.partial(_attn_body, qi, *refs, sk=(v + 1) * per * Q_BLOCK,
                                 first_bias_chunk=v * per - 1, topk=topk, alpha=alpha)
        pl.when(qi // per == v)(body)


def _attn_layer(x, w_in, kv_g, ln_g, ln_b, w_out, bias_t, g, b, alpha):
    bsz, s, d = x.shape
    nq = s // Q_BLOCK
    topk = min(TOPK_MAX, s // 4)
    per = -(-topk // Q_BLOCK)
    assert nq % per == 0
    nv = nq // per
    q, ckv, ckvt, qidx, kidx, widxt = _attn_proj(x, w_in, kv_g, ln_g, ln_b)
    nqi = N_IDX_HEADS * IDX_DIM
    kern = functools.partial(_attn_kernel, nv=nv, per=per, topk=topk, alpha=alpha)
    return pl.pallas_call(
        kern,
        out_shape=jax.ShapeDtypeStruct((bsz, s, d), F32),
        grid=(bsz, nq),
        in_specs=[
            pl.BlockSpec((None, N_HEADS, Q_BLOCK, KV_LATENT), lambda i, j: (i, 0, j, 0)),
            pl.BlockSpec((None, Q_BLOCK, nqi), lambda i, j: (i, j, 0)),
            pl.BlockSpec((None, N_IDX_HEADS, Q_BLOCK), lambda i, j: (i, 0, j)),
            pl.BlockSpec((None, s, IDX_DIM), lambda i, j: (i, 0, 0)),
            pl.BlockSpec((None, s, KV_LATENT), lambda i, j: (i, 0, 0)),
            pl.BlockSpec((None, KV_LATENT, s), lambda i, j: (i, 0, 0)),
            pl.BlockSpec((N_HEADS, BIAS_ROWS, Q_BLOCK), lambda i, j: (0, 0, 0)),
            pl.BlockSpec((None, Q_BLOCK, d), lambda i, j: (i, j, 0)),
            pl.BlockSpec((N_HEADS, KV_LATENT, d), lambda i, j: (0, 0, 0)),
            pl.BlockSpec((1, d), lambda i, j: (0, 0)),
            pl.BlockSpec((1, d), lambda i, j: (0, 0)),
        ],
        out_specs=pl.BlockSpec((None, Q_BLOCK, d), lambda i, j: (i, j, 0)),
        scratch_shapes=[
            pltpu.VMEM((s, Q_BLOCK), I32),
            pltpu.VMEM((s, Q_BLOCK), F32),
            pltpu.VMEM((2, s, 2 * Q_BLOCK), F32),
            pltpu.VMEM((2, s, 2 * Q_BLOCK), BF16),
            pltpu.VMEM((N_HEADS, Q_BLOCK, KV_LATENT), BF16),
        ],
        compiler_params=_cparams(("arbitrary", "arbitrary"), 48),
        name="dsa_attention_ln",
    )(q, qidx, widxt, kidx, ckv, ckvt, bias_t, x,
      w_out.astype(BF16).reshape(N_HEADS, KV_LATENT, d), g.reshape(1, d), b.reshape(1, d))


def _split_bf16(a):
    hi = a.astype(BF16)
    lo = (a - hi.astype(F32)).astype(BF16)
    return hi, lo


def _router_kernel(x_ref, w_ref, b_ref, eid_ref, gate_ref, rank_ref, cnt_ref, base_ref, u_ref, *, tt):
    @pl.when(pl.program_id(0) == 0)
    def _():
        base_ref[...] = jnp.zeros_like(base_ref)
        u_ref[...] = jnp.where(lax.broadcasted_iota(I32, (tt, tt), 0)
                               < lax.broadcasted_iota(I32, (tt, tt), 1), 1.0, 0.0).astype(BF16)

    xh, xl = _split_bf16(x_ref[...])
    wh, wl = _split_bf16(w_ref[...])
    lt = _dot_nt(wh, xh) + (_dot_nt(wh, xl) + _dot_nt(wl, xh)) + b_ref[...]

    ng, ne = N_GROUPS, EXPERTS_PER_GROUP
    gl = lt[0:ng]
    iog = lax.broadcasted_iota(I32, (ng, tt), 0).astype(F32)
    gmax = jnp.max(gl, axis=0, keepdims=True)
    gidx = jnp.min(jnp.where(gl == gmax, iog, float(ng)), axis=0, keepdims=True)
    g_gate = 1.0 / jnp.sum(jnp.exp(gl - gmax), axis=0, keepdims=True)

    el = jnp.zeros((ne, tt), F32)
    for gi in range(ng):
        el = jnp.where(gidx == float(gi), lt[ng + gi * ne:ng + (gi + 1) * ne], el)
    ioe = lax.broadcasted_iota(I32, (ne, tt), 0).astype(F32)
    m1 = jnp.max(el, axis=0, keepdims=True)
    i1 = jnp.min(jnp.where(el == m1, ioe, float(ne)), axis=0, keepdims=True)
    el2 = jnp.where(ioe == i1, NEG_INF, el)
    m2 = jnp.max(el2, axis=0, keepdims=True)
    i2 = jnp.min(jnp.where(el2 == m2, ioe, float(ne)), axis=0, keepdims=True)
    ex = jnp.exp(m2 - m1)
    p1 = 1.0 / (1.0 + ex)
    gate_ref[0:1, :] = p1 * g_gate
    gate_ref[1:2, :] = ex * p1 * g_gate
    e1 = gidx * float(ne) + i1
    e2 = gidx * float(ne) + i2
    eid_ref[0:1, :] = e1.astype(I32)
    eid_ref[1:2, :] = e2.astype(I32)

    iox = lax.broadcasted_iota(I32, (N_EXPERTS, tt), 0).astype(F32)
    oh1 = jnp.where(iox == e1, 1.0, 0.0)
    oh2 = jnp.where(iox == e2, 1.0, 0.0)
    pre1 = _dot(oh1.astype(BF16), u_ref[...])
    pre2 = _dot(oh2.astype(BF16), u_ref[...])
    tot1 = jnp.sum(oh1, axis=1, keepdims=True)
    tot2 = jnp.sum(oh2, axis=1, keepdims=True)
    base = base_ref[...]
    rank_ref[0:1, :] = jnp.sum(oh1 * (base + pre1), axis=0, keepdims=True).astype(I32)
    rank_ref[1:2, :] = jnp.sum(oh2 * (base + tot1 + pre2), axis=0, keepdims=True).astype(I32)
    base = base + tot1 + tot2
    base_ref[...] = base
    cnt_ref[...] = jnp.broadcast_to(base, cnt_ref.shape).astype(I32)


def _router(xt, wg, bg, we, be):
    t, d = xt.shape
    tt = 512
    rows = V7X_LANES
    wcat = jnp.pad(jnp.concatenate([wg, we], axis=1).T, ((0, rows - N_GROUPS - N_EXPERTS), (0, 0)))
    bcat = jnp.pad(jnp.concatenate([bg, be]), (0, rows - N_GROUPS - N_EXPERTS)).reshape(rows, 1)
    kern = functools.partial(_router_kernel, tt=tt)
    return pl.pallas_call(
        kern,
        out_shape=(
            jax.ShapeDtypeStruct((TOPK_IN_GROUP, t), I32),
            jax.ShapeDtypeStruct((TOPK_IN_GROUP, t), F32),
            jax.ShapeDtypeStruct((TOPK_IN_GROUP, t), I32),
            jax.ShapeDtypeStruct((N_EXPERTS, V7X_LANES), I32),
        ),
        grid=(t // tt,),
        in_specs=[
            pl.BlockSpec((tt, d), lambda i: (i, 0)),
            pl.BlockSpec((rows, d), lambda i: (0, 0)),
            pl.BlockSpec((rows, 1), lambda i: (0, 0)),
        ],
        out_specs=(
            pl.BlockSpec((TOPK_IN_GROUP, tt), lambda i: (0, i)),
            pl.BlockSpec((TOPK_IN_GROUP, tt), lambda i: (0, i)),
            pl.BlockSpec((TOPK_IN_GROUP, tt), lambda i: (0, i)),
            pl.BlockSpec((N_EXPERTS, V7X_LANES), lambda i: (0, 0)),
        ),
        scratch_shapes=[pltpu.VMEM((N_EXPERTS, 1), F32), pltpu.VMEM((tt, tt), BF16)],
        compiler_params=_cparams(("arbitrary",), 32),
        name="moe_router",
    )(xt, wcat, bcat)


def _dest_kernel(cnt_ref, eid_ref, rank_ref, dest_ref, blke_ref, meta_ref, pstart_ref, *, nblk):
    shift = MOE_BLOCK.bit_length() - 1

    def expert_body(e, acc):
        nb = (cnt_ref[e] + (MOE_BLOCK - 1)) >> shift
        pstart_ref[e] = acc
        b0 = acc >> shift

        def blk_body(j, c):
            blke_ref[b0 + j] = e
            return c

        lax.fori_loop(0, nb, blk_body, 0)
        return acc + (nb << shift)

    total = lax.fori_loop(0, N_EXPERTS, expert_body, jnp.int32(0))
    nused = total >> shift
    meta_ref[0] = nused
    last_e = blke_ref[nused - 1]

    def tail_body(j, c):
        blke_ref[j] = last_e
        return c

    lax.fori_loop(nused, nblk, tail_body, 0)

    def dest_body(e, dest):
        return dest + jnp.where(eid_ref[...] == e, pstart_ref[e], 0)

    dest_ref[...] = lax.fori_loop(0, N_EXPERTS, dest_body, rank_ref[...])


def _dest(cnt, eid, rank, nblk):
    t = eid.shape[1]
    kern = functools.partial(_dest_kernel, nblk=nblk)
    return pl.pallas_call(
        kern,
        out_shape=(
            jax.ShapeDtypeStruct((TOPK_IN_GROUP, t), I32),
            jax.ShapeDtypeStruct((nblk,), I32),
            jax.ShapeDtypeStruct((1,), I32),
        ),
        in_specs=[
            pl.BlockSpec(memory_space=pltpu.SMEM),
            pl.BlockSpec(memory_space=pltpu.VMEM),
            pl.BlockSpec(memory_space=pltpu.VMEM),
        ],
        out_specs=(
            pl.BlockSpec(memory_space=pltpu.VMEM),
            pl.BlockSpec(memory_space=pltpu.SMEM),
            pl.BlockSpec(memory_space=pltpu.SMEM),
        ),
        scratch_shapes=[pltpu.SMEM((N_EXPERTS,), I32)],
        name="moe_dest",
    )(cnt, eid, rank)


def _row_copy(src_ref, src_row, dst_ref, dst_row, sem):
    return pltpu.make_async_copy(src_ref.at[pl.ds(src_row, 1)], dst_ref.at[pl.ds(dst_row, 1)], sem)


def _scatter_kernel(dest_ref, x_ref, xs_in_ref, xs_ref, sem, *, tr, t):
    del xs_in_ref
    base = pl.program_id(0) * tr

    def issue(r, c):
        for k in range(TOPK_IN_GROUP):
            _row_copy(x_ref, r, xs_ref, dest_ref[k * t + base + r], sem).start()
        return c

    lax.fori_loop(0, tr, issue, 0)

    def drain(r, c):
        for k in range(TOPK_IN_GROUP):
            _row_copy(x_ref, 0, xs_ref, 0, sem).wait()
        return c

    lax.fori_loop(0, tr, drain, 0)


def _scatter(dest_flat, xt, nrows):
    t, d = xt.shape
    tr = 256
    kern = functools.partial(_scatter_kernel, tr=tr, t=t)
    return pl.pallas_call(
        kern,
        out_shape=jax.ShapeDtypeStruct((nrows, d), F32),
        grid_spec=pltpu.PrefetchScalarGridSpec(
            num_scalar_prefetch=1,
            grid=(t // tr,),
            in_specs=[
                pl.BlockSpec((tr, d), lambda i, dest: (i, 0)),
                pl.BlockSpec(memory_space=pl.ANY),
            ],
            out_specs=pl.BlockSpec(memory_space=pl.ANY),
            scratch_shapes=[pltpu.SemaphoreType.DMA],
        ),
        input_output_aliases={2: 0},
        compiler_params=_cparams(("arbitrary",), 32),
        name="moe_scatter_rows",
    )(dest_flat, xt, jnp.zeros((nrows, d), F32))


def _gmm_kernel(blke_ref, meta_ref, xs_ref, w1_ref, w3_ref, w2_ref, ys_ref, w1b_ref, w3b_ref, w2b_ref):
    nb = pl.program_id(0)

    @pl.when(nb < meta_ref[0])
    def _():
        @pl.when((nb == 0) | (blke_ref[nb] != blke_ref[jnp.maximum(nb - 1, 0)]))
        def _():
            w1b_ref[...] = w1_ref[...].astype(BF16)
            w3b_ref[...] = w3_ref[...].astype(BF16)
            w2b_ref[...] = w2_ref[...].astype(BF16)

        xb = xs_ref[...].astype(BF16)
        h1 = _dot(xb, w1b_ref[...])
        h3 = _dot(xb, w3b_ref[...])
        hh = (h1 * jax.nn.sigmoid(h1) * h3).astype(BF16)
        ys_ref[...] = _dot(hh, w2b_ref[...])

    @pl.when(nb >= meta_ref[0])
    def _():
        ys_ref[...] = jnp.zeros_like(ys_ref)


def _gmm(blke, meta, xs, w1, w3, w2, layer):
    nrows, d = xs.shape
    de = w1.shape[-1]
    nblk = nrows // MOE_BLOCK

    def row_map(i, blke, meta):
        return (jnp.minimum(i, meta[0] - 1), 0)

    def w_map(i, blke, meta):
        return (layer, blke[i], 0, 0)

    return pl.pallas_call(
        _gmm_kernel,
        out_shape=jax.ShapeDtypeStruct((nrows, d), F32),
        grid_spec=pltpu.PrefetchScalarGridSpec(
            num_scalar_prefetch=2,
            grid=(nblk,),
            in_specs=[
                pl.BlockSpec((MOE_BLOCK, d), row_map),
                pl.BlockSpec((None, None, d, de), w_map),
                pl.BlockSpec((None, None, d, de), w_map),
                pl.BlockSpec((None, None, de, d), w_map),
            ],
            out_specs=pl.BlockSpec((MOE_BLOCK, d), lambda i, blke, meta: (i, 0)),
            scratch_shapes=[pltpu.VMEM((d, de), BF16), pltpu.VMEM((d, de), BF16),
                            pltpu.VMEM((de, d), BF16)],
        ),
        compiler_params=_cparams(("arbitrary",), 32),
        name="moe_experts",
    )(blke, meta, xs, w1, w3, w2)


def _combine_kernel(dest_ref, ys_ref, x_ref, gate_ref, g_ref, b_ref, o_ref, buf_ref, sem, *, tr, t, alpha):
    base = pl.program_id(0) * tr

    def issue(r, c):
        for k in range(TOPK_IN_GROUP):
            _row_copy(ys_ref, dest_ref[k * t + base + r], buf_ref.at[k], r, sem).start()
        return c

    lax.fori_loop(0, tr, issue, 0)

    def drain(r, c):
        for k in range(TOPK_IN_GROUP):
            _row_copy(ys_ref, 0, buf_ref.at[k], 0, sem).wait()
        return c

    lax.fori_loop(0, tr, drain, 0)
    gate = gate_ref[...]
    f = buf_ref[0] * gate[:, 0:1] + buf_ref[1] * gate[:, 1:2]
    o_ref[...] = _layer_norm(alpha * x_ref[...] + f, g_ref[...], b_ref[...])


def _combine(dest_flat, ys, xt, gate_t, g, b, alpha):
    t, d = xt.shape
    tr = 256
    kern = functools.partial(_combine_kernel, tr=tr, t=t, alpha=alpha)
    return pl.pallas_call(
        kern,
        out_shape=jax.ShapeDtypeStruct((t, d), F32),
        grid_spec=pltpu.PrefetchScalarGridSpec(
            num_scalar_prefetch=1,
            grid=(t // tr,),
            in_specs=[
                pl.BlockSpec(memory_space=pl.ANY),
                pl.BlockSpec((tr, d), lambda i, dest: (i, 0)),
                pl.BlockSpec((tr, TOPK_IN_GROUP), lambda i, dest: (i, 0)),
                pl.BlockSpec((1, d), lambda i, dest: (0, 0)),
                pl.BlockSpec((1, d), lambda i, dest: (0, 0)),
            ],
            out_specs=pl.BlockSpec((tr, d), lambda i, dest: (i, 0)),
            scratch_shapes=[pltpu.VMEM((TOPK_IN_GROUP, tr, d), F32), pltpu.SemaphoreType.DMA],
        ),
        compiler_params=_cparams(("arbitrary",), 32),
        name="moe_combine_ln",
    )(dest_flat, ys, xt, gate_t, g.reshape(1, d), b.reshape(1, d))


def _moe_layer(xt, wg, bg, we, be, w1, w3, w2, layer, g, b, alpha):
    t, d = xt.shape
    nblk = -(-t * TOPK_IN_GROUP // MOE_BLOCK) + N_EXPERTS
    eid, gate, rank, cnt = _router(xt, wg, bg, we, be)
    dest, blke, meta = _dest(cnt[:, 0], eid, rank, nblk)
    dest_flat = dest.reshape(-1)
    xs = _scatter(dest_flat, xt, nblk * MOE_BLOCK)
    ys = _gmm(blke, meta, xs, w1, w3, w2, layer)
    return _combine(dest_flat, ys, xt, gate.T, g, b, alpha)


def kernel(x, conv_w_in, conv_k, conv_w_out, attn_w_in, kv_norm_g, kidx_ln_g, kidx_ln_b, attn_w_out, rel_bias, router_wg, router_bg, router_we, router_be, exp_w1, exp_w3, exp_w2, ln1_g, ln1_b, ln2_g, ln2_b):
    bsz, s, d = x.shape
    depth = ln1_g.shape[0]
    n_mixers = 2
    alpha = (2.0 * depth) ** 0.25
    bias_t = _bias_tiles(rel_bias)
    for i in range(depth):
        j = i // n_mixers
        if i % n_mixers == 0:
            x = _conv_layer(x, conv_w_in[j], conv_k[j], conv_w_out[j], ln1_g[i], ln1_b[i], alpha)
        else:
            x = _attn_layer(x, attn_w_in[j], kv_norm_g[j], kidx_ln_g[j], kidx_ln_b[j],
                            attn_w_out[j], bias_t, ln1_g[i], ln1_b[i], alpha)
        xt = _moe_layer(x.reshape(bsz * s, d), router_wg[i], router_bg[i], router_we[i],
                        router_be[i], exp_w1, exp_w3, exp_w2, i, ln2_g[i], ln2_b[i], alpha)
        x = xt.reshape(bsz, s, d)
    return x
```

```python
import functools
import math

import jax
import jax.numpy as jnp
from jax import lax
from jax.experimental import pallas as pl
from jax.experimental.pallas import tpu as pltpu

CONV_WIDTH = 3
N_HEADS = 16
KV_LATENT = 128
N_IDX_HEADS = 8
IDX_DIM = 64
TOPK_MAX = 256
Q_BLOCK = 128
N_BUCKETS = 32
MAX_DISTANCE = 128
N_GROUPS = 8
EXPERTS_PER_GROUP = 8
N_EXPERTS = N_GROUPS * EXPERTS_PER_GROUP
TOPK_IN_GROUP = 2
MOE_BLOCK = 256
LN_EPS = 1e-5
RMS_EPS = 1e-6

V7X_LANES = 128
V7X_SUBLANES = 8
V7X_VMEM_BYTES = 64 * 1024 * 1024

F32 = jnp.float32
BF16 = jnp.bfloat16
I32 = jnp.int32
NEG_INF = float("-inf")
INT32_MIN = -(2 ** 31)
LOG2E = math.log2(math.e)
LOGIT_SCALE2 = (KV_LATENT ** -0.5) * LOG2E
BIAS_ROWS = 4 * Q_BLOCK

_NT = (((1,), (1,)), ((), ()))


def _dot(a, b):
    return jnp.dot(a, b, preferred_element_type=F32)


def _dot_nt(a, b):
    return lax.dot_general(a, b, _NT, preferred_element_type=F32)


def _layer_norm(z, g, b):
    mu = jnp.mean(z, axis=-1, keepdims=True)
    zc = z - mu
    var = jnp.mean(zc * zc, axis=-1, keepdims=True)
    return zc * lax.rsqrt(var + LN_EPS) * g + b


def _cparams(semantics, vmem_mib):
    assert vmem_mib * 1024 * 1024 < V7X_VMEM_BYTES
    return pltpu.CompilerParams(dimension_semantics=semantics,
                                vmem_limit_bytes=vmem_mib * 1024 * 1024)


def _conv_kernel(x_ref, win_ref, ck_ref, wout_ref, g_ref, b_ref, o_ref, carry_ref, gbuf_ref,
                 *, ts, d, cw, alpha):
    @pl.when(pl.program_id(1) == 0)
    def _():
        carry_ref[...] = jnp.zeros_like(carry_ref)

    x = x_ref[...]
    xb = x.astype(BF16)
    row = lax.broadcasted_iota(I32, (ts, cw), 0)
    for c in range(d // cw):
        lo, hi = c * cw, (c + 1) * cw
        bg = _dot(xb, win_ref[:, lo:hi])
        cg = _dot(xb, win_ref[:, d + lo:d + hi])
        hh = _dot(xb, win_ref[:, 2 * d + lo:2 * d + hi])
        u = cg * hh
        prev = carry_ref[:, lo:hi]
        u1 = jnp.where(row == 0, prev[7:8], pltpu.roll(u, 1, 0))
        u2 = jnp.where(row == 0, prev[6:7], jnp.where(row == 1, prev[7:8], pltpu.roll(u, 2, 0)))
        k = ck_ref[:, lo:hi]
        conv = u2 * k[0:1] + u1 * k[1:2] + u * k[2:3]
        gbuf_ref[:, lo:hi] = (bg * conv).astype(BF16)
        carry_ref[:, lo:hi] = u[ts - V7X_SUBLANES:ts]
    y = _dot(gbuf_ref[...], wout_ref[...])
    o_ref[...] = _layer_norm(alpha * x + y, g_ref[...], b_ref[...])


def _conv_layer(x, w_in, conv_k, w_out, g, b, alpha):
    bsz, s, d = x.shape
    ts, cw = 512, 512
    kern = functools.partial(_conv_kernel, ts=ts, d=d, cw=cw, alpha=alpha)
    return pl.pallas_call(
        kern,
        out_shape=jax.ShapeDtypeStruct((bsz, s, d), F32),
        grid=(bsz, s // ts),
        in_specs=[
            pl.BlockSpec((None, ts, d), lambda i, j: (i, j, 0)),
            pl.BlockSpec((d, 3 * d), lambda i, j: (0, 0)),
            pl.BlockSpec((CONV_WIDTH, d), lambda i, j: (0, 0)),
            pl.BlockSpec((d, d), lambda i, j: (0, 0)),
            pl.BlockSpec((1, d), lambda i, j: (0, 0)),
            pl.BlockSpec((1, d), lambda i, j: (0, 0)),
        ],
        out_specs=pl.BlockSpec((None, ts, d), lambda i, j: (i, j, 0)),
        scratch_shapes=[pltpu.VMEM((V7X_SUBLANES, d), F32), pltpu.VMEM((ts, d), BF16)],
        compiler_params=_cparams(("arbitrary", "arbitrary"), 48),
        name="conv_mixer_ln",
    )(x, w_in.astype(BF16), conv_k, w_out.astype(BF16), g.reshape(1, d), b.reshape(1, d))


def _proj_kernel(x_ref, wq_ref, ws_ref, ww_ref, kvg_ref, lng_ref, lnb_ref,
                 q_ref, ckv_ref, ckvt_ref, qidx_ref, kidx_ref, widxt_ref, *, idx_scale):
    xb = x_ref[...].astype(BF16)
    q = _dot(xb, wq_ref[...]) * LOGIT_SCALE2
    for h in range(N_HEADS):
        q_ref[h] = q[:, h * KV_LATENT:(h + 1) * KV_LATENT].astype(BF16)
    sm = _dot(xb, ws_ref[...])
    ckv = sm[:, :KV_LATENT]
    ckv = ckv * lax.rsqrt(jnp.mean(ckv * ckv, axis=-1, keepdims=True) + RMS_EPS) * kvg_ref[...]
    ckv_ref[...] = ckv.astype(BF16)
    ckvt_ref[...] = ckv.T.astype(BF16)
    nq = N_IDX_HEADS * IDX_DIM
    qidx_ref[...] = sm[:, KV_LATENT:KV_LATENT + nq].astype(BF16)
    kidx = sm[:, KV_LATENT + nq:KV_LATENT + nq + IDX_DIM]
    kidx_ref[...] = _layer_norm(kidx, lng_ref[...], lnb_ref[...]).astype(BF16)
    widxt_ref[...] = _dot_nt(ww_ref[...], xb) * idx_scale


def _attn_proj(x, w_in, kv_g, ln_g, ln_b):
    bsz, s, d = x.shape
    ts = 512
    hq = N_HEADS * KV_LATENT
    nq = N_IDX_HEADS * IDX_DIM
    small = KV_LATENT + nq + IDX_DIM
    small_pad = -(-small // V7X_LANES) * V7X_LANES
    wq = w_in[:, :hq].astype(BF16)
    ws = jnp.pad(w_in[:, hq:hq + small], ((0, 0), (0, small_pad - small))).astype(BF16)
    ww = w_in[:, hq + small:].T.astype(BF16)
    idx_scale = (N_IDX_HEADS ** -0.5) * (IDX_DIM ** -0.5)
    kern = functools.partial(_proj_kernel, idx_scale=idx_scale)
    return pl.pallas_call(
        kern,
        out_shape=(
            jax.ShapeDtypeStruct((bsz, N_HEADS, s, KV_LATENT), BF16),
            jax.ShapeDtypeStruct((bsz, s, KV_LATENT), BF16),
            jax.ShapeDtypeStruct((bsz, KV_LATENT, s), BF16),
            jax.ShapeDtypeStruct((bsz, s, nq), BF16),
            jax.ShapeDtypeStruct((bsz, s, IDX_DIM), BF16),
            jax.ShapeDtypeStruct((bsz, N_IDX_HEADS, s), F32),
        ),
        grid=(bsz, s // ts),
        in_specs=[
            pl.BlockSpec((None, ts, d), lambda i, j: (i, j, 0)),
            pl.BlockSpec((d, hq), lambda i, j: (0, 0)),
            pl.BlockSpec((d, small_pad), lambda i, j: (0, 0)),
            pl.BlockSpec((N_IDX_HEADS, d), lambda i, j: (0, 0)),
            pl.BlockSpec((1, KV_LATENT), lambda i, j: (0, 0)),
            pl.BlockSpec((1, IDX_DIM), lambda i, j: (0, 0)),
            pl.BlockSpec((1, IDX_DIM), lambda i, j: (0, 0)),
        ],
        out_specs=(
            pl.BlockSpec((None, N_HEADS, ts, KV_LATENT), lambda i, j: (i, 0, j, 0)),
            pl.BlockSpec((None, ts, KV_LATENT), lambda i, j: (i, j, 0)),
            pl.BlockSpec((None, KV_LATENT, ts), lambda i, j: (i, 0, j)),
            pl.BlockSpec((None, ts, nq), lambda i, j: (i, j, 0)),
            pl.BlockSpec((None, ts, IDX_DIM), lambda i, j: (i, j, 0)),
            pl.BlockSpec((None, N_IDX_HEADS, ts), lambda i, j: (i, 0, j)),
        ),
        compiler_params=_cparams(("arbitrary", "arbitrary"), 48),
        name="attn_proj",
    )(x, wq, ws, ww, kv_g.reshape(1, -1), ln_g.reshape(1, -1), ln_b.reshape(1, -1))


def _bias_kernel(rb_ref, o_ref):
    rows = o_ref.shape[1]
    j = lax.broadcasted_iota(I32, (rows, Q_BLOCK), 0)
    r = lax.broadcasted_iota(I32, (rows, Q_BLOCK), 1)
    dist = 2 * Q_BLOCK + r - j
    dpos = jnp.maximum(dist, 0)
    max_exact = N_BUCKETS // 2
    d_f = jnp.maximum(dpos, 1).astype(F32)
    large = max_exact + (jnp.log(d_f / max_exact) / math.log(MAX_DISTANCE / max_exact)
                         * (N_BUCKETS - max_exact)).astype(I32)
    large = jnp.minimum(large, N_BUCKETS - 1)
    bucket = jnp.where(dpos < max_exact, dpos, large)
    for h in range(N_HEADS):
        acc = jnp.zeros((rows, Q_BLOCK), F32)
        for bk in range(N_BUCKETS):
            acc = jnp.where(bucket == bk, rb_ref[bk, h], acc)
        o_ref[h] = jnp.where(dist >= 0, (acc - rb_ref[N_BUCKETS - 1, h]) * LOG2E, 0.0)


def _bias_tiles(rel_bias):
    return pl.pallas_call(
        _bias_kernel,
        out_shape=jax.ShapeDtypeStruct((N_HEADS, BIAS_ROWS, Q_BLOCK), F32),
        in_specs=[pl.BlockSpec(memory_space=pltpu.SMEM)],
        out_specs=pl.BlockSpec(memory_space=pltpu.VMEM),
        name="rel_bias_tiles",
    )(rel_bias)


def _rows_reduce(parts, op):
    accs = [None, None]
    for c, p in enumerate(parts):
        accs[c % 2] = p if accs[c % 2] is None else op(accs[c % 2], p)
    return accs[0] if accs[1] is None else op(accs[0], accs[1])


def _attn_body(qi, q_ref, qidx_ref, widxt_ref, kidx_ref, ckv_ref, ckvt_ref, bias_ref, x_ref,
               wout_ref, g_ref, b_ref, o_ref, key_ref, madd_ref, lg_ref, eb_ref, obuf_ref,
               *, sk, first_bias_chunk, topk, alpha):
    ch = Q_BLOCK
    n = sk // ch
    t_abs = qi * Q_BLOCK + lax.broadcasted_iota(I32, (1, Q_BLOCK), 1)
    s_abs = lax.broadcasted_iota(I32, (sk, Q_BLOCK), 0)
    valid = s_abs <= t_abs

    kidx = kidx_ref[0:sk, :]
    score = jnp.zeros((sk, Q_BLOCK), F32)
    for h in range(N_IDX_HEADS):
        sh = _dot_nt(kidx, qidx_ref[:, h * IDX_DIM:(h + 1) * IDX_DIM])
        score = score + jnp.maximum(sh, 0.0) * widxt_ref[h:h + 1, :]
    score = jnp.where(score == 0.0, 0.0, score)
    score = jnp.where(valid, score, NEG_INF)
    bits = pltpu.bitcast(score, I32)
    key_ref[0:sk] = bits ^ ((bits >> 31) & 0x7FFFFFFF)

    def count(pred):
        parts = [jnp.where(pred(key_ref[c * ch:(c + 1) * ch]), 1.0, 0.0) for c in range(n)]
        return jnp.sum(_rows_reduce(parts, jnp.add), axis=0, keepdims=True)

    def bit_body(it, thr):
        cand = thr + lax.shift_left(jnp.int32(1), 31 - it)
        return jnp.where(count(lambda k: k >= cand) >= topk, cand, thr)

    thr = lax.fori_loop(0, 32, bit_body, jnp.full((1, Q_BLOCK), INT32_MIN, I32))
    cnt_ge = count(lambda k: k >= thr)
    madd_ref[0:sk] = jnp.where((key_ref[0:sk] >= thr) & valid, 0.0, NEG_INF)

    tied = jnp.where((cnt_ge > topk) & (t_abs >= topk - 1), 1.0, 0.0)

    @pl.when(jnp.max(tied) > 0.0)
    def _():
        chunk = 2 * ch
        need = topk - count(lambda k: k > thr)
        tri = jnp.where(lax.broadcasted_iota(I32, (chunk, chunk), 0)
                        >= lax.broadcasted_iota(I32, (chunk, chunk), 1), 1.0, 0.0).astype(BF16)
        run = jnp.zeros((1, Q_BLOCK), F32)
        for c in range(sk // chunk):
            kc = key_ref[c * chunk:(c + 1) * chunk]
            tie = kc == thr
            pre = _dot(tri, jnp.where(tie, 1.0, 0.0).astype(BF16)) + run
            run = pre[chunk - 1:chunk]
            vc = (c * chunk + lax.broadcasted_iota(I32, (chunk, Q_BLOCK), 0)) <= t_abs
            sel = ((kc > thr) | (tie & (pre <= need))) & vc
            madd_ref[c * chunk:(c + 1) * chunk] = jnp.where(sel, 0.0, NEG_INF)

    last_tile = BIAS_ROWS // ch - 1
    cr = 64
    nsteps = sk // cr
    npairs = N_HEADS // 2

    def stage_steps(p_logits, slot_logits, p_soft, slot_soft, m_soft):
        lg_w = eb = macc = sacc = logit = None
        if p_logits is not None:
            q2 = q_ref[pl.ds(2 * p_logits, 2)].reshape(2 * Q_BLOCK, KV_LATENT)
            logit = _dot_nt(ckv_ref[0:sk, :], q2)
            lg_w = lg_ref.at[slot_logits]
        if p_soft is not None:
            lg_r = lg_ref.at[slot_soft]
            eb = eb_ref.at[slot_soft]
        for c in range(nsteps):
            rows = slice(c * cr, (c + 1) * cr)
            if p_logits is not None:
                madd = madd_ref[rows]
                v = logit[rows] + jnp.concatenate([madd, madd], axis=1)
                blk = (c * cr) // ch
                if blk >= first_bias_chunk:
                    off = jnp.clip(2 - qi + blk, 0, last_tile) * ch + (c * cr) % ch
                    off = pl.multiple_of(off, cr)
                    v = v + jnp.concatenate([bias_ref[2 * p_logits, pl.ds(off, cr), :],
                                             bias_ref[2 * p_logits + 1, pl.ds(off, cr), :]], axis=1)
                lg_w[rows] = v
                macc = v if macc is None else jnp.maximum(macc, v)
            if p_soft is not None:
                e = jnp.exp2(lg_r[rows] - m_soft)
                eb[rows] = e.astype(BF16)
                sacc = e if sacc is None else sacc + e
        m_new = None
        if p_logits is not None:
            m_new = jnp.max(macc, axis=0, keepdims=True)
        if p_soft is not None:
            ssum = jnp.sum(sacc, axis=0, keepdims=True)
            ot = _dot(ckvt_ref[:, 0:sk], eb[0:sk]) / ssum
            obuf_ref[2 * p_soft] = ot[:, :Q_BLOCK].T.astype(BF16)
            obuf_ref[2 * p_soft + 1] = ot[:, Q_BLOCK:].T.astype(BF16)
        return m_new

    m0 = stage_steps(0, 0, None, None, None)

    def pipe_body(j, m_even):
        m_odd = stage_steps(2 * j + 1, 1, 2 * j, 0, m_even)
        return stage_steps(2 * j + 2, 0, 2 * j + 1, 1, m_odd)

    m_even = lax.fori_loop(0, npairs // 2 - 1, pipe_body, m0)
    m_odd = stage_steps(npairs - 1, 1, npairs - 2, 0, m_even)
    stage_steps(None, None, npairs - 1, 1, m_odd)
    o_all = jnp.concatenate([obuf_ref[h] for h in range(N_HEADS)], axis=1)
    acc = _dot(o_all, wout_ref[...])
    o_ref[...] = _layer_norm(alpha * x_ref[...] + acc, g_ref[...], b_ref[...])


def _attn_kernel(*refs, nv, per, topk, alpha):
    qi = pl.program_id(1)
    for v in range(nv):
        body = functools.partial(_attn_body, qi, *refs, sk=(v + 1) * per * Q_BLOCK,
                                 first_bias_chunk=v * per - 1, topk=topk, alpha=alpha)
        pl.when(qi // per == v)(body)


def _attn_layer(x, w_in, kv_g, ln_g, ln_b, w_out, bias_t, g, b, alpha):
    bsz, s, d = x.shape
    nq = s // Q_BLOCK
    topk = min(TOPK_MAX, s // 4)
    per = -(-topk // Q_BLOCK)
    assert nq % per == 0
    nv = nq // per
    q, ckv, ckvt, qidx, kidx, widxt = _attn_proj(x, w_in, kv_g, ln_g, ln_b)
    nqi = N_IDX_HEADS * IDX_DIM
    kern = functools.partial(_attn_kernel, nv=nv, per=per, topk=topk, alpha=alpha)
    return pl.pallas_call(
        kern,
        out_shape=jax.ShapeDtypeStruct((bsz, s, d), F32),
        grid=(bsz, nq),
        in_specs=[
            pl.BlockSpec((None, N_HEADS, Q_BLOCK, KV_LATENT), lambda i, j: (i, 0, j, 0)),
            pl.BlockSpec((None, Q_BLOCK, nqi), lambda i, j: (i, j, 0)),
            pl.BlockSpec((None, N_IDX_HEADS, Q_BLOCK), lambda i, j: (i, 0, j)),
            pl.BlockSpec((None, s, IDX_DIM), lambda i, j: (i, 0, 0)),
            pl.BlockSpec((None, s, KV_LATENT), lambda i, j: (i, 0, 0)),
            pl.BlockSpec((None, KV_LATENT, s), lambda i, j: (i, 0, 0)),
            pl.BlockSpec((N_HEADS, BIAS_ROWS, Q_BLOCK), lambda i, j: (0, 0, 0)),
            pl.BlockSpec((None, Q_BLOCK, d), lambda i, j: (i, j, 0)),
            pl.BlockSpec((N_HEADS * KV_LATENT, d), lambda i, j: (0, 0)),
            pl.BlockSpec((1, d), lambda i, j: (0, 0)),
            pl.BlockSpec((1, d), lambda i, j: (0, 0)),
        ],
        out_specs=pl.BlockSpec((None, Q_BLOCK, d), lambda i, j: (i, j, 0)),
        scratch_shapes=[
            pltpu.VMEM((s, Q_BLOCK), I32),
            pltpu.VMEM((s, Q_BLOCK), F32),
            pltpu.VMEM((2, s, 2 * Q_BLOCK), F32),
            pltpu.VMEM((2, s, 2 * Q_BLOCK), BF16),
            pltpu.VMEM((N_HEADS, Q_BLOCK, KV_LATENT), BF16),
        ],
        compiler_params=_cparams(("arbitrary", "arbitrary"), 48),
        name="dsa_attention_ln",
    )(q, qidx, widxt, kidx, ckv, ckvt, bias_t, x,
      w_out.astype(BF16), g.reshape(1, d), b.reshape(1, d))


def _split_bf16(a):
    hi = a.astype(BF16)
    lo = (a - hi.astype(F32)).astype(BF16)
    return hi, lo


def _router_kernel(x_ref, w_ref, b_ref, eid_ref, gate_ref, rank_ref, cnt_ref, base_ref, u_ref, *, tt):
    @pl.when(pl.program_id(0) == 0)
    def _():
        base_ref[...] = jnp.zeros_like(base_ref)
        u_ref[...] = jnp.where(lax.broadcasted_iota(I32, (tt, tt), 0)
                               < lax.broadcasted_iota(I32, (tt, tt), 1), 1.0, 0.0).astype(BF16)

    xh, xl = _split_bf16(x_ref[...])
    wh, wl = _split_bf16(w_ref[...])
    lt = _dot_nt(wh, xh) + (_dot_nt(wh, xl) + _dot_nt(wl, xh)) + b_ref[...]

    ng, ne = N_GROUPS, EXPERTS_PER_GROUP
    gl = lt[0:ng]
    iog = lax.broadcasted_iota(I32, (ng, tt), 0).astype(F32)
    gmax = jnp.max(gl, axis=0, keepdims=True)
    gidx = jnp.min(jnp.where(gl == gmax, iog, float(ng)), axis=0, keepdims=True)
    g_gate = 1.0 / jnp.sum(jnp.exp(gl - gmax), axis=0, keepdims=True)

    el = jnp.zeros((ne, tt), F32)
    for gi in range(ng):
        el = jnp.where(gidx == float(gi), lt[ng + gi * ne:ng + (gi + 1) * ne], el)
    ioe = lax.broadcasted_iota(I32, (ne, tt), 0).astype(F32)
    m1 = jnp.max(el, axis=0, keepdims=True)
    i1 = jnp.min(jnp.where(el == m1, ioe, float(ne)), axis=0, keepdims=True)
    el2 = jnp.where(ioe == i1, NEG_INF, el)
    m2 = jnp.max(el2, axis=0, keepdims=True)
    i2 = jnp.min(jnp.where(el2 == m2, ioe, float(ne)), axis=0, keepdims=True)
    ex = jnp.exp(m2 - m1)
    p1 = 1.0 / (1.0 + ex)
    gate_ref[0:1, :] = p1 * g_gate
    gate_ref[1:2, :] = ex * p1 * g_gate
    e1 = gidx * float(ne) + i1
    e2 = gidx * float(ne) + i2
    eid_ref[0:1, :] = e1.astype(I32)
    eid_ref[1:2, :] = e2.astype(I32)

    iox = lax.broadcasted_iota(I32, (N_EXPERTS, tt), 0).astype(F32)
    oh1 = jnp.where(iox == e1, 1.0, 0.0)
    oh2 = jnp.where(iox == e2, 1.0, 0.0)
    pre1 = _dot(oh1.astype(BF16), u_ref[...])
    pre2 = _dot(oh2.astype(BF16), u_ref[...])
    tot1 = jnp.sum(oh1, axis=1, keepdims=True)
    tot2 = jnp.sum(oh2, axis=1, keepdims=True)
    base = base_ref[...]
    rank_ref[0:1, :] = jnp.sum(oh1 * (base + pre1), axis=0, keepdims=True).astype(I32)
    rank_ref[1:2, :] = jnp.sum(oh2 * (base + tot1 + pre2), axis=0, keepdims=True).astype(I32)
    base = base + tot1 + tot2
    base_ref[...] = base
    cnt_ref[...] = jnp.broadcast_to(base, cnt_ref.shape).astype(I32)


def _router(xt, wg, bg, we, be):
    t, d = xt.shape
    tt = 512
    rows = V7X_LANES
    wcat = jnp.pad(jnp.concatenate([wg, we], axis=1).T, ((0, rows - N_GROUPS - N_EXPERTS), (0, 0)))
    bcat = jnp.pad(jnp.concatenate([bg, be]), (0, rows - N_GROUPS - N_EXPERTS)).reshape(rows, 1)
    kern = functools.partial(_router_kernel, tt=tt)
    return pl.pallas_call(
        kern,
        out_shape=(
            jax.ShapeDtypeStruct((TOPK_IN_GROUP, t), I32),
            jax.ShapeDtypeStruct((TOPK_IN_GROUP, t), F32),
            jax.ShapeDtypeStruct((TOPK_IN_GROUP, t), I32),
            jax.ShapeDtypeStruct((N_EXPERTS, V7X_LANES), I32),
        ),
        grid=(t // tt,),
        in_specs=[
            pl.BlockSpec((tt, d), lambda i: (i, 0)),
            pl.BlockSpec((rows, d), lambda i: (0, 0)),
            pl.BlockSpec((rows, 1), lambda i: (0, 0)),
        ],
        out_specs=(
            pl.BlockSpec((TOPK_IN_GROUP, tt), lambda i: (0, i)),
            pl.BlockSpec((TOPK_IN_GROUP, tt), lambda i: (0, i)),
            pl.BlockSpec((TOPK_IN_GROUP, tt), lambda i: (0, i)),
            pl.BlockSpec((N_EXPERTS, V7X_LANES), lambda i: (0, 0)),
        ),
        scratch_shapes=[pltpu.VMEM((N_EXPERTS, 1), F32), pltpu.VMEM((tt, tt), BF16)],
        compiler_params=_cparams(("arbitrary",), 32),
        name="moe_router",
    )(xt, wcat, bcat)


def _dest_kernel(cnt_ref, eid_ref, rank_ref, dest_ref, blke_ref, meta_ref, pstart_ref, *, nblk):
    shift = MOE_BLOCK.bit_length() - 1

    def expert_body(e, acc):
        nb = (cnt_ref[e] + (MOE_BLOCK - 1)) >> shift
        pstart_ref[e] = acc
        b0 = acc >> shift

        def blk_body(j, c):
            blke_ref[b0 + j] = e
            return c

        lax.fori_loop(0, nb, blk_body, 0)
        return acc + (nb << shift)

    total = lax.fori_loop(0, N_EXPERTS, expert_body, jnp.int32(0))
    nused = total >> shift
    meta_ref[0] = nused
    last_e = blke_ref[nused - 1]

    def tail_body(j, c):
        blke_ref[j] = last_e
        return c

    lax.fori_loop(nused, nblk, tail_body, 0)

    def dest_body(e, dest):
        return dest + jnp.where(eid_ref[...] == e, pstart_ref[e], 0)

    dest_ref[...] = lax.fori_loop(0, N_EXPERTS, dest_body, rank_ref[...])


def _dest(cnt, eid, rank, nblk):
    t = eid.shape[1]
    kern = functools.partial(_dest_kernel, nblk=nblk)
    return pl.pallas_call(
        kern,
        out_shape=(
            jax.ShapeDtypeStruct((TOPK_IN_GROUP, t), I32),
            jax.ShapeDtypeStruct((nblk,), I32),
            jax.ShapeDtypeStruct((1,), I32),
        ),
        in_specs=[
            pl.BlockSpec(memory_space=pltpu.SMEM),
            pl.BlockSpec(memory_space=pltpu.VMEM),
            pl.BlockSpec(memory_space=pltpu.VMEM),
        ],
        out_specs=(
            pl.BlockSpec(memory_space=pltpu.VMEM),
            pl.BlockSpec(memory_space=pltpu.SMEM),
            pl.BlockSpec(memory_space=pltpu.SMEM),
        ),
        scratch_shapes=[pltpu.SMEM((N_EXPERTS,), I32)],
        name="moe_dest",
    )(cnt, eid, rank)


def _row_copy(src_ref, src_row, dst_ref, dst_row, sem):
    return pltpu.make_async_copy(src_ref.at[pl.ds(src_row, 1)], dst_ref.at[pl.ds(dst_row, 1)], sem)


def _wait_rows(hbm_ref, vmem_rows_ref, sem):
    n = vmem_rows_ref.shape[0]
    pltpu.make_async_copy(hbm_ref.at[pl.ds(0, n)], vmem_rows_ref, sem).wait()


def _scatter_kernel(dest_ref, x_ref, xs_in_ref, xs_ref, stage_ref, sem, *, tr, t, nsteps):
    del xs_in_ref
    i = pl.program_id(0)
    slot = i % 2
    base = i * tr

    def drain(s):
        for _ in range(TOPK_IN_GROUP):
            _wait_rows(xs_ref, stage_ref.at[s], sem.at[s])

    @pl.when(i >= 2)
    def _():
        drain(slot)

    stage_ref[slot] = x_ref[...]

    def issue(r, c):
        for k in range(TOPK_IN_GROUP):
            _row_copy(stage_ref.at[slot], r, xs_ref, dest_ref[k * t + base + r], sem.at[slot]).start()
        return c

    lax.fori_loop(0, tr, issue, 0, unroll=4)

    @pl.when(i == nsteps - 1)
    def _():
        drain(slot)
        if nsteps > 1:
            drain(1 - slot)


def _scatter(dest_flat, xt, nrows):
    t, d = xt.shape
    tr = 256
    nsteps = t // tr
    kern = functools.partial(_scatter_kernel, tr=tr, t=t, nsteps=nsteps)
    return pl.pallas_call(
        kern,
        out_shape=jax.ShapeDtypeStruct((nrows, d), F32),
        grid_spec=pltpu.PrefetchScalarGridSpec(
            num_scalar_prefetch=1,
            grid=(t // tr,),
            in_specs=[
                pl.BlockSpec((tr, d), lambda i, dest: (i, 0)),
                pl.BlockSpec(memory_space=pl.ANY),
            ],
            out_specs=pl.BlockSpec(memory_space=pl.ANY),
            scratch_shapes=[pltpu.VMEM((2, tr, d), F32), pltpu.SemaphoreType.DMA((2,))],
        ),
        input_output_aliases={2: 0},
        compiler_params=_cparams(("arbitrary",), 32),
        name="moe_scatter_rows",
    )(dest_flat, xt, jnp.zeros((nrows, d), F32))


def _gmm_kernel(blke_ref, meta_ref, xs_ref, w1_ref, w3_ref, w2_ref, ys_ref, w1b_ref, w3b_ref, w2b_ref):
    nb = pl.program_id(0)

    @pl.when(nb < meta_ref[0])
    def _():
        @pl.when((nb == 0) | (blke_ref[nb] != blke_ref[jnp.maximum(nb - 1, 0)]))
        def _():
            w1b_ref[...] = w1_ref[...].astype(BF16)
            w3b_ref[...] = w3_ref[...].astype(BF16)
            w2b_ref[...] = w2_ref[...].astype(BF16)

        xb = xs_ref[...].astype(BF16)
        h1 = _dot(xb, w1b_ref[...])
        h3 = _dot(xb, w3b_ref[...])
        hh = (h1 * jax.nn.sigmoid(h1) * h3).astype(BF16)
        ys_ref[...] = _dot(hh, w2b_ref[...])

    @pl.when(nb >= meta_ref[0])
    def _():
        ys_ref[...] = jnp.zeros_like(ys_ref)


def _gmm(blke, meta, xs, w1, w3, w2, layer):
    nrows, d = xs.shape
    de = w1.shape[-1]
    nblk = nrows // MOE_BLOCK

    def row_map(i, blke, meta):
        return (jnp.minimum(i, meta[0] - 1), 0)

    def w_map(i, blke, meta):
        return (layer, blke[i], 0, 0)

    return pl.pallas_call(
        _gmm_kernel,
        out_shape=jax.ShapeDtypeStruct((nrows, d), F32),
        grid_spec=pltpu.PrefetchScalarGridSpec(
            num_scalar_prefetch=2,
            grid=(nblk,),
            in_specs=[
                pl.BlockSpec((MOE_BLOCK, d), row_map),
                pl.BlockSpec((None, None, d, de), w_map),
                pl.BlockSpec((None, None, d, de), w_map),
                pl.BlockSpec((None, None, de, d), w_map),
            ],
            out_specs=pl.BlockSpec((MOE_BLOCK, d), lambda i, blke, meta: (i, 0)),
            scratch_shapes=[pltpu.VMEM((d, de), BF16), pltpu.VMEM((d, de), BF16),
                            pltpu.VMEM((de, d), BF16)],
        ),
        compiler_params=_cparams(("arbitrary",), 32),
        name="moe_experts",
    )(blke, meta, xs, w1, w3, w2)


def _combine_kernel(dest_ref, ys_ref, x_ref, gate_ref, g_ref, b_ref, o_ref, buf_ref, sem,
                    *, tr, t, nsteps, alpha):
    i = pl.program_id(0)
    slot = i % 2

    def gather(step, s):
        def issue(r, c):
            for k in range(TOPK_IN_GROUP):
                _row_copy(ys_ref, dest_ref[k * t + step * tr + r], buf_ref.at[s], k * tr + r,
                          sem.at[s]).start()
            return c

        lax.fori_loop(0, tr, issue, 0, unroll=4)

    @pl.when(i == 0)
    def _():
        gather(0, 0)

    @pl.when(i + 1 < nsteps)
    def _():
        gather(i + 1, 1 - slot)

    _wait_rows(ys_ref, buf_ref.at[slot], sem.at[slot])
    gate = gate_ref[...]
    f = buf_ref[slot, 0:tr] * gate[:, 0:1] + buf_ref[slot, tr:2 * tr] * gate[:, 1:2]
    o_ref[...] = _layer_norm(alpha * x_ref[...] + f, g_ref[...], b_ref[...])


def _combine(dest_flat, ys, xt, gate_t, g, b, alpha):
    t, d = xt.shape
    tr = 256
    nsteps = t // tr
    kern = functools.partial(_combine_kernel, tr=tr, t=t, nsteps=nsteps, alpha=alpha)
    return pl.pallas_call(
        kern,
        out_shape=jax.ShapeDtypeStruct((t, d), F32),
        grid_spec=pltpu.PrefetchScalarGridSpec(
            num_scalar_prefetch=1,
            grid=(t // tr,),
            in_specs=[
                pl.BlockSpec(memory_space=pl.ANY),
                pl.BlockSpec((tr, d), lambda i, dest: (i, 0)),
                pl.BlockSpec((tr, TOPK_IN_GROUP), lambda i, dest: (i, 0)),
                pl.BlockSpec((1, d), lambda i, dest: (0, 0)),
                pl.BlockSpec((1, d), lambda i, dest: (0, 0)),
            ],
            out_specs=pl.BlockSpec((tr, d), lambda i, dest: (i, 0)),
            scratch_shapes=[pltpu.VMEM((2, TOPK_IN_GROUP * tr, d), F32),
                            pltpu.SemaphoreType.DMA((2,))],
        ),
        compiler_params=_cparams(("arbitrary",), 32),
        name="moe_combine_ln",
    )(dest_flat, ys, xt, gate_t, g.reshape(1, d), b.reshape(1, d))


def _moe_layer(xt, wg, bg, we, be, w1, w3, w2, layer, g, b, alpha):
    t, d = xt.shape
    nblk = -(-t * TOPK_IN_GROUP // MOE_BLOCK) + N_EXPERTS
    eid, gate, rank, cnt = _router(xt, wg, bg, we, be)
    dest, blke, meta = _dest(cnt[:, 0], eid, rank, nblk)
    dest_flat = dest.reshape(-1)
    xs = _scatter(dest_flat, xt, nblk * MOE_BLOCK)
    ys = _gmm(blke, meta, xs, w1, w3, w2, layer)
    return _combine(dest_flat, ys, xt, gate.T, g, b, alpha)


def kernel(x, conv_w_in, conv_k, conv_w_out, attn_w_in, kv_norm_g, kidx_ln_g, kidx_ln_b, attn_w_out, rel_bias, router_wg, router_bg, router_we, router_be, exp_w1, exp_w3, exp_w2, ln1_g, ln1_b, ln2_g, ln2_b):
    bsz, s, d = x.shape
    depth = ln1_g.shape[0]
    n_mixers = 2
    alpha = (2.0 * depth) ** 0.25
    bias_t = _bias_tiles(rel_bias)
    for i in range(depth):
        j = i // n_mixers
        if i % n_mixers == 0:
            x = _conv_layer(x, conv_w_in[j], conv_k[j], conv_w_out[j], ln1_g[i], ln1_b[i], alpha)
        else:
            x = _attn_layer(x, attn_w_in[j], kv_norm_g[j], kidx_ln_g[j], kidx_ln_b[j],
                            attn_w_out[j], bias_t, ln1_g[i], ln1_b[i], alpha)
        xt = _moe_layer(x.reshape(bsz * s, d), router_wg[i], router_bg[i], router_we[i],
                        router_be[i], exp_w1, exp_w3, exp_w2, i, ln2_g[i], ln2_b[i], alpha)
        x = xt.reshape(bsz, s, d)
    return x
```

```python
import functools
import math

import jax
import jax.numpy as jnp
from jax import lax
from jax.experimental import pallas as pl
from jax.experimental.pallas import tpu as pltpu

CONV_WIDTH = 3
N_HEADS = 16
KV_LATENT = 128
N_IDX_HEADS = 8
IDX_DIM = 64
TOPK_MAX = 256
Q_BLOCK = 128
N_BUCKETS = 32
MAX_DISTANCE = 128
N_GROUPS = 8
EXPERTS_PER_GROUP = 8
N_EXPERTS = N_GROUPS * EXPERTS_PER_GROUP
TOPK_IN_GROUP = 2
MOE_BLOCK = 256
LN_EPS = 1e-5
RMS_EPS = 1e-6

V7X_LANES = 128
V7X_SUBLANES = 8
V7X_VMEM_BYTES = 64 * 1024 * 1024

F32 = jnp.float32
BF16 = jnp.bfloat16
I32 = jnp.int32
U32 = jnp.uint32
NEG_INF = float("-inf")
INT16_MIN = -(2 ** 15)
LOG2E = math.log2(math.e)
LOGIT_SCALE2 = (KV_LATENT ** -0.5) * LOG2E
BIAS_ROWS = 4 * Q_BLOCK

_NT = (((1,), (1,)), ((), ()))


def _dot(a, b):
    return jnp.dot(a, b, preferred_element_type=F32)


def _dot_nt(a, b):
    return lax.dot_general(a, b, _NT, preferred_element_type=F32)


def _layer_norm(z, g, b):
    mu = jnp.mean(z, axis=-1, keepdims=True)
    zc = z - mu
    var = jnp.mean(zc * zc, axis=-1, keepdims=True)
    return zc * lax.rsqrt(var + LN_EPS) * g + b


def _cparams(semantics, vmem_mib):
    assert vmem_mib * 1024 * 1024 < V7X_VMEM_BYTES
    return pltpu.CompilerParams(dimension_semantics=semantics,
                                vmem_limit_bytes=vmem_mib * 1024 * 1024)


def _conv_kernel(x_ref, win_ref, ck_ref, wout_ref, g_ref, b_ref, o_ref, carry_ref, gbuf_ref,
                 *, ts, d, cw, alpha):
    @pl.when(pl.program_id(1) == 0)
    def _():
        carry_ref[...] = jnp.zeros_like(carry_ref)

    x = x_ref[...]
    xb = x.astype(BF16)
    row = lax.broadcasted_iota(I32, (ts, cw), 0)
    for c in range(d // cw):
        lo, hi = c * cw, (c + 1) * cw
        bg = _dot(xb, win_ref[:, lo:hi])
        cg = _dot(xb, win_ref[:, d + lo:d + hi])
        hh = _dot(xb, win_ref[:, 2 * d + lo:2 * d + hi])
        u = cg * hh
        prev = carry_ref[:, lo:hi]
        u1 = jnp.where(row == 0, prev[7:8], pltpu.roll(u, 1, 0))
        u2 = jnp.where(row == 0, prev[6:7], jnp.where(row == 1, prev[7:8], pltpu.roll(u, 2, 0)))
        k = ck_ref[:, lo:hi]
        conv = u2 * k[0:1] + u1 * k[1:2] + u * k[2:3]
        gbuf_ref[:, lo:hi] = (bg * conv).astype(BF16)
        carry_ref[:, lo:hi] = u[ts - V7X_SUBLANES:ts]
    y = _dot(gbuf_ref[...], wout_ref[...])
    o_ref[...] = _layer_norm(alpha * x + y, g_ref[...], b_ref[...])


def _conv_layer(x, w_in, conv_k, w_out, g, b, alpha):
    bsz, s, d = x.shape
    ts, cw = 512, 512
    kern = functools.partial(_conv_kernel, ts=ts, d=d, cw=cw, alpha=alpha)
    return pl.pallas_call(
        kern,
        out_shape=jax.ShapeDtypeStruct((bsz, s, d), F32),
        grid=(bsz, s // ts),
        in_specs=[
            pl.BlockSpec((None, ts, d), lambda i, j: (i, j, 0)),
            pl.BlockSpec((d, 3 * d), lambda i, j: (0, 0)),
            pl.BlockSpec((CONV_WIDTH, d), lambda i, j: (0, 0)),
            pl.BlockSpec((d, d), lambda i, j: (0, 0)),
            pl.BlockSpec((1, d), lambda i, j: (0, 0)),
            pl.BlockSpec((1, d), lambda i, j: (0, 0)),
        ],
        out_specs=pl.BlockSpec((None, ts, d), lambda i, j: (i, j, 0)),
        scratch_shapes=[pltpu.VMEM((V7X_SUBLANES, d), F32), pltpu.VMEM((ts, d), BF16)],
        compiler_params=_cparams(("arbitrary", "arbitrary"), 48),
        name="conv_mixer_ln",
    )(x, w_in.astype(BF16), conv_k, w_out.astype(BF16), g.reshape(1, d), b.reshape(1, d))


def _proj_kernel(x_ref, wq_ref, ws_ref, ww_ref, kvg_ref, lng_ref, lnb_ref,
                 q_ref, ckv_ref, ckvt_ref, qidx_ref, kidx_ref, widxt_ref, *, idx_scale):
    xb = x_ref[...].astype(BF16)
    q = _dot(xb, wq_ref[...]) * LOGIT_SCALE2
    for h in range(N_HEADS):
        q_ref[h] = q[:, h * KV_LATENT:(h + 1) * KV_LATENT].astype(BF16)
    sm = _dot(xb, ws_ref[...])
    ckv = sm[:, :KV_LATENT]
    ckv = ckv * lax.rsqrt(jnp.mean(ckv * ckv, axis=-1, keepdims=True) + RMS_EPS) * kvg_ref[...]
    ckv_ref[...] = ckv.astype(BF16)
    ckvt_ref[...] = ckv.T.astype(BF16)
    nq = N_IDX_HEADS * IDX_DIM
    qidx_ref[...] = sm[:, KV_LATENT:KV_LATENT + nq].astype(BF16)
    kidx = sm[:, KV_LATENT + nq:KV_LATENT + nq + IDX_DIM]
    kidx_ref[...] = _layer_norm(kidx, lng_ref[...], lnb_ref[...]).astype(BF16)
    widxt_ref[...] = _dot_nt(ww_ref[...], xb) * idx_scale


def _attn_proj(x, w_in, kv_g, ln_g, ln_b):
    bsz, s, d = x.shape
    ts = 512
    hq = N_HEADS * KV_LATENT
    nq = N_IDX_HEADS * IDX_DIM
    small = KV_LATENT + nq + IDX_DIM
    small_pad = -(-small // V7X_LANES) * V7X_LANES
    wq = w_in[:, :hq].astype(BF16)
    ws = jnp.pad(w_in[:, hq:hq + small], ((0, 0), (0, small_pad - small))).astype(BF16)
    ww = w_in[:, hq + small:].T.astype(BF16)
    idx_scale = (N_IDX_HEADS ** -0.5) * (IDX_DIM ** -0.5)
    kern = functools.partial(_proj_kernel, idx_scale=idx_scale)
    return pl.pallas_call(
        kern,
        out_shape=(
            jax.ShapeDtypeStruct((bsz, N_HEADS, s, KV_LATENT), BF16),
            jax.ShapeDtypeStruct((bsz, s, KV_LATENT), BF16),
            jax.ShapeDtypeStruct((bsz, KV_LATENT, s), BF16),
            jax.ShapeDtypeStruct((bsz, s, nq), BF16),
            jax.ShapeDtypeStruct((bsz, s, IDX_DIM), BF16),
            jax.ShapeDtypeStruct((bsz, N_IDX_HEADS, s), F32),
        ),
        grid=(bsz, s // ts),
        in_specs=[
            pl.BlockSpec((None, ts, d), lambda i, j: (i, j, 0)),
            pl.BlockSpec((d, hq), lambda i, j: (0, 0)),
            pl.BlockSpec((d, small_pad), lambda i, j: (0, 0)),
            pl.BlockSpec((N_IDX_HEADS, d), lambda i, j: (0, 0)),
            pl.BlockSpec((1, KV_LATENT), lambda i, j: (0, 0)),
            pl.BlockSpec((1, IDX_DIM), lambda i, j: (0, 0)),
            pl.BlockSpec((1, IDX_DIM), lambda i, j: (0, 0)),
        ],
        out_specs=(
            pl.BlockSpec((None, N_HEADS, ts, KV_LATENT), lambda i, j: (i, 0, j, 0)),
            pl.BlockSpec((None, ts, KV_LATENT), lambda i, j: (i, j, 0)),
            pl.BlockSpec((None, KV_LATENT, ts), lambda i, j: (i, 0, j)),
            pl.BlockSpec((None, ts, nq), lambda i, j: (i, j, 0)),
            pl.BlockSpec((None, ts, IDX_DIM), lambda i, j: (i, j, 0)),
            pl.BlockSpec((None, N_IDX_HEADS, ts), lambda i, j: (i, 0, j)),
        ),
        compiler_params=_cparams(("arbitrary", "arbitrary"), 48),
        name="attn_proj",
    )(x, wq, ws, ww, kv_g.reshape(1, -1), ln_g.reshape(1, -1), ln_b.reshape(1, -1))


def _bias_kernel(rb_ref, o_ref):
    rows = o_ref.shape[1]
    j = lax.broadcasted_iota(I32, (rows, Q_BLOCK), 0)
    r = lax.broadcasted_iota(I32, (rows, Q_BLOCK), 1)
    dist = 2 * Q_BLOCK + r - j
    dpos = jnp.maximum(dist, 0)
    max_exact = N_BUCKETS // 2
    d_f = jnp.maximum(dpos, 1).astype(F32)
    large = max_exact + (jnp.log(d_f / max_exact) / math.log(MAX_DISTANCE / max_exact)
                         * (N_BUCKETS - max_exact)).astype(I32)
    large = jnp.minimum(large, N_BUCKETS - 1)
    bucket = jnp.where(dpos < max_exact, dpos, large)
    for h in range(N_HEADS):
        acc = jnp.zeros((rows, Q_BLOCK), F32)
        for bk in range(N_BUCKETS):
            acc = jnp.where(bucket == bk, rb_ref[bk, h], acc)
        o_ref[h] = jnp.where(dist >= 0, (acc - rb_ref[N_BUCKETS - 1, h]) * LOG2E, 0.0)


def _bias_tiles(rel_bias):
    return pl.pallas_call(
        _bias_kernel,
        out_shape=jax.ShapeDtypeStruct((N_HEADS, BIAS_ROWS, Q_BLOCK), F32),
        in_specs=[pl.BlockSpec(memory_space=pltpu.SMEM)],
        out_specs=pl.BlockSpec(memory_space=pltpu.VMEM),
        name="rel_bias_tiles",
    )(rel_bias)


def _rows_reduce(parts, op):
    accs = [None, None]
    for c, p in enumerate(parts):
        accs[c % 2] = p if accs[c % 2] is None else op(accs[c % 2], p)
    return accs[0] if accs[1] is None else op(accs[0], accs[1])


def _attn_body(qi, q_ref, qidx_ref, widxt_ref, kidx_ref, ckv_ref, ckvt_ref, bias_ref, x_ref,
               wout_ref, g_ref, b_ref, o_ref, key_ref, k16_ref, madd_ref, lg_ref, eb_ref, obuf_ref,
               *, sk, first_bias_chunk, topk, alpha):
    ch = Q_BLOCK
    n = sk // ch
    t_abs = qi * Q_BLOCK + lax.broadcasted_iota(I32, (1, Q_BLOCK), 1)
    s_abs = lax.broadcasted_iota(I32, (sk, Q_BLOCK), 0)
    valid = s_abs <= t_abs

    kidx = kidx_ref[0:sk, :]
    score = jnp.zeros((sk, Q_BLOCK), F32)
    for h in range(N_IDX_HEADS):
        sh = _dot_nt(kidx, qidx_ref[:, h * IDX_DIM:(h + 1) * IDX_DIM])
        score = score + jnp.maximum(sh, 0.0) * widxt_ref[h:h + 1, :]
    score = jnp.where(score == 0.0, 0.0, score)
    score = jnp.where(valid, score, NEG_INF)
    bits = pltpu.bitcast(score, I32)
    key_ref[0:sk] = bits ^ ((bits >> 31) & 0x7FFFFFFF)

    def count(pred):
        parts = [jnp.where(pred(key_ref[c * ch:(c + 1) * ch]), 1.0, 0.0) for c in range(n)]
        return jnp.sum(_rows_reduce(parts, jnp.add), axis=0, keepdims=True)

    def count16(pred):
        parts = [jnp.where(pred(k16_ref[c * ch:(c + 1) * ch]), jnp.int16(1), jnp.int16(0))
                 for c in range(n)]
        return jnp.sum(_rows_reduce(parts, jnp.add).astype(I32), axis=0, keepdims=True)

    def search16(need):
        def bit_body(it, thr16):
            cand = thr16 + lax.shift_left(jnp.int32(1), 15 - it)
            cand16 = cand.astype(jnp.int16)
            return jnp.where(count16(lambda k: k >= cand16) >= need, cand, thr16)

        return lax.fori_loop(0, 16, bit_body, jnp.full((1, Q_BLOCK), INT16_MIN, I32))

    k16_ref[0:sk] = (key_ref[0:sk] >> 16).astype(jnp.int16)
    thr_hi = search16(topk)
    thr_hi16 = thr_hi.astype(jnp.int16)
    need_lo = topk - count16(lambda k: k > thr_hi16)
    key = key_ref[0:sk]
    k16_ref[0:sk] = jnp.where((key >> 16) == thr_hi, (key & 0xFFFF) + INT16_MIN,
                              INT16_MIN).astype(jnp.int16)
    thr_lo = search16(need_lo)
    thr = (thr_hi << 16) | (thr_lo - INT16_MIN)
    cnt_ge = count(lambda k: k >= thr)
    madd_ref[0:sk] = jnp.where((key_ref[0:sk] >= thr) & valid, 0.0, NEG_INF)

    tied = jnp.where((cnt_ge > topk) & (t_abs >= topk - 1), 1.0, 0.0)

    @pl.when(jnp.max(tied) > 0.0)
    def _():
        chunk = 2 * ch
        need = topk - count(lambda k: k > thr)
        tri = jnp.where(lax.broadcasted_iota(I32, (chunk, chunk), 0)
                        >= lax.broadcasted_iota(I32, (chunk, chunk), 1), 1.0, 0.0).astype(BF16)
        run = jnp.zeros((1, Q_BLOCK), F32)
        for c in range(sk // chunk):
            kc = key_ref[c * chunk:(c + 1) * chunk]
            tie = kc == thr
            pre = _dot(tri, jnp.where(tie, 1.0, 0.0).astype(BF16)) + run
            run = pre[chunk - 1:chunk]
            vc = (c * chunk + lax.broadcasted_iota(I32, (chunk, Q_BLOCK), 0)) <= t_abs
            sel = ((kc > thr) | (tie & (pre <= need))) & vc
            madd_ref[c * chunk:(c + 1) * chunk] = jnp.where(sel, 0.0, NEG_INF)

    last_tile = BIAS_ROWS // ch - 1
    cr = 64
    nsteps = sk // cr
    npairs = N_HEADS // 2

    def stage_steps(p_logits, slot_logits, p_soft, slot_soft, m_soft):
        lg_w = eb = macc = sacc = logit = None
        if p_logits is not None:
            q2 = q_ref[pl.ds(2 * p_logits, 2)].reshape(2 * Q_BLOCK, KV_LATENT)
            logit = _dot_nt(ckv_ref[0:sk, :], q2)
            lg_w = lg_ref.at[slot_logits]
        if p_soft is not None:
            lg_r = lg_ref.at[slot_soft]
            eb = eb_ref.at[slot_soft]
        for c in range(nsteps):
            rows = slice(c * cr, (c + 1) * cr)
            if p_logits is not None:
                madd = madd_ref[rows]
                v = logit[rows] + jnp.concatenate([madd, madd], axis=1)
                blk = (c * cr) // ch
                if blk >= first_bias_chunk:
                    off = jnp.clip(2 - qi + blk, 0, last_tile) * ch + (c * cr) % ch
                    off = pl.multiple_of(off, cr)
                    v = v + jnp.concatenate([bias_ref[2 * p_logits, pl.ds(off, cr), :],
                                             bias_ref[2 * p_logits + 1, pl.ds(off, cr), :]], axis=1)
                lg_w[rows] = v
                macc = v if macc is None else jnp.maximum(macc, v)
            if p_soft is not None:
                e = jnp.exp2(lg_r[rows] - m_soft)
                eb[rows] = e.astype(BF16)
                sacc = e if sacc is None else sacc + e
        m_new = None
        if p_logits is not None:
            m_new = jnp.max(macc, axis=0, keepdims=True)
        if p_soft is not None:
            ssum = jnp.sum(sacc, axis=0, keepdims=True)
            ot = _dot(ckvt_ref[:, 0:sk], eb[0:sk]) / ssum
            obuf_ref[2 * p_soft] = ot[:, :Q_BLOCK].T.astype(BF16)
            obuf_ref[2 * p_soft + 1] = ot[:, Q_BLOCK:].T.astype(BF16)
        return m_new

    m0 = stage_steps(0, 0, None, None, None)

    def pipe_body(j, m_even):
        m_odd = stage_steps(2 * j + 1, 1, 2 * j, 0, m_even)
        return stage_steps(2 * j + 2, 0, 2 * j + 1, 1, m_odd)

    m_even = lax.fori_loop(0, npairs // 2 - 1, pipe_body, m0)
    m_odd = stage_steps(npairs - 1, 1, npairs - 2, 0, m_even)
    stage_steps(None, None, npairs - 1, 1, m_odd)
    o_all = jnp.concatenate([obuf_ref[h] for h in range(N_HEADS)], axis=1)
    acc = _dot(o_all, wout_ref[...])
    o_ref[...] = _layer_norm(alpha * x_ref[...] + acc, g_ref[...], b_ref[...])


def _attn_kernel(*refs, nv, per, topk, alpha):
    qi = pl.program_id(1)
    for v in range(nv):
        body = functools.partial(_attn_body, qi, *refs, sk=(v + 1) * per * Q_BLOCK,
                                 first_bias_chunk=v * per - 1, topk=topk, alpha=alpha)
        pl.when(qi // per == v)(body)


def _attn_layer(x, w_in, kv_g, ln_g, ln_b, w_out, bias_t, g, b, alpha):
    bsz, s, d = x.shape
    nq = s // Q_BLOCK
    topk = min(TOPK_MAX, s // 4)
    per = -(-topk // Q_BLOCK)
    assert nq % per == 0
    nv = nq // per
    q, ckv, ckvt, qidx, kidx, widxt = _attn_proj(x, w_in, kv_g, ln_g, ln_b)
    nqi = N_IDX_HEADS * IDX_DIM
    kern = functools.partial(_attn_kernel, nv=nv, per=per, topk=topk, alpha=alpha)
    return pl.pallas_call(
        kern,
        out_shape=jax.ShapeDtypeStruct((bsz, s, d), F32),
        grid=(bsz, nq),
        in_specs=[
            pl.BlockSpec((None, N_HEADS, Q_BLOCK, KV_LATENT), lambda i, j: (i, 0, j, 0)),
            pl.BlockSpec((None, Q_BLOCK, nqi), lambda i, j: (i, j, 0)),
            pl.BlockSpec((None, N_IDX_HEADS, Q_BLOCK), lambda i, j: (i, 0, j)),
            pl.BlockSpec((None, s, IDX_DIM), lambda i, j: (i, 0, 0)),
            pl.BlockSpec((None, s, KV_LATENT), lambda i, j: (i, 0, 0)),
            pl.BlockSpec((None, KV_LATENT, s), lambda i, j: (i, 0, 0)),
            pl.BlockSpec((N_HEADS, BIAS_ROWS, Q_BLOCK), lambda i, j: (0, 0, 0)),
            pl.BlockSpec((None, Q_BLOCK, d), lambda i, j: (i, j, 0)),
            pl.BlockSpec((N_HEADS * KV_LATENT, d), lambda i, j: (0, 0)),
            pl.BlockSpec((1, d), lambda i, j: (0, 0)),
            pl.BlockSpec((1, d), lambda i, j: (0, 0)),
        ],
        out_specs=pl.BlockSpec((None, Q_BLOCK, d), lambda i, j: (i, j, 0)),
        scratch_shapes=[
            pltpu.VMEM((s, Q_BLOCK), I32),
            pltpu.VMEM((s, Q_BLOCK), jnp.int16),
            pltpu.VMEM((s, Q_BLOCK), F32),
            pltpu.VMEM((2, s, 2 * Q_BLOCK), F32),
            pltpu.VMEM((2, s, 2 * Q_BLOCK), BF16),
            pltpu.VMEM((N_HEADS, Q_BLOCK, KV_LATENT), BF16),
        ],
        compiler_params=_cparams(("arbitrary", "arbitrary"), 48),
        name="dsa_attention_ln",
    )(q, qidx, widxt, kidx, ckv, ckvt, bias_t, x,
      w_out.astype(BF16), g.reshape(1, d), b.reshape(1, d))


def _split_bf16(a):
    hi = a.astype(BF16)
    lo = (a - hi.astype(F32)).astype(BF16)
    return hi, lo


def _router_kernel(x_ref, w_ref, b_ref, eid_ref, gate_ref, rank_ref, cnt_ref, base_ref, u_ref, *, tt):
    @pl.when(pl.program_id(0) == 0)
    def _():
        base_ref[...] = jnp.zeros_like(base_ref)
        u_ref[...] = jnp.where(lax.broadcasted_iota(I32, (tt, tt), 0)
                               < lax.broadcasted_iota(I32, (tt, tt), 1), 1.0, 0.0).astype(BF16)

    xh, xl = _split_bf16(x_ref[...])
    wh, wl = _split_bf16(w_ref[...])
    lt = _dot_nt(wh, xh) + (_dot_nt(wh, xl) + _dot_nt(wl, xh)) + b_ref[...]

    ng, ne = N_GROUPS, EXPERTS_PER_GROUP
    gl = lt[0:ng]
    iog = lax.broadcasted_iota(I32, (ng, tt), 0).astype(F32)
    gmax = jnp.max(gl, axis=0, keepdims=True)
    gidx = jnp.min(jnp.where(gl == gmax, iog, float(ng)), axis=0, keepdims=True)
    g_gate = 1.0 / jnp.sum(jnp.exp(gl - gmax), axis=0, keepdims=True)

    el = jnp.zeros((ne, tt), F32)
    for gi in range(ng):
        el = jnp.where(gidx == float(gi), lt[ng + gi * ne:ng + (gi + 1) * ne], el)
    ioe = lax.broadcasted_iota(I32, (ne, tt), 0).astype(F32)
    m1 = jnp.max(el, axis=0, keepdims=True)
    i1 = jnp.min(jnp.where(el == m1, ioe, float(ne)), axis=0, keepdims=True)
    el2 = jnp.where(ioe == i1, NEG_INF, el)
    m2 = jnp.max(el2, axis=0, keepdims=True)
    i2 = jnp.min(jnp.where(el2 == m2, ioe, float(ne)), axis=0, keepdims=True)
    ex = jnp.exp(m2 - m1)
    p1 = 1.0 / (1.0 + ex)
    gate_ref[0:1, :] = p1 * g_gate
    gate_ref[1:2, :] = ex * p1 * g_gate
    e1 = gidx * float(ne) + i1
    e2 = gidx * float(ne) + i2
    eid_ref[0:1, :] = e1.astype(I32)
    eid_ref[1:2, :] = e2.astype(I32)

    iox = lax.broadcasted_iota(I32, (N_EXPERTS, tt), 0).astype(F32)
    oh1 = jnp.where(iox == e1, 1.0, 0.0)
    oh2 = jnp.where(iox == e2, 1.0, 0.0)
    pre1 = _dot(oh1.astype(BF16), u_ref[...])
    pre2 = _dot(oh2.astype(BF16), u_ref[...])
    tot1 = jnp.sum(oh1, axis=1, keepdims=True)
    tot2 = jnp.sum(oh2, axis=1, keepdims=True)
    base = base_ref[...]
    rank_ref[0:1, :] = jnp.sum(oh1 * (base + pre1), axis=0, keepdims=True).astype(I32)
    rank_ref[1:2, :] = jnp.sum(oh2 * (base + tot1 + pre2), axis=0, keepdims=True).astype(I32)
    base = base + tot1 + tot2
    base_ref[...] = base
    cnt_ref[...] = jnp.broadcast_to(base, cnt_ref.shape).astype(I32)


def _router(xt, wg, bg, we, be):
    t, d = xt.shape
    tt = 512
    rows = V7X_LANES
    wcat = jnp.pad(jnp.concatenate([wg, we], axis=1).T, ((0, rows - N_GROUPS - N_EXPERTS), (0, 0)))
    bcat = jnp.pad(jnp.concatenate([bg, be]), (0, rows - N_GROUPS - N_EXPERTS)).reshape(rows, 1)
    kern = functools.partial(_router_kernel, tt=tt)
    return pl.pallas_call(
        kern,
        out_shape=(
            jax.ShapeDtypeStruct((TOPK_IN_GROUP, t), I32),
            jax.ShapeDtypeStruct((TOPK_IN_GROUP, t), F32),
            jax.ShapeDtypeStruct((TOPK_IN_GROUP, t), I32),
            jax.ShapeDtypeStruct((N_EXPERTS, V7X_LANES), I32),
        ),
        grid=(t // tt,),
        in_specs=[
            pl.BlockSpec((tt, d), lambda i: (i, 0)),
            pl.BlockSpec((rows, d), lambda i: (0, 0)),
            pl.BlockSpec((rows, 1), lambda i: (0, 0)),
        ],
        out_specs=(
            pl.BlockSpec((TOPK_IN_GROUP, tt), lambda i: (0, i)),
            pl.BlockSpec((TOPK_IN_GROUP, tt), lambda i: (0, i)),
            pl.BlockSpec((TOPK_IN_GROUP, tt), lambda i: (0, i)),
            pl.BlockSpec((N_EXPERTS, V7X_LANES), lambda i: (0, 0)),
        ),
        scratch_shapes=[pltpu.VMEM((N_EXPERTS, 1), F32), pltpu.VMEM((tt, tt), BF16)],
        compiler_params=_cparams(("arbitrary",), 32),
        name="moe_router",
    )(xt, wcat, bcat)


def _dest_kernel(cnt_ref, eid_ref, rank_ref, dest_ref, blke_ref, meta_ref, pstart_ref, *, nblk):
    shift = MOE_BLOCK.bit_length() - 1

    def expert_body(e, acc):
        nb = (cnt_ref[e] + (MOE_BLOCK - 1)) >> shift
        pstart_ref[e] = acc
        b0 = acc >> shift

        def blk_body(j, c):
            blke_ref[b0 + j] = e
            return c

        lax.fori_loop(0, nb, blk_body, 0)
        return acc + (nb << shift)

    total = lax.fori_loop(0, N_EXPERTS, expert_body, jnp.int32(0))
    nused = total >> shift
    meta_ref[0] = nused
    last_e = blke_ref[nused - 1]

    def tail_body(j, c):
        blke_ref[j] = last_e
        return c

    lax.fori_loop(nused, nblk, tail_body, 0)

    def dest_body(e, dest):
        return dest + jnp.where(eid_ref[...] == e, pstart_ref[e], 0)

    dest_ref[...] = lax.fori_loop(0, N_EXPERTS, dest_body, rank_ref[...])


def _dest(cnt, eid, rank, nblk):
    t = eid.shape[1]
    kern = functools.partial(_dest_kernel, nblk=nblk)
    return pl.pallas_call(
        kern,
        out_shape=(
            jax.ShapeDtypeStruct((TOPK_IN_GROUP, t), I32),
            jax.ShapeDtypeStruct((nblk,), I32),
            jax.ShapeDtypeStruct((1,), I32),
        ),
        in_specs=[
            pl.BlockSpec(memory_space=pltpu.SMEM),
            pl.BlockSpec(memory_space=pltpu.VMEM),
            pl.BlockSpec(memory_space=pltpu.VMEM),
        ],
        out_specs=(
            pl.BlockSpec(memory_space=pltpu.VMEM),
            pl.BlockSpec(memory_space=pltpu.SMEM),
            pl.BlockSpec(memory_space=pltpu.SMEM),
        ),
        scratch_shapes=[pltpu.SMEM((N_EXPERTS,), I32)],
        name="moe_dest",
    )(cnt, eid, rank)


def _row_copy(src_ref, src_row, dst_ref, dst_row, sem):
    return pltpu.make_async_copy(src_ref.at[pl.ds(src_row, 1)], dst_ref.at[pl.ds(dst_row, 1)], sem)


def _wait_rows(hbm_ref, vmem_rows_ref, sem):
    n = vmem_rows_ref.shape[0]
    pltpu.make_async_copy(hbm_ref.at[pl.ds(0, n)], vmem_rows_ref, sem).wait()


def _pack_bf16_pairs(x):
    half = x.shape[1] // 2
    hi = pltpu.bitcast(x[:, :half].astype(BF16).astype(F32), U32)
    lo = pltpu.bitcast(x[:, half:].astype(BF16).astype(F32), U32)
    return hi | (lo >> 16)


def _unpack_bf16_pairs(u):
    hi = pltpu.bitcast(u & jnp.uint32(0xFFFF0000), F32).astype(BF16)
    lo = pltpu.bitcast(u << 16, F32).astype(BF16)
    return jnp.concatenate([hi, lo], axis=1)


def _scatter_kernel(dest_ref, x_ref, xs_in_ref, xs_ref, stage_ref, sem, *, tr, t, nsteps):
    del xs_in_ref
    i = pl.program_id(0)
    slot = i % 2
    base = i * tr

    def drain(s):
        for _ in range(TOPK_IN_GROUP):
            _wait_rows(xs_ref, stage_ref.at[s], sem.at[s])

    @pl.when(i >= 2)
    def _():
        drain(slot)

    stage_ref[slot] = _pack_bf16_pairs(x_ref[...])

    def issue(r, c):
        for k in range(TOPK_IN_GROUP):
            _row_copy(stage_ref.at[slot], r, xs_ref, dest_ref[k * t + base + r], sem.at[slot]).start()
        return c

    lax.fori_loop(0, tr, issue, 0, unroll=4)

    @pl.when(i == nsteps - 1)
    def _():
        drain(slot)
        if nsteps > 1:
            drain(1 - slot)


def _scatter(dest_flat, xt, nrows):
    t, d = xt.shape
    tr = 256
    nsteps = t // tr
    kern = functools.partial(_scatter_kernel, tr=tr, t=t, nsteps=nsteps)
    return pl.pallas_call(
        kern,
        out_shape=jax.ShapeDtypeStruct((nrows, d // 2), U32),
        grid_spec=pltpu.PrefetchScalarGridSpec(
            num_scalar_prefetch=1,
            grid=(t // tr,),
            in_specs=[
                pl.BlockSpec((tr, d), lambda i, dest: (i, 0)),
                pl.BlockSpec(memory_space=pl.ANY),
            ],
            out_specs=pl.BlockSpec(memory_space=pl.ANY),
            scratch_shapes=[pltpu.VMEM((2, tr, d // 2), U32), pltpu.SemaphoreType.DMA((2,))],
        ),
        input_output_aliases={2: 0},
        compiler_params=_cparams(("arbitrary",), 32),
        name="moe_scatter_rows",
    )(dest_flat, xt, jnp.zeros((nrows, d // 2), U32))


def _gmm_kernel(blke_ref, meta_ref, xs_ref, w1_ref, w3_ref, w2_ref, ys_ref, w1b_ref, w3b_ref, w2b_ref):
    nb = pl.program_id(0)

    @pl.when(nb < meta_ref[0])
    def _():
        @pl.when((nb == 0) | (blke_ref[nb] != blke_ref[jnp.maximum(nb - 1, 0)]))
        def _():
            w1b_ref[...] = w1_ref[...].astype(BF16)
            w3b_ref[...] = w3_ref[...].astype(BF16)
            w2b_ref[...] = w2_ref[...].astype(BF16)

        xb = _unpack_bf16_pairs(xs_ref[...])
        h1 = _dot(xb, w1b_ref[...])
        h3 = _dot(xb, w3b_ref[...])
        hh = (h1 * jax.nn.sigmoid(h1) * h3).astype(BF16)
        ys_ref[...] = _dot(hh, w2b_ref[...])

    @pl.when(nb >= meta_ref[0])
    def _():
        ys_ref[...] = jnp.zeros_like(ys_ref)


def _gmm(blke, meta, xs, w1, w3, w2, layer):
    nrows = xs.shape[0]
    d, de = w1.shape[-2:]
    nblk = nrows // MOE_BLOCK

    def row_map(i, blke, meta):
        return (jnp.minimum(i, meta[0] - 1), 0)

    def w_map(i, blke, meta):
        return (layer, blke[i], 0, 0)

    return pl.pallas_call(
        _gmm_kernel,
        out_shape=jax.ShapeDtypeStruct((nrows, d), F32),
        grid_spec=pltpu.PrefetchScalarGridSpec(
            num_scalar_prefetch=2,
            grid=(nblk,),
            in_specs=[
                pl.BlockSpec((MOE_BLOCK, d // 2), row_map),
                pl.BlockSpec((None, None, d, de), w_map),
                pl.BlockSpec((None, None, d, de), w_map),
                pl.BlockSpec((None, None, de, d), w_map),
            ],
            out_specs=pl.BlockSpec((MOE_BLOCK, d), lambda i, blke, meta: (i, 0)),
            scratch_shapes=[pltpu.VMEM((d, de), BF16), pltpu.VMEM((d, de), BF16),
                            pltpu.VMEM((de, d), BF16)],
        ),
        compiler_params=_cparams(("arbitrary",), 32),
        name="moe_experts",
    )(blke, meta, xs, w1, w3, w2)


def _combine_kernel(dest_ref, ys_ref, x_ref, gate_ref, g_ref, b_ref, o_ref, buf_ref, sem,
                    *, tr, t, nsteps, alpha):
    i = pl.program_id(0)
    slot = i % 2

    def gather(step, s):
        def issue(r, c):
            for k in range(TOPK_IN_GROUP):
                _row_copy(ys_ref, dest_ref[k * t + step * tr + r], buf_ref.at[s], k * tr + r,
                          sem.at[s]).start()
            return c

        lax.fori_loop(0, tr, issue, 0, unroll=4)

    @pl.when(i == 0)
    def _():
        gather(0, 0)

    @pl.when(i + 1 < nsteps)
    def _():
        gather(i + 1, 1 - slot)

    _wait_rows(ys_ref, buf_ref.at[slot], sem.at[slot])
    gate = gate_ref[...]
    f = buf_ref[slot, 0:tr] * gate[:, 0:1] + buf_ref[slot, tr:2 * tr] * gate[:, 1:2]
    o_ref[...] = _layer_norm(alpha * x_ref[...] + f, g_ref[...], b_ref[...])


def _combine(dest_flat, ys, xt, gate_t, g, b, alpha):
    t, d = xt.shape
    tr = 256
    nsteps = t // tr
    kern = functools.partial(_combine_kernel, tr=tr, t=t, nsteps=nsteps, alpha=alpha)
    return pl.pallas_call(
        kern,
        out_shape=jax.ShapeDtypeStruct((t, d), F32),
        grid_spec=pltpu.PrefetchScalarGridSpec(
            num_scalar_prefetch=1,
            grid=(t // tr,),
            in_specs=[
                pl.BlockSpec(memory_space=pl.ANY),
                pl.BlockSpec((tr, d), lambda i, dest: (i, 0)),
                pl.BlockSpec((tr, TOPK_IN_GROUP), lambda i, dest: (i, 0)),
                pl.BlockSpec((1, d), lambda i, dest: (0, 0)),
                pl.BlockSpec((1, d), lambda i, dest: (0, 0)),
            ],
            out_specs=pl.BlockSpec((tr, d), lambda i, dest: (i, 0)),
            scratch_shapes=[pltpu.VMEM((2, TOPK_IN_GROUP * tr, d), F32),
                            pltpu.SemaphoreType.DMA((2,))],
        ),
        compiler_params=_cparams(("arbitrary",), 32),
        name="moe_combine_ln",
    )(dest_flat, ys, xt, gate_t, g.reshape(1, d), b.reshape(1, d))


def _moe_layer(xt, wg, bg, we, be, w1, w3, w2, layer, g, b, alpha):
    t, d = xt.shape
    nblk = -(-t * TOPK_IN_GROUP // MOE_BLOCK) + N_EXPERTS
    eid, gate, rank, cnt = _router(xt, wg, bg, we, be)
    dest, blke, meta = _dest(cnt[:, 0], eid, rank, nblk)
    dest_flat = dest.reshape(-1)
    xs = _scatter(dest_flat, xt, nblk * MOE_BLOCK)
    ys = _gmm(blke, meta, xs, w1, w3, w2, layer)
    return _combine(dest_flat, ys, xt, gate.T, g, b, alpha)


def kernel(x, conv_w_in, conv_k, conv_w_out, attn_w_in, kv_norm_g, kidx_ln_g, kidx_ln_b, attn_w_out, rel_bias, router_wg, router_bg, router_we, router_be, exp_w1, exp_w3, exp_w2, ln1_g, ln1_b, ln2_g, ln2_b):
    bsz, s, d = x.shape
    depth = ln1_g.shape[0]
    n_mixers = 2
    alpha = (2.0 * depth) ** 0.25
    bias_t = _bias_tiles(rel_bias)
    for i in range(depth):
        j = i // n_mixers
        if i % n_mixers == 0:
            x = _conv_layer(x, conv_w_in[j], conv_k[j], conv_w_out[j], ln1_g[i], ln1_b[i], alpha)
        else:
            x = _attn_layer(x, attn_w_in[j], kv_norm_g[j], kidx_ln_g[j], kidx_ln_b[j],
                            attn_w_out[j], bias_t, ln1_g[i], ln1_b[i], alpha)
        xt = _moe_layer(x.reshape(bsz * s, d), router_wg[i], router_bg[i], router_we[i],
                        router_be[i], exp_w1, exp_w3, exp_w2, i, ln2_g[i], ln2_b[i], alpha)
        x = xt.reshape(bsz, s, d)
    return x
```

```python
import functools
import math

import jax
import jax.numpy as jnp
from jax import lax
from jax.experimental import pallas as pl
from jax.experimental.pallas import tpu as pltpu

CONV_WIDTH = 3
N_HEADS = 16
KV_LATENT = 128
N_IDX_HEADS = 8
IDX_DIM = 64
TOPK_MAX = 256
Q_BLOCK = 128
N_BUCKETS = 32
MAX_DISTANCE = 128
N_GROUPS = 8
EXPERTS_PER_GROUP = 8
N_EXPERTS = N_GROUPS * EXPERTS_PER_GROUP
TOPK_IN_GROUP = 2
MOE_BLOCK = 256
LN_EPS = 1e-5
RMS_EPS = 1e-6

V7X_LANES = 128
V7X_SUBLANES = 8
V7X_VMEM_BYTES = 64 * 1024 * 1024

F32 = jnp.float32
BF16 = jnp.bfloat16
I32 = jnp.int32
U32 = jnp.uint32
NEG_INF = float("-inf")
INT32_MIN = -(2 ** 31)
LOG2E = math.log2(math.e)
LOGIT_SCALE2 = (KV_LATENT ** -0.5) * LOG2E
BIAS_ROWS = 4 * Q_BLOCK

_NT = (((1,), (1,)), ((), ()))


def _dot(a, b):
    return jnp.dot(a, b, preferred_element_type=F32)


def _dot_nt(a, b):
    return lax.dot_general(a, b, _NT, preferred_element_type=F32)


def _layer_norm(z, g, b):
    mu = jnp.mean(z, axis=-1, keepdims=True)
    zc = z - mu
    var = jnp.mean(zc * zc, axis=-1, keepdims=True)
    return zc * lax.rsqrt(var + LN_EPS) * g + b


def _cparams(semantics, vmem_mib):
    assert vmem_mib * 1024 * 1024 < V7X_VMEM_BYTES
    return pltpu.CompilerParams(dimension_semantics=semantics,
                                vmem_limit_bytes=vmem_mib * 1024 * 1024)


def _conv_kernel(x_ref, win_ref, ck_ref, wout_ref, g_ref, b_ref, rw_ref, rb_ref,
                 o_ref, eid_ref, gate_ref, rank_ref, cnt_ref, carry_ref, gbuf_ref, base_ref, u_ref,
                 *, ts, d, cw, alpha):
    @pl.when(pl.program_id(1) == 0)
    def _():
        carry_ref[...] = jnp.zeros_like(carry_ref)

    x = x_ref[...]
    xb = x.astype(BF16)
    row = lax.broadcasted_iota(I32, (ts, cw), 0)
    for c in range(d // cw):
        lo, hi = c * cw, (c + 1) * cw
        bg = _dot(xb, win_ref[:, lo:hi])
        cg = _dot(xb, win_ref[:, d + lo:d + hi])
        hh = _dot(xb, win_ref[:, 2 * d + lo:2 * d + hi])
        u = cg * hh
        prev = carry_ref[:, lo:hi]
        u1 = jnp.where(row == 0, prev[7:8], pltpu.roll(u, 1, 0))
        u2 = jnp.where(row == 0, prev[6:7], jnp.where(row == 1, prev[7:8], pltpu.roll(u, 2, 0)))
        k = ck_ref[:, lo:hi]
        conv = u2 * k[0:1] + u1 * k[1:2] + u * k[2:3]
        gbuf_ref[:, lo:hi] = (bg * conv).astype(BF16)
        carry_ref[:, lo:hi] = u[ts - V7X_SUBLANES:ts]
    y = _dot(gbuf_ref[...], wout_ref[...])
    out = _layer_norm(alpha * x + y, g_ref[...], b_ref[...])
    o_ref[...] = out
    first = (pl.program_id(0) == 0) & (pl.program_id(1) == 0)
    _route_tile(out, first, rw_ref, rb_ref, eid_ref, gate_ref, rank_ref, cnt_ref, base_ref, u_ref)


def _conv_layer(x, w_in, conv_k, w_out, g, b, router, alpha):
    bsz, s, d = x.shape
    ts, cw = 512, 512
    nst = s // ts
    r_in, r_shape, r_out, r_scratch = _router_specs(bsz * s, ts, d, lambda i, j: i * nst + j)
    kern = functools.partial(_conv_kernel, ts=ts, d=d, cw=cw, alpha=alpha)
    return pl.pallas_call(
        kern,
        out_shape=(jax.ShapeDtypeStruct((bsz, s, d), F32),) + r_shape,
        grid=(bsz, nst),
        in_specs=[
            pl.BlockSpec((None, ts, d), lambda i, j: (i, j, 0)),
            pl.BlockSpec((d, 3 * d), lambda i, j: (0, 0)),
            pl.BlockSpec((CONV_WIDTH, d), lambda i, j: (0, 0)),
            pl.BlockSpec((d, d), lambda i, j: (0, 0)),
            pl.BlockSpec((1, d), lambda i, j: (0, 0)),
            pl.BlockSpec((1, d), lambda i, j: (0, 0)),
        ] + r_in,
        out_specs=(pl.BlockSpec((None, ts, d), lambda i, j: (i, j, 0)),) + r_out,
        scratch_shapes=[pltpu.VMEM((V7X_SUBLANES, d), F32), pltpu.VMEM((ts, d), BF16)] + r_scratch,
        compiler_params=_cparams(("arbitrary", "arbitrary"), 48),
        name="conv_mixer_ln",
    )(x, w_in.astype(BF16), conv_k, w_out.astype(BF16), g.reshape(1, d), b.reshape(1, d), *router)


def _proj_kernel(x_ref, wq_ref, ws_ref, ww_ref, kvg_ref, lng_ref, lnb_ref,
                 q_ref, ckv_ref, ckvt_ref, qidx_ref, kidx_ref, widxt_ref, *, idx_scale):
    xb = x_ref[...].astype(BF16)
    q = _dot(xb, wq_ref[...]) * LOGIT_SCALE2
    for h in range(N_HEADS):
        q_ref[h] = q[:, h * KV_LATENT:(h + 1) * KV_LATENT].astype(BF16)
    sm = _dot(xb, ws_ref[...])
    ckv = sm[:, :KV_LATENT]
    ckv = ckv * lax.rsqrt(jnp.mean(ckv * ckv, axis=-1, keepdims=True) + RMS_EPS) * kvg_ref[...]
    ckv_ref[...] = ckv.astype(BF16)
    ckvt_ref[...] = ckv.T.astype(BF16)
    nq = N_IDX_HEADS * IDX_DIM
    qidx_ref[...] = sm[:, KV_LATENT:KV_LATENT + nq].astype(BF16)
    kidx = sm[:, KV_LATENT + nq:KV_LATENT + nq + IDX_DIM]
    kidx_ref[...] = _layer_norm(kidx, lng_ref[...], lnb_ref[...]).astype(BF16)
    widxt_ref[...] = _dot_nt(ww_ref[...], xb) * idx_scale


def _attn_proj(x, w_in, kv_g, ln_g, ln_b):
    bsz, s, d = x.shape
    ts = 512
    hq = N_HEADS * KV_LATENT
    nq = N_IDX_HEADS * IDX_DIM
    small = KV_LATENT + nq + IDX_DIM
    small_pad = -(-small // V7X_LANES) * V7X_LANES
    wq = w_in[:, :hq].astype(BF16)
    ws = jnp.pad(w_in[:, hq:hq + small], ((0, 0), (0, small_pad - small))).astype(BF16)
    ww = w_in[:, hq + small:].T.astype(BF16)
    idx_scale = (N_IDX_HEADS ** -0.5) * (IDX_DIM ** -0.5)
    kern = functools.partial(_proj_kernel, idx_scale=idx_scale)
    return pl.pallas_call(
        kern,
        out_shape=(
            jax.ShapeDtypeStruct((bsz, N_HEADS, s, KV_LATENT), BF16),
            jax.ShapeDtypeStruct((bsz, s, KV_LATENT), BF16),
            jax.ShapeDtypeStruct((bsz, KV_LATENT, s), BF16),
            jax.ShapeDtypeStruct((bsz, s, nq), BF16),
            jax.ShapeDtypeStruct((bsz, s, IDX_DIM), BF16),
            jax.ShapeDtypeStruct((bsz, N_IDX_HEADS, s), F32),
        ),
        grid=(bsz, s // ts),
        in_specs=[
            pl.BlockSpec((None, ts, d), lambda i, j: (i, j, 0)),
            pl.BlockSpec((d, hq), lambda i, j: (0, 0)),
            pl.BlockSpec((d, small_pad), lambda i, j: (0, 0)),
            pl.BlockSpec((N_IDX_HEADS, d), lambda i, j: (0, 0)),
            pl.BlockSpec((1, KV_LATENT), lambda i, j: (0, 0)),
            pl.BlockSpec((1, IDX_DIM), lambda i, j: (0, 0)),
            pl.BlockSpec((1, IDX_DIM), lambda i, j: (0, 0)),
        ],
        out_specs=(
            pl.BlockSpec((None, N_HEADS, ts, KV_LATENT), lambda i, j: (i, 0, j, 0)),
            pl.BlockSpec((None, ts, KV_LATENT), lambda i, j: (i, j, 0)),
            pl.BlockSpec((None, KV_LATENT, ts), lambda i, j: (i, 0, j)),
            pl.BlockSpec((None, ts, nq), lambda i, j: (i, j, 0)),
            pl.BlockSpec((None, ts, IDX_DIM), lambda i, j: (i, j, 0)),
            pl.BlockSpec((None, N_IDX_HEADS, ts), lambda i, j: (i, 0, j)),
        ),
        compiler_params=_cparams(("arbitrary", "arbitrary"), 48),
        name="attn_proj",
    )(x, wq, ws, ww, kv_g.reshape(1, -1), ln_g.reshape(1, -1), ln_b.reshape(1, -1))


def _bias_kernel(rb_ref, o_ref):
    rows = o_ref.shape[1]
    j = lax.broadcasted_iota(I32, (rows, Q_BLOCK), 0)
    r = lax.broadcasted_iota(I32, (rows, Q_BLOCK), 1)
    dist = 2 * Q_BLOCK + r - j
    dpos = jnp.maximum(dist, 0)
    max_exact = N_BUCKETS // 2
    d_f = jnp.maximum(dpos, 1).astype(F32)
    large = max_exact + (jnp.log(d_f / max_exact) / math.log(MAX_DISTANCE / max_exact)
                         * (N_BUCKETS - max_exact)).astype(I32)
    large = jnp.minimum(large, N_BUCKETS - 1)
    bucket = jnp.where(dpos < max_exact, dpos, large)
    for h in range(N_HEADS):
        acc = jnp.zeros((rows, Q_BLOCK), F32)
        for bk in range(N_BUCKETS):
            acc = jnp.where(bucket == bk, rb_ref[bk, h], acc)
        o_ref[h] = jnp.where(dist >= 0, (acc - rb_ref[N_BUCKETS - 1, h]) * LOG2E, 0.0)


def _bias_tiles(rel_bias):
    return pl.pallas_call(
        _bias_kernel,
        out_shape=jax.ShapeDtypeStruct((N_HEADS, BIAS_ROWS, Q_BLOCK), F32),
        in_specs=[pl.BlockSpec(memory_space=pltpu.SMEM)],
        out_specs=pl.BlockSpec(memory_space=pltpu.VMEM),
        name="rel_bias_tiles",
    )(rel_bias)


def _rows_reduce(parts, op):
    accs = [None, None]
    for c, p in enumerate(parts):
        accs[c % 2] = p if accs[c % 2] is None else op(accs[c % 2], p)
    return accs[0] if accs[1] is None else op(accs[0], accs[1])


def _attn_body(qi, q_ref, qidx_ref, widxt_ref, kidx_ref, ckv_ref, ckvt_ref, bias_ref, x_ref,
               wout_ref, g_ref, b_ref, o_ref, key_ref, madd_ref, lg_ref, eb_ref, obuf_ref,
               *, sk, first_bias_chunk, topk, alpha):
    ch = Q_BLOCK
    n = sk // ch
    t_abs = qi * Q_BLOCK + lax.broadcasted_iota(I32, (1, Q_BLOCK), 1)
    s_abs = lax.broadcasted_iota(I32, (sk, Q_BLOCK), 0)
    valid = s_abs <= t_abs

    kidx = kidx_ref[0:sk, :]
    score = jnp.zeros((sk, Q_BLOCK), F32)
    for h in range(N_IDX_HEADS):
        sh = _dot_nt(kidx, qidx_ref[:, h * IDX_DIM:(h + 1) * IDX_DIM])
        score = score + jnp.maximum(sh, 0.0) * widxt_ref[h:h + 1, :]
    score = jnp.where(score == 0.0, 0.0, score)
    score = jnp.where(valid, score, NEG_INF)
    bits = pltpu.bitcast(score, I32)
    key_ref[0:sk] = bits ^ ((bits >> 31) & 0x7FFFFFFF)

    def count(pred):
        parts = [jnp.where(pred(key_ref[c * ch:(c + 1) * ch]), 1.0, 0.0) for c in range(n)]
        return jnp.sum(_rows_reduce(parts, jnp.add), axis=0, keepdims=True)

    def bit_body(it, thr):
        cand = thr + lax.shift_left(jnp.int32(1), 31 - it)
        return jnp.where(count(lambda k: k >= cand) >= topk, cand, thr)

    thr = lax.fori_loop(0, 32, bit_body, jnp.full((1, Q_BLOCK), INT32_MIN, I32))
    cnt_ge = count(lambda k: k >= thr)
    madd_ref[0:sk] = jnp.where((key_ref[0:sk] >= thr) & valid, 0.0, NEG_INF)

    tied = jnp.where((cnt_ge > topk) & (t_abs >= topk - 1), 1.0, 0.0)

    @pl.when(jnp.max(tied) > 0.0)
    def _():
        chunk = ch
        need = topk - count(lambda k: k > thr)
        tri = jnp.where(lax.broadcasted_iota(I32, (chunk, chunk), 0)
                        >= lax.broadcasted_iota(I32, (chunk, chunk), 1), 1.0, 0.0).astype(BF16)
        run = jnp.zeros((1, Q_BLOCK), F32)
        for c in range(sk // chunk):
            kc = key_ref[c * chunk:(c + 1) * chunk]
            tie = kc == thr
            pre = _dot(tri, jnp.where(tie, 1.0, 0.0).astype(BF16)) + run
            run = pre[chunk - 1:chunk]
            vc = (c * chunk + lax.broadcasted_iota(I32, (chunk, Q_BLOCK), 0)) <= t_abs
            sel = ((kc > thr) | (tie & (pre <= need))) & vc
            madd_ref[c * chunk:(c + 1) * chunk] = jnp.where(sel, 0.0, NEG_INF)

    last_tile = BIAS_ROWS // ch - 1
    cr = 64
    nsteps = sk // cr
    npairs = N_HEADS // 2

    def stage_steps(p_logits, slot_logits, p_soft, slot_soft, m_soft):
        lg_w = eb = macc = sacc = logit = None
        if p_logits is not None:
            q2 = q_ref[pl.ds(2 * p_logits, 2)].reshape(2 * Q_BLOCK, KV_LATENT)
            logit = _dot_nt(ckv_ref[0:sk, :], q2)
            lg_w = lg_ref.at[slot_logits]
        if p_soft is not None:
            lg_r = lg_ref.at[slot_soft]
            eb = eb_ref.at[slot_soft]
        for c in range(nsteps):
            rows = slice(c * cr, (c + 1) * cr)
            if p_logits is not None:
                madd = madd_ref[rows]
                v = logit[rows] + jnp.concatenate([madd, madd], axis=1)
                blk = (c * cr) // ch
                if blk >= first_bias_chunk:
                    off = min(max(2 - qi + blk, 0), last_tile) * ch + (c * cr) % ch
                    v = v + jnp.concatenate([bias_ref[2 * p_logits, pl.ds(off, cr), :],
                                             bias_ref[2 * p_logits + 1, pl.ds(off, cr), :]], axis=1)
                lg_w[rows] = v
                macc = v if macc is None else jnp.maximum(macc, v)
            if p_soft is not None:
                e = jnp.exp2(lg_r[rows] - m_soft)
                eb[rows] = e.astype(BF16)
                sacc = e if sacc is None else sacc + e
        m_new = None
        if p_logits is not None:
            m_new = jnp.max(macc, axis=0, keepdims=True)
        if p_soft is not None:
            ssum = jnp.sum(sacc, axis=0, keepdims=True)
            ot = _dot(ckvt_ref[:, 0:sk], eb[0:sk]) / ssum
            obuf_ref[2 * p_soft] = ot[:, :Q_BLOCK].T.astype(BF16)
            obuf_ref[2 * p_soft + 1] = ot[:, Q_BLOCK:].T.astype(BF16)
        return m_new

    m0 = stage_steps(0, 0, None, None, None)

    def pipe_body(j, m_even):
        m_odd = stage_steps(2 * j + 1, 1, 2 * j, 0, m_even)
        return stage_steps(2 * j + 2, 0, 2 * j + 1, 1, m_odd)

    m_even = lax.fori_loop(0, npairs // 2 - 1, pipe_body, m0)
    m_odd = stage_steps(npairs - 1, 1, npairs - 2, 0, m_even)
    stage_steps(None, None, npairs - 1, 1, m_odd)
    o_all = jnp.concatenate([obuf_ref[h] for h in range(N_HEADS)], axis=1)
    acc = _dot(o_all, wout_ref[...])
    o_ref[...] = _layer_norm(alpha * x_ref[...] + acc, g_ref[...], b_ref[...])


def _attn_kernel(q_ref, qidx_ref, widxt_ref, kidx_ref, ckv_ref, ckvt_ref, bias_ref, x_ref, wout_ref,
                 g_ref, b_ref, rw_ref, rb_ref, o_ref, eid_ref, gate_ref, rank_ref, cnt_ref,
                 key_ref, madd_ref, lg_ref, eb_ref, obuf_ref, base_ref, u_ref,
                 *, nq, min_blocks, topk, alpha):
    qi = pl.program_id(1)
    for v in range(nq):
        body = functools.partial(
            _attn_body, v, q_ref, qidx_ref, widxt_ref, kidx_ref, ckv_ref, ckvt_ref, bias_ref, x_ref,
            wout_ref, g_ref, b_ref, o_ref, key_ref, madd_ref, lg_ref, eb_ref, obuf_ref,
            sk=max(v + 1, min_blocks) * Q_BLOCK, first_bias_chunk=v - 1, topk=topk, alpha=alpha)
        pl.when(qi == v)(body)
    first = (pl.program_id(0) == 0) & (qi == 0)
    _route_tile(o_ref[...], first, rw_ref, rb_ref, eid_ref, gate_ref, rank_ref, cnt_ref,
                base_ref, u_ref)


def _attn_layer(x, w_in, kv_g, ln_g, ln_b, w_out, bias_t, g, b, router, alpha):
    bsz, s, d = x.shape
    nq = s // Q_BLOCK
    topk = min(TOPK_MAX, s // 4)
    min_blocks = -(-topk // Q_BLOCK)
    q, ckv, ckvt, qidx, kidx, widxt = _attn_proj(x, w_in, kv_g, ln_g, ln_b)
    nqi = N_IDX_HEADS * IDX_DIM
    r_in, r_shape, r_out, r_scratch = _router_specs(bsz * s, Q_BLOCK, d, lambda i, j: i * nq + j)
    kern = functools.partial(_attn_kernel, nq=nq, min_blocks=min_blocks, topk=topk, alpha=alpha)
    return pl.pallas_call(
        kern,
        out_shape=(jax.ShapeDtypeStruct((bsz, s, d), F32),) + r_shape,
        grid=(bsz, nq),
        in_specs=[
            pl.BlockSpec((None, N_HEADS, Q_BLOCK, KV_LATENT), lambda i, j: (i, 0, j, 0)),
            pl.BlockSpec((None, Q_BLOCK, nqi), lambda i, j: (i, j, 0)),
            pl.BlockSpec((None, N_IDX_HEADS, Q_BLOCK), lambda i, j: (i, 0, j)),
            pl.BlockSpec((None, s, IDX_DIM), lambda i, j: (i, 0, 0)),
            pl.BlockSpec((None, s, KV_LATENT), lambda i, j: (i, 0, 0)),
            pl.BlockSpec((None, KV_LATENT, s), lambda i, j: (i, 0, 0)),
            pl.BlockSpec((N_HEADS, BIAS_ROWS, Q_BLOCK), lambda i, j: (0, 0, 0)),
            pl.BlockSpec((None, Q_BLOCK, d), lambda i, j: (i, j, 0)),
            pl.BlockSpec((N_HEADS * KV_LATENT, d), lambda i, j: (0, 0)),
            pl.BlockSpec((1, d), lambda i, j: (0, 0)),
            pl.BlockSpec((1, d), lambda i, j: (0, 0)),
        ] + r_in,
        out_specs=(pl.BlockSpec((None, Q_BLOCK, d), lambda i, j: (i, j, 0)),) + r_out,
        scratch_shapes=[
            pltpu.VMEM((s, Q_BLOCK), I32),
            pltpu.VMEM((s, Q_BLOCK), F32),
            pltpu.VMEM((2, s, 2 * Q_BLOCK), F32),
            pltpu.VMEM((2, s, 2 * Q_BLOCK), BF16),
            pltpu.VMEM((N_HEADS, Q_BLOCK, KV_LATENT), BF16),
        ] + r_scratch,
        compiler_params=_cparams(("arbitrary", "arbitrary"), 48),
        name="dsa_attention_ln",
    )(q, qidx, widxt, kidx, ckv, ckvt, bias_t, x,
      w_out.astype(BF16), g.reshape(1, d), b.reshape(1, d), *router)


def _split_bf16(a):
    hi = a.astype(BF16)
    lo = (a - hi.astype(F32)).astype(BF16)
    return hi, lo


def _route_tile(x, first, w_ref, b_ref, eid_ref, gate_ref, rank_ref, cnt_ref, base_ref, u_ref):
    tt = x.shape[0]

    @pl.when(first)
    def _():
        base_ref[...] = jnp.zeros_like(base_ref)
        u_ref[...] = jnp.where(lax.broadcasted_iota(I32, (tt, tt), 0)
                               < lax.broadcasted_iota(I32, (tt, tt), 1), 1.0, 0.0).astype(BF16)

    xh, xl = _split_bf16(x)
    wh, wl = _split_bf16(w_ref[...])
    lt = _dot_nt(wh, xh) + (_dot_nt(wh, xl) + _dot_nt(wl, xh)) + b_ref[...]

    ng, ne = N_GROUPS, EXPERTS_PER_GROUP
    gl = lt[0:ng]
    iog = lax.broadcasted_iota(I32, (ng, tt), 0).astype(F32)
    gmax = jnp.max(gl, axis=0, keepdims=True)
    gidx = jnp.min(jnp.where(gl == gmax, iog, float(ng)), axis=0, keepdims=True)
    g_gate = 1.0 / jnp.sum(jnp.exp(gl - gmax), axis=0, keepdims=True)

    el = jnp.zeros((ne, tt), F32)
    for gi in range(ng):
        el = jnp.where(gidx == float(gi), lt[ng + gi * ne:ng + (gi + 1) * ne], el)
    ioe = lax.broadcasted_iota(I32, (ne, tt), 0).astype(F32)
    m1 = jnp.max(el, axis=0, keepdims=True)
    i1 = jnp.min(jnp.where(el == m1, ioe, float(ne)), axis=0, keepdims=True)
    el2 = jnp.where(ioe == i1, NEG_INF, el)
    m2 = jnp.max(el2, axis=0, keepdims=True)
    i2 = jnp.min(jnp.where(el2 == m2, ioe, float(ne)), axis=0, keepdims=True)
    ex = jnp.exp(m2 - m1)
    p1 = 1.0 / (1.0 + ex)
    gate_ref[0:1, :] = p1 * g_gate
    gate_ref[1:2, :] = ex * p1 * g_gate
    e1 = gidx * float(ne) + i1
    e2 = gidx * float(ne) + i2
    eid_ref[0:1, :] = e1.astype(I32)
    eid_ref[1:2, :] = e2.astype(I32)

    iox = lax.broadcasted_iota(I32, (N_EXPERTS, tt), 0).astype(F32)
    oh1 = jnp.where(iox == e1, 1.0, 0.0)
    oh2 = jnp.where(iox == e2, 1.0, 0.0)
    pre1 = _dot(oh1.astype(BF16), u_ref[...])
    pre2 = _dot(oh2.astype(BF16), u_ref[...])
    tot1 = jnp.sum(oh1, axis=1, keepdims=True)
    tot2 = jnp.sum(oh2, axis=1, keepdims=True)
    base = base_ref[...]
    rank_ref[0:1, :] = jnp.sum(oh1 * (base + pre1), axis=0, keepdims=True).astype(I32)
    rank_ref[1:2, :] = jnp.sum(oh2 * (base + tot1 + pre2), axis=0, keepdims=True).astype(I32)
    base = base + tot1 + tot2
    base_ref[...] = base
    cnt_ref[...] = jnp.broadcast_to(base, cnt_ref.shape).astype(I32)


def _router_operands(wg, bg, we, be):
    rows = V7X_LANES
    pad = rows - N_GROUPS - N_EXPERTS
    wcat = jnp.pad(jnp.concatenate([wg, we], axis=1).T, ((0, pad), (0, 0)))
    bcat = jnp.pad(jnp.concatenate([bg, be]), (0, pad)).reshape(rows, 1)
    return wcat, bcat


def _router_specs(t, tt, d, tile_index):
    def const2(*_):
        return (0, 0)

    def tile(*g):
        return (0, tile_index(*g))

    in_specs = [pl.BlockSpec((V7X_LANES, d), const2), pl.BlockSpec((V7X_LANES, 1), const2)]
    out_shape = (
        jax.ShapeDtypeStruct((TOPK_IN_GROUP, t), I32),
        jax.ShapeDtypeStruct((TOPK_IN_GROUP, t), F32),
        jax.ShapeDtypeStruct((TOPK_IN_GROUP, t), I32),
        jax.ShapeDtypeStruct((N_EXPERTS, V7X_LANES), I32),
    )
    out_specs = (
        pl.BlockSpec((TOPK_IN_GROUP, tt), tile),
        pl.BlockSpec((TOPK_IN_GROUP, tt), tile),
        pl.BlockSpec((TOPK_IN_GROUP, tt), tile),
        pl.BlockSpec((N_EXPERTS, V7X_LANES), const2),
    )
    scratch = [pltpu.VMEM((N_EXPERTS, 1), F32), pltpu.VMEM((tt, tt), BF16)]
    return in_specs, out_shape, out_specs, scratch


def _dest_kernel(cnt_ref, eid_ref, rank_ref, dest_ref, blke_ref, meta_ref, pstart_ref, *, nblk):
    shift = MOE_BLOCK.bit_length() - 1

    def expert_body(e, acc):
        nb = (cnt_ref[e] + (MOE_BLOCK - 1)) >> shift
        pstart_ref[e] = acc
        b0 = acc >> shift

        def blk_body(j, c):
            blke_ref[b0 + j] = e
            return c

        lax.fori_loop(0, nb, blk_body, 0)
        return acc + (nb << shift)

    total = lax.fori_loop(0, N_EXPERTS, expert_body, jnp.int32(0))
    nused = total >> shift
    meta_ref[0] = nused
    last_e = blke_ref[nused - 1]

    def tail_body(j, c):
        blke_ref[j] = last_e
        return c

    lax.fori_loop(nused, nblk, tail_body, 0)

    def dest_body(e, dest):
        return dest + jnp.where(eid_ref[...] == e, pstart_ref[e], 0)

    dest_ref[...] = lax.fori_loop(0, N_EXPERTS, dest_body, rank_ref[...])


def _dest(cnt, eid, rank, nblk):
    t = eid.shape[1]
    kern = functools.partial(_dest_kernel, nblk=nblk)
    return pl.pallas_call(
        kern,
        out_shape=(
            jax.ShapeDtypeStruct((TOPK_IN_GROUP, t), I32),
            jax.ShapeDtypeStruct((nblk,), I32),
            jax.ShapeDtypeStruct((1,), I32),
        ),
        in_specs=[
            pl.BlockSpec(memory_space=pltpu.SMEM),
            pl.BlockSpec(memory_space=pltpu.VMEM),
            pl.BlockSpec(memory_space=pltpu.VMEM),
        ],
        out_specs=(
            pl.BlockSpec(memory_space=pltpu.VMEM),
            pl.BlockSpec(memory_space=pltpu.SMEM),
            pl.BlockSpec(memory_space=pltpu.SMEM),
        ),
        scratch_shapes=[pltpu.SMEM((N_EXPERTS,), I32)],
        name="moe_dest",
    )(cnt, eid, rank)


def _row_copy(src_ref, src_row, dst_ref, dst_row, sem):
    return pltpu.make_async_copy(src_ref.at[pl.ds(src_row, 1)], dst_ref.at[pl.ds(dst_row, 1)], sem)


def _wait_rows(hbm_ref, vmem_rows_ref, sem):
    n = vmem_rows_ref.shape[0]
    pltpu.make_async_copy(hbm_ref.at[pl.ds(0, n)], vmem_rows_ref, sem).wait()


def _pack_bf16_pairs(x):
    half = x.shape[1] // 2
    hi = pltpu.bitcast(x[:, :half].astype(BF16).astype(F32), U32)
    lo = pltpu.bitcast(x[:, half:].astype(BF16).astype(F32), U32)
    return hi | (lo >> 16)


def _unpack_bf16_pairs(u):
    hi = pltpu.bitcast(u & jnp.uint32(0xFFFF0000), F32).astype(BF16)
    lo = pltpu.bitcast(u << 16, F32).astype(BF16)
    return jnp.concatenate([hi, lo], axis=1)


def _scatter_kernel(dest_ref, x_ref, xs_in_ref, xs_ref, stage_ref, sem, *, tr, t, nsteps):
    del xs_in_ref
    i = pl.program_id(0)
    slot = i % 2
    base = i * tr

    def drain(s):
        for _ in range(TOPK_IN_GROUP):
            _wait_rows(xs_ref, stage_ref.at[s], sem.at[s])

    @pl.when(i >= 2)
    def _():
        drain(slot)

    stage_ref[slot] = _pack_bf16_pairs(x_ref[...])

    def issue(r, c):
        for k in range(TOPK_IN_GROUP):
            _row_copy(stage_ref.at[slot], r, xs_ref, dest_ref[k * t + base + r], sem.at[slot]).start()
        return c

    lax.fori_loop(0, tr, issue, 0, unroll=4)

    @pl.when(i == nsteps - 1)
    def _():
        drain(slot)
        if nsteps > 1:
            drain(1 - slot)


def _scatter(dest_flat, xt, nrows):
    t, d = xt.shape
    tr = 256
    nsteps = t // tr
    kern = functools.partial(_scatter_kernel, tr=tr, t=t, nsteps=nsteps)
    return pl.pallas_call(
        kern,
        out_shape=jax.ShapeDtypeStruct((nrows, d // 2), U32),
        grid_spec=pltpu.PrefetchScalarGridSpec(
            num_scalar_prefetch=1,
            grid=(t // tr,),
            in_specs=[
                pl.BlockSpec((tr, d), lambda i, dest: (i, 0)),
                pl.BlockSpec(memory_space=pl.ANY),
            ],
            out_specs=pl.BlockSpec(memory_space=pl.ANY),
            scratch_shapes=[pltpu.VMEM((2, tr, d // 2), U32), pltpu.SemaphoreType.DMA((2,))],
        ),
        input_output_aliases={2: 0},
        compiler_params=_cparams(("arbitrary",), 32),
        name="moe_scatter_rows",
    )(dest_flat, xt, jnp.zeros((nrows, d // 2), U32))


def _gmm_kernel(blke_ref, meta_ref, xs_ref, w1_ref, w3_ref, w2_ref, ys_ref, w1b_ref, w3b_ref, w2b_ref):
    nb = pl.program_id(0)

    @pl.when(nb < meta_ref[0])
    def _():
        @pl.when((nb == 0) | (blke_ref[nb] != blke_ref[jnp.maximum(nb - 1, 0)]))
        def _():
            w1b_ref[...] = w1_ref[...].astype(BF16)
            w3b_ref[...] = w3_ref[...].astype(BF16)
            w2b_ref[...] = w2_ref[...].astype(BF16)

        xb = _unpack_bf16_pairs(xs_ref[...])
        h1 = _dot(xb, w1b_ref[...])
        h3 = _dot(xb, w3b_ref[...])
        hh = (h1 * jax.nn.sigmoid(h1) * h3).astype(BF16)
        ys_ref[...] = _dot(hh, w2b_ref[...])

    @pl.when(nb >= meta_ref[0])
    def _():
        ys_ref[...] = jnp.zeros_like(ys_ref)


def _gmm(blke, meta, xs, w1, w3, w2, layer):
    nrows = xs.shape[0]
    d, de = w1.shape[-2:]
    nblk = nrows // MOE_BLOCK

    def row_map(i, blke, meta):
        return (jnp.minimum(i, meta[0] - 1), 0)

    def w_map(i, blke, meta):
        return (layer, blke[i], 0, 0)

    return pl.pallas_call(
        _gmm_kernel,
        out_shape=jax.ShapeDtypeStruct((nrows, d), F32),
        grid_spec=pltpu.PrefetchScalarGridSpec(
            num_scalar_prefetch=2,
            grid=(nblk,),
            in_specs=[
                pl.BlockSpec((MOE_BLOCK, d // 2), row_map),
                pl.BlockSpec((None, None, d, de), w_map),
                pl.BlockSpec((None, None, d, de), w_map),
                pl.BlockSpec((None, None, de, d), w_map),
            ],
            out_specs=pl.BlockSpec((MOE_BLOCK, d), lambda i, blke, meta: (i, 0)),
            scratch_shapes=[pltpu.VMEM((d, de), BF16), pltpu.VMEM((d, de), BF16),
                            pltpu.VMEM((de, d), BF16)],
        ),
        compiler_params=_cparams(("arbitrary",), 32),
        name="moe_experts",
    )(blke, meta, xs, w1, w3, w2)


def _combine_kernel(dest_ref, ys_ref, x_ref, gate_ref, g_ref, b_ref, o_ref, buf_ref, sem,
                    *, tr, t, nsteps, alpha):
    i = pl.program_id(0)
    slot = i % 2

    def gather(step, s):
        def issue(r, c):
            for k in range(TOPK_IN_GROUP):
                _row_copy(ys_ref, dest_ref[k * t + step * tr + r], buf_ref.at[s], k * tr + r,
                          sem.at[s]).start()
            return c

        lax.fori_loop(0, tr, issue, 0, unroll=4)

    @pl.when(i == 0)
    def _():
        gather(0, 0)

    @pl.when(i + 1 < nsteps)
    def _():
        gather(i + 1, 1 - slot)

    _wait_rows(ys_ref, buf_ref.at[slot], sem.at[slot])
    gate = gate_ref[...]
    f = buf_ref[slot, 0:tr] * gate[:, 0:1] + buf_ref[slot, tr:2 * tr] * gate[:, 1:2]
    o_ref[...] = _layer_norm(alpha * x_ref[...] + f, g_ref[...], b_ref[...])


def _combine(dest_flat, ys, xt, gate_t, g, b, alpha):
    t, d = xt.shape
    tr = 256
    nsteps = t // tr
    kern = functools.partial(_combine_kernel, tr=tr, t=t, nsteps=nsteps, alpha=alpha)
    return pl.pallas_call(
        kern,
        out_shape=jax.ShapeDtypeStruct((t, d), F32),
        grid_spec=pltpu.PrefetchScalarGridSpec(
            num_scalar_prefetch=1,
            grid=(t // tr,),
            in_specs=[
                pl.BlockSpec(memory_space=pl.ANY),
                pl.BlockSpec((tr, d), lambda i, dest: (i, 0)),
                pl.BlockSpec((tr, TOPK_IN_GROUP), lambda i, dest: (i, 0)),
                pl.BlockSpec((1, d), lambda i, dest: (0, 0)),
                pl.BlockSpec((1, d), lambda i, dest: (0, 0)),
            ],
            out_specs=pl.BlockSpec((tr, d), lambda i, dest: (i, 0)),
            scratch_shapes=[pltpu.VMEM((2, TOPK_IN_GROUP * tr, d), F32),
                            pltpu.SemaphoreType.DMA((2,))],
        ),
        compiler_params=_cparams(("arbitrary",), 32),
        name="moe_combine_ln",
    )(dest_flat, ys, xt, gate_t, g.reshape(1, d), b.reshape(1, d))


def _moe_layer(xt, routing, w1, w3, w2, layer, g, b, alpha):
    t, d = xt.shape
    nblk = -(-t * TOPK_IN_GROUP // MOE_BLOCK) + N_EXPERTS
    eid, gate, rank, cnt = routing
    dest, blke, meta = _dest(cnt[:, 0], eid, rank, nblk)
    dest_flat = dest.reshape(-1)
    xs = _scatter(dest_flat, xt, nblk * MOE_BLOCK)
    ys = _gmm(blke, meta, xs, w1, w3, w2, layer)
    return _combine(dest_flat, ys, xt, gate.T, g, b, alpha)


def kernel(x, conv_w_in, conv_k, conv_w_out, attn_w_in, kv_norm_g, kidx_ln_g, kidx_ln_b, attn_w_out, rel_bias, router_wg, router_bg, router_we, router_be, exp_w1, exp_w3, exp_w2, ln1_g, ln1_b, ln2_g, ln2_b):
    bsz, s, d = x.shape
    depth = ln1_g.shape[0]
    n_mixers = 2
    alpha = (2.0 * depth) ** 0.25
    bias_t = _bias_tiles(rel_bias)
    for i in range(depth):
        j = i // n_mixers
        router = _router_operands(router_wg[i], router_bg[i], router_we[i], router_be[i])
        if i % n_mixers == 0:
            x, *routing = _conv_layer(x, conv_w_in[j], conv_k[j], conv_w_out[j], ln1_g[i], ln1_b[i],
                                      router, alpha)
        else:
            x, *routing = _attn_layer(x, attn_w_in[j], kv_norm_g[j], kidx_ln_g[j], kidx_ln_b[j],
                                      attn_w_out[j], bias_t, ln1_g[i], ln1_b[i], router, alpha)
        xt = _moe_layer(x.reshape(bsz * s, d), routing, exp_w1, exp_w3, exp_w2, i,
                        ln2_g[i], ln2_b[i], alpha)
        x = xt.reshape(bsz, s, d)
    return x
```

```python
import functools
import math

import jax
import jax.numpy as jnp
from jax import lax
from jax.experimental import pallas as pl
from jax.experimental.pallas import tpu as pltpu

CONV_WIDTH = 3
N_HEADS = 16
KV_LATENT = 128
N_IDX_HEADS = 8
IDX_DIM = 64
TOPK_MAX = 256
Q_BLOCK = 128
N_BUCKETS = 32
MAX_DISTANCE = 128
N_GROUPS = 8
EXPERTS_PER_GROUP = 8
N_EXPERTS = N_GROUPS * EXPERTS_PER_GROUP
TOPK_IN_GROUP = 2
MOE_BLOCK = 256
LN_EPS = 1e-5
RMS_EPS = 1e-6

V7X_LANES = 128
V7X_SUBLANES = 8
V7X_VMEM_BYTES = 64 * 1024 * 1024

F32 = jnp.float32
BF16 = jnp.bfloat16
I32 = jnp.int32
U32 = jnp.uint32
NEG_INF = float("-inf")
INT32_MIN = -(2 ** 31)
LOG2E = math.log2(math.e)
LOGIT_SCALE2 = (KV_LATENT ** -0.5) * LOG2E
BIAS_ROWS = 4 * Q_BLOCK

_NT = (((1,), (1,)), ((), ()))


def _dot(a, b):
    return jnp.dot(a, b, preferred_element_type=F32)


def _dot_nt(a, b):
    return lax.dot_general(a, b, _NT, preferred_element_type=F32)


def _layer_norm(z, g, b):
    mu = jnp.mean(z, axis=-1, keepdims=True)
    zc = z - mu
    var = jnp.mean(zc * zc, axis=-1, keepdims=True)
    return zc * lax.rsqrt(var + LN_EPS) * g + b


def _cparams(semantics, vmem_mib):
    assert vmem_mib * 1024 * 1024 < V7X_VMEM_BYTES
    return pltpu.CompilerParams(dimension_semantics=semantics,
                                vmem_limit_bytes=vmem_mib * 1024 * 1024)


def _conv_kernel(x_ref, win_ref, ck_ref, wout_ref, g_ref, b_ref, o_ref, carry_ref, gbuf_ref,
                 *, ts, d, cw, alpha):
    @pl.when(pl.program_id(1) == 0)
    def _():
        carry_ref[...] = jnp.zeros_like(carry_ref)

    x = x_ref[...]
    xb = x.astype(BF16)
    row = lax.broadcasted_iota(I32, (ts, cw), 0)
    for c in range(d // cw):
        lo, hi = c * cw, (c + 1) * cw
        bg = _dot(xb, win_ref[:, lo:hi])
        cg = _dot(xb, win_ref[:, d + lo:d + hi])
        hh = _dot(xb, win_ref[:, 2 * d + lo:2 * d + hi])
        u = cg * hh
        prev = carry_ref[:, lo:hi]
        u1 = jnp.where(row == 0, prev[7:8], pltpu.roll(u, 1, 0))
        u2 = jnp.where(row == 0, prev[6:7], jnp.where(row == 1, prev[7:8], pltpu.roll(u, 2, 0)))
        k = ck_ref[:, lo:hi]
        conv = u2 * k[0:1] + u1 * k[1:2] + u * k[2:3]
        gbuf_ref[:, lo:hi] = (bg * conv).astype(BF16)
        carry_ref[:, lo:hi] = u[ts - V7X_SUBLANES:ts]
    y = _dot(gbuf_ref[...], wout_ref[...])
    o_ref[...] = _layer_norm(alpha * x + y, g_ref[...], b_ref[...])


def _conv_layer(x, w_in, conv_k, w_out, g, b, alpha):
    bsz, s, d = x.shape
    ts, cw = 512, 512
    kern = functools.partial(_conv_kernel, ts=ts, d=d, cw=cw, alpha=alpha)
    return pl.pallas_call(
        kern,
        out_shape=jax.ShapeDtypeStruct((bsz, s, d), F32),
        grid=(bsz, s // ts),
        in_specs=[
            pl.BlockSpec((None, ts, d), lambda i, j: (i, j, 0)),
            pl.BlockSpec((d, 3 * d), lambda i, j: (0, 0)),
            pl.BlockSpec((CONV_WIDTH, d), lambda i, j: (0, 0)),
            pl.BlockSpec((d, d), lambda i, j: (0, 0)),
            pl.BlockSpec((1, d), lambda i, j: (0, 0)),
            pl.BlockSpec((1, d), lambda i, j: (0, 0)),
        ],
        out_specs=pl.BlockSpec((None, ts, d), lambda i, j: (i, j, 0)),
        scratch_shapes=[pltpu.VMEM((V7X_SUBLANES, d), F32), pltpu.VMEM((ts, d), BF16)],
        compiler_params=_cparams(("arbitrary", "arbitrary"), 48),
        name="conv_mixer_ln",
    )(x, w_in.astype(BF16), conv_k, w_out.astype(BF16), g.reshape(1, d), b.reshape(1, d))


def _proj_kernel(x_ref, wq_ref, ws_ref, ww_ref, kvg_ref, lng_ref, lnb_ref,
                 q_ref, ckv_ref, ckvt_ref, qidx_ref, kidx_ref, widxt_ref, *, idx_scale):
    xb = x_ref[...].astype(BF16)
    q = _dot(xb, wq_ref[...]) * LOGIT_SCALE2
    for h in range(N_HEADS):
        q_ref[h] = q[:, h * KV_LATENT:(h + 1) * KV_LATENT].astype(BF16)
    sm = _dot(xb, ws_ref[...])
    ckv = sm[:, :KV_LATENT]
    ckv = ckv * lax.rsqrt(jnp.mean(ckv * ckv, axis=-1, keepdims=True) + RMS_EPS) * kvg_ref[...]
    ckv_ref[...] = ckv.astype(BF16)
    ckvt_ref[...] = ckv.T.astype(BF16)
    nq = N_IDX_HEADS * IDX_DIM
    for h in range(N_IDX_HEADS):
        lo = KV_LATENT + h * IDX_DIM
        qidx_ref[h] = sm[:, lo:lo + IDX_DIM].astype(BF16)
    kidx = sm[:, KV_LATENT + nq:KV_LATENT + nq + IDX_DIM]
    kidx_ref[...] = _layer_norm(kidx, lng_ref[...], lnb_ref[...]).astype(BF16)
    widxt_ref[...] = _dot_nt(ww_ref[...], xb) * idx_scale


def _attn_proj(x, w_in, kv_g, ln_g, ln_b):
    bsz, s, d = x.shape
    ts = 512
    hq = N_HEADS * KV_LATENT
    nq = N_IDX_HEADS * IDX_DIM
    small = KV_LATENT + nq + IDX_DIM
    small_pad = -(-small // V7X_LANES) * V7X_LANES
    wq = w_in[:, :hq].astype(BF16)
    ws = jnp.pad(w_in[:, hq:hq + small], ((0, 0), (0, small_pad - small))).astype(BF16)
    ww = w_in[:, hq + small:].T.astype(BF16)
    idx_scale = (N_IDX_HEADS ** -0.5) * (IDX_DIM ** -0.5)
    kern = functools.partial(_proj_kernel, idx_scale=idx_scale)
    return pl.pallas_call(
        kern,
        out_shape=(
            jax.ShapeDtypeStruct((bsz, N_HEADS, s, KV_LATENT), BF16),
            jax.ShapeDtypeStruct((bsz, s, KV_LATENT), BF16),
            jax.ShapeDtypeStruct((bsz, KV_LATENT, s), BF16),
            jax.ShapeDtypeStruct((bsz, N_IDX_HEADS, s, IDX_DIM), BF16),
            jax.ShapeDtypeStruct((bsz, s, IDX_DIM), BF16),
            jax.ShapeDtypeStruct((bsz, N_IDX_HEADS, s), F32),
        ),
        grid=(bsz, s // ts),
        in_specs=[
            pl.BlockSpec((None, ts, d), lambda i, j: (i, j, 0)),
            pl.BlockSpec((d, hq), lambda i, j: (0, 0)),
            pl.BlockSpec((d, small_pad), lambda i, j: (0, 0)),
            pl.BlockSpec((N_IDX_HEADS, d), lambda i, j: (0, 0)),
            pl.BlockSpec((1, KV_LATENT), lambda i, j: (0, 0)),
            pl.BlockSpec((1, IDX_DIM), lambda i, j: (0, 0)),
            pl.BlockSpec((1, IDX_DIM), lambda i, j: (0, 0)),
        ],
        out_specs=(
            pl.BlockSpec((None, N_HEADS, ts, KV_LATENT), lambda i, j: (i, 0, j, 0)),
            pl.BlockSpec((None, ts, KV_LATENT), lambda i, j: (i, j, 0)),
            pl.BlockSpec((None, KV_LATENT, ts), lambda i, j: (i, 0, j)),
            pl.BlockSpec((None, N_IDX_HEADS, ts, IDX_DIM), lambda i, j: (i, 0, j, 0)),
            pl.BlockSpec((None, ts, IDX_DIM), lambda i, j: (i, j, 0)),
            pl.BlockSpec((None, N_IDX_HEADS, ts), lambda i, j: (i, 0, j)),
        ),
        compiler_params=_cparams(("arbitrary", "arbitrary"), 48),
        name="attn_proj",
    )(x, wq, ws, ww, kv_g.reshape(1, -1), ln_g.reshape(1, -1), ln_b.reshape(1, -1))


def _bias_kernel(rb_ref, o_ref):
    rows = o_ref.shape[1]
    j = lax.broadcasted_iota(I32, (rows, Q_BLOCK), 0)
    r = lax.broadcasted_iota(I32, (rows, Q_BLOCK), 1)
    dist = 2 * Q_BLOCK + r - j
    dpos = jnp.maximum(dist, 0)
    max_exact = N_BUCKETS // 2
    d_f = jnp.maximum(dpos, 1).astype(F32)
    large = max_exact + (jnp.log(d_f / max_exact) / math.log(MAX_DISTANCE / max_exact)
                         * (N_BUCKETS - max_exact)).astype(I32)
    large = jnp.minimum(large, N_BUCKETS - 1)
    bucket = jnp.where(dpos < max_exact, dpos, large)
    for h in range(N_HEADS):
        acc = jnp.zeros((rows, Q_BLOCK), F32)
        for bk in range(N_BUCKETS):
            acc = jnp.where(bucket == bk, rb_ref[bk, h], acc)
        o_ref[h] = jnp.where(dist >= 0, (acc - rb_ref[N_BUCKETS - 1, h]) * LOG2E, 0.0)


def _bias_tiles(rel_bias):
    return pl.pallas_call(
        _bias_kernel,
        out_shape=jax.ShapeDtypeStruct((N_HEADS, BIAS_ROWS, Q_BLOCK), F32),
        in_specs=[pl.BlockSpec(memory_space=pltpu.SMEM)],
        out_specs=pl.BlockSpec(memory_space=pltpu.VMEM),
        name="rel_bias_tiles",
    )(rel_bias)


def _rows_reduce(parts, op):
    accs = [None, None]
    for c, p in enumerate(parts):
        accs[c % 2] = p if accs[c % 2] is None else op(accs[c % 2], p)
    return accs[0] if accs[1] is None else op(accs[0], accs[1])


def _attn_body(qi, q_ref, qidx_ref, widxt_ref, kidx_ref, ckv_ref, ckvt_ref, bias_ref, x_ref,
               wout_ref, g_ref, b_ref, o_ref, key_ref, madd_ref, lg_ref, eb_ref, obuf_ref,
               *, sk, first_bias_chunk, topk, alpha):
    ch = Q_BLOCK
    n = sk // ch
    t_abs = qi * Q_BLOCK + lax.broadcasted_iota(I32, (1, Q_BLOCK), 1)
    s_abs = lax.broadcasted_iota(I32, (sk, Q_BLOCK), 0)
    valid = s_abs <= t_abs

    kidx = kidx_ref[0:sk, :]
    score = jnp.zeros((sk, Q_BLOCK), F32)
    for hp in range(N_IDX_HEADS // 2):
        sh = _dot_nt(kidx, qidx_ref[2 * hp:2 * hp + 2].reshape(2 * Q_BLOCK, IDX_DIM))
        score = (score + jnp.maximum(sh[:, :Q_BLOCK], 0.0) * widxt_ref[2 * hp:2 * hp + 1, :]
                 + jnp.maximum(sh[:, Q_BLOCK:], 0.0) * widxt_ref[2 * hp + 1:2 * hp + 2, :])
    score = jnp.where(score == 0.0, 0.0, score)
    score = jnp.where(valid, score, NEG_INF)
    bits = pltpu.bitcast(score, I32)
    key_ref[0:sk] = bits ^ ((bits >> 31) & 0x7FFFFFFF)

    def count(pred):
        parts = [jnp.where(pred(key_ref[c * ch:(c + 1) * ch]), 1.0, 0.0) for c in range(n)]
        return jnp.sum(_rows_reduce(parts, jnp.add), axis=0, keepdims=True)

    def bit_body(it, thr):
        cand = thr + lax.shift_left(jnp.int32(1), 31 - it)
        return jnp.where(count(lambda k: k >= cand) >= topk, cand, thr)

    thr = lax.fori_loop(0, 32, bit_body, jnp.full((1, Q_BLOCK), INT32_MIN, I32))
    cnt_ge = count(lambda k: k >= thr)
    madd_ref[0:sk] = jnp.where((key_ref[0:sk] >= thr) & valid, 0.0, NEG_INF)

    tied = jnp.where((cnt_ge > topk) & (t_abs >= topk - 1), 1.0, 0.0)

    @pl.when(jnp.max(tied) > 0.0)
    def _():
        chunk = ch
        need = topk - count(lambda k: k > thr)
        tri = jnp.where(lax.broadcasted_iota(I32, (chunk, chunk), 0)
                        >= lax.broadcasted_iota(I32, (chunk, chunk), 1), 1.0, 0.0).astype(BF16)
        run = jnp.zeros((1, Q_BLOCK), F32)
        for c in range(sk // chunk):
            kc = key_ref[c * chunk:(c + 1) * chunk]
            tie = kc == thr
            pre = _dot(tri, jnp.where(tie, 1.0, 0.0).astype(BF16)) + run
            run = pre[chunk - 1:chunk]
            vc = (c * chunk + lax.broadcasted_iota(I32, (chunk, Q_BLOCK), 0)) <= t_abs
            sel = ((kc > thr) | (tie & (pre <= need))) & vc
            madd_ref[c * chunk:(c + 1) * chunk] = jnp.where(sel, 0.0, NEG_INF)

    last_tile = BIAS_ROWS // ch - 1
    cr = 64
    nsteps = sk // cr
    npairs = N_HEADS // 2

    def stage_steps(p_logits, slot_logits, p_soft, slot_soft, m_soft):
        lg_w = eb = macc = sacc = logit = None
        if p_logits is not None:
            q2 = q_ref[pl.ds(2 * p_logits, 2)].reshape(2 * Q_BLOCK, KV_LATENT)
            logit = _dot_nt(ckv_ref[0:sk, :], q2)
            lg_w = lg_ref.at[slot_logits]
        if p_soft is not None:
            lg_r = lg_ref.at[slot_soft]
            eb = eb_ref.at[slot_soft]
        for c in range(nsteps):
            rows = slice(c * cr, (c + 1) * cr)
            if p_logits is not None:
                madd = madd_ref[rows]
                v = logit[rows] + jnp.concatenate([madd, madd], axis=1)
                blk = (c * cr) // ch
                if blk >= first_bias_chunk:
                    off = jnp.clip(2 - qi + blk, 0, last_tile) * ch + (c * cr) % ch
                    off = pl.multiple_of(off, cr)
                    v = v + jnp.concatenate([bias_ref[2 * p_logits, pl.ds(off, cr), :],
                                             bias_ref[2 * p_logits + 1, pl.ds(off, cr), :]], axis=1)
                lg_w[rows] = v
                macc = v if macc is None else jnp.maximum(macc, v)
            if p_soft is not None:
                e = jnp.exp2(lg_r[rows] - m_soft)
                eb[rows] = e.astype(BF16)
                sacc = e if sacc is None else sacc + e
        m_new = None
        if p_logits is not None:
            m_new = jnp.max(macc, axis=0, keepdims=True)
        if p_soft is not None:
            ssum = jnp.sum(sacc, axis=0, keepdims=True)
            ot = _dot(ckvt_ref[:, 0:sk], eb[0:sk]) / ssum
            obuf_ref[2 * p_soft] = ot[:, :Q_BLOCK].T.astype(BF16)
            obuf_ref[2 * p_soft + 1] = ot[:, Q_BLOCK:].T.astype(BF16)
        return m_new

    m0 = stage_steps(0, 0, None, None, None)

    def pipe_body(j, m_even):
        m_odd = stage_steps(2 * j + 1, 1, 2 * j, 0, m_even)
        return stage_steps(2 * j + 2, 0, 2 * j + 1, 1, m_odd)

    m_even = lax.fori_loop(0, npairs // 2 - 1, pipe_body, m0)
    m_odd = stage_steps(npairs - 1, 1, npairs - 2, 0, m_even)
    stage_steps(None, None, npairs - 1, 1, m_odd)
    o_all = jnp.concatenate([obuf_ref[h] for h in range(N_HEADS)], axis=1)
    acc = _dot(o_all, wout_ref[...])
    o_ref[...] = _layer_norm(alpha * x_ref[...] + acc, g_ref[...], b_ref[...])


def _attn_kernel(*refs, nv, per, topk, alpha):
    qi = pl.program_id(1)
    for v in range(nv):
        body = functools.partial(_attn_body, qi, *refs, sk=(v + 1) * per * Q_BLOCK,
                                 first_bias_chunk=v * per - 1, topk=topk, alpha=alpha)
        pl.when(qi // per == v)(body)


def _attn_layer(x, w_in, kv_g, ln_g, ln_b, w_out, bias_t, g, b, alpha):
    bsz, s, d = x.shape
    nq = s // Q_BLOCK
    topk = min(TOPK_MAX, s // 4)
    per = -(-topk // Q_BLOCK)
    assert nq % per == 0
    nv = nq // per
    q, ckv, ckvt, qidx, kidx, widxt = _attn_proj(x, w_in, kv_g, ln_g, ln_b)
    kern = functools.partial(_attn_kernel, nv=nv, per=per, topk=topk, alpha=alpha)
    return pl.pallas_call(
        kern,
        out_shape=jax.ShapeDtypeStruct((bsz, s, d), F32),
        grid=(bsz, nq),
        in_specs=[
            pl.BlockSpec((None, N_HEADS, Q_BLOCK, KV_LATENT), lambda i, j: (i, 0, j, 0)),
            pl.BlockSpec((None, N_IDX_HEADS, Q_BLOCK, IDX_DIM), lambda i, j: (i, 0, j, 0)),
            pl.BlockSpec((None, N_IDX_HEADS, Q_BLOCK), lambda i, j: (i, 0, j)),
            pl.BlockSpec((None, s, IDX_DIM), lambda i, j: (i, 0, 0)),
            pl.BlockSpec((None, s, KV_LATENT), lambda i, j: (i, 0, 0)),
            pl.BlockSpec((None, KV_LATENT, s), lambda i, j: (i, 0, 0)),
            pl.BlockSpec((N_HEADS, BIAS_ROWS, Q_BLOCK), lambda i, j: (0, 0, 0)),
            pl.BlockSpec((None, Q_BLOCK, d), lambda i, j: (i, j, 0)),
            pl.BlockSpec((N_HEADS * KV_LATENT, d), lambda i, j: (0, 0)),
            pl.BlockSpec((1, d), lambda i, j: (0, 0)),
            pl.BlockSpec((1, d), lambda i, j: (0, 0)),
        ],
        out_specs=pl.BlockSpec((None, Q_BLOCK, d), lambda i, j: (i, j, 0)),
        scratch_shapes=[
            pltpu.VMEM((s, Q_BLOCK), I32),
            pltpu.VMEM((s, Q_BLOCK), F32),
            pltpu.VMEM((2, s, 2 * Q_BLOCK), F32),
            pltpu.VMEM((2, s, 2 * Q_BLOCK), BF16),
            pltpu.VMEM((N_HEADS, Q_BLOCK, KV_LATENT), BF16),
        ],
        compiler_params=_cparams(("arbitrary", "arbitrary"), 48),
        name="dsa_attention_ln",
    )(q, qidx, widxt, kidx, ckv, ckvt, bias_t, x,
      w_out.astype(BF16), g.reshape(1, d), b.reshape(1, d))


def _split_bf16(a):
    hi = a.astype(BF16)
    lo = (a - hi.astype(F32)).astype(BF16)
    return hi, lo


def _router_kernel(x_ref, w_ref, b_ref, eid_ref, gate_ref, rank_ref, cnt_ref, base_ref, u_ref, *, tt):
    @pl.when(pl.program_id(0) == 0)
    def _():
        base_ref[...] = jnp.zeros_like(base_ref)
        u_ref[...] = jnp.where(lax.broadcasted_iota(I32, (tt, tt), 0)
                               < lax.broadcasted_iota(I32, (tt, tt), 1), 1.0, 0.0).astype(BF16)

    xh, xl = _split_bf16(x_ref[...])
    wh, wl = _split_bf16(w_ref[...])
    lt = _dot_nt(wh, xh) + (_dot_nt(wh, xl) + _dot_nt(wl, xh)) + b_ref[...]

    ng, ne = N_GROUPS, EXPERTS_PER_GROUP
    gl = lt[0:ng]
    iog = lax.broadcasted_iota(I32, (ng, tt), 0).astype(F32)
    gmax = jnp.max(gl, axis=0, keepdims=True)
    gidx = jnp.min(jnp.where(gl == gmax, iog, float(ng)), axis=0, keepdims=True)
    g_gate = 1.0 / jnp.sum(jnp.exp(gl - gmax), axis=0, keepdims=True)

    el = jnp.zeros((ne, tt), F32)
    for gi in range(ng):
        el = jnp.where(gidx == float(gi), lt[ng + gi * ne:ng + (gi + 1) * ne], el)
    ioe = lax.broadcasted_iota(I32, (ne, tt), 0).astype(F32)
    m1 = jnp.max(el, axis=0, keepdims=True)
    i1 = jnp.min(jnp.where(el == m1, ioe, float(ne)), axis=0, keepdims=True)
    el2 = jnp.where(ioe == i1, NEG_INF, el)
    m2 = jnp.max(el2, axis=0, keepdims=True)
    i2 = jnp.min(jnp.where(el2 == m2, ioe, float(ne)), axis=0, keepdims=True)
    ex = jnp.exp(m2 - m1)
    p1 = 1.0 / (1.0 + ex)
    gate_ref[0:1, :] = p1 * g_gate
    gate_ref[1:2, :] = ex * p1 * g_gate
    e1 = gidx * float(ne) + i1
    e2 = gidx * float(ne) + i2
    eid_ref[0:1, :] = e1.astype(I32)
    eid_ref[1:2, :] = e2.astype(I32)

    iox = lax.broadcasted_iota(I32, (N_EXPERTS, tt), 0).astype(F32)
    oh1 = jnp.where(iox == e1, 1.0, 0.0)
    oh2 = jnp.where(iox == e2, 1.0, 0.0)
    pre1 = _dot(oh1.astype(BF16), u_ref[...])
    pre2 = _dot(oh2.astype(BF16), u_ref[...])
    tot1 = jnp.sum(oh1, axis=1, keepdims=True)
    tot2 = jnp.sum(oh2, axis=1, keepdims=True)
    base = base_ref[...]
    rank_ref[0:1, :] = jnp.sum(oh1 * (base + pre1), axis=0, keepdims=True).astype(I32)
    rank_ref[1:2, :] = jnp.sum(oh2 * (base + tot1 + pre2), axis=0, keepdims=True).astype(I32)
    base = base + tot1 + tot2
    base_ref[...] = base
    cnt_ref[...] = jnp.broadcast_to(base, cnt_ref.shape).astype(I32)


def _router(xt, wg, bg, we, be):
    t, d = xt.shape
    tt = 512
    rows = V7X_LANES
    wcat = jnp.pad(jnp.concatenate([wg, we], axis=1).T, ((0, rows - N_GROUPS - N_EXPERTS), (0, 0)))
    bcat = jnp.pad(jnp.concatenate([bg, be]), (0, rows - N_GROUPS - N_EXPERTS)).reshape(rows, 1)
    kern = functools.partial(_router_kernel, tt=tt)
    return pl.pallas_call(
        kern,
        out_shape=(
            jax.ShapeDtypeStruct((TOPK_IN_GROUP, t), I32),
            jax.ShapeDtypeStruct((TOPK_IN_GROUP, t), F32),
            jax.ShapeDtypeStruct((TOPK_IN_GROUP, t), I32),
            jax.ShapeDtypeStruct((N_EXPERTS, V7X_LANES), I32),
        ),
        grid=(t // tt,),
        in_specs=[
            pl.BlockSpec((tt, d), lambda i: (i, 0)),
            pl.BlockSpec((rows, d), lambda i: (0, 0)),
            pl.BlockSpec((rows, 1), lambda i: (0, 0)),
        ],
        out_specs=(
            pl.BlockSpec((TOPK_IN_GROUP, tt), lambda i: (0, i)),
            pl.BlockSpec((TOPK_IN_GROUP, tt), lambda i: (0, i)),
            pl.BlockSpec((TOPK_IN_GROUP, tt), lambda i: (0, i)),
            pl.BlockSpec((N_EXPERTS, V7X_LANES), lambda i: (0, 0)),
        ),
        scratch_shapes=[pltpu.VMEM((N_EXPERTS, 1), F32), pltpu.VMEM((tt, tt), BF16)],
        compiler_params=_cparams(("arbitrary",), 32),
        name="moe_router",
    )(xt, wcat, bcat)


def _dest_kernel(cnt_ref, eid_ref, rank_ref, dest_ref, blke_ref, nxte_ref, meta_ref, pstart_ref, *, nblk):
    shift = MOE_BLOCK.bit_length() - 1

    def expert_body(e, acc):
        nb = (cnt_ref[e] + (MOE_BLOCK - 1)) >> shift
        pstart_ref[e] = acc
        b0 = acc >> shift

        def blk_body(j, c):
            blke_ref[b0 + j] = e
            return c

        lax.fori_loop(0, nb, blk_body, 0)
        return acc + (nb << shift)

    total = lax.fori_loop(0, N_EXPERTS, expert_body, jnp.int32(0))
    nused = total >> shift
    meta_ref[0] = nused
    last_e = blke_ref[nused - 1]

    def tail_body(j, c):
        blke_ref[j] = last_e
        nxte_ref[j] = -1
        return c

    lax.fori_loop(nused, nblk, tail_body, 0)

    def next_body(i, nxt):
        e = N_EXPERTS - 1 - i
        nb = (cnt_ref[e] + (MOE_BLOCK - 1)) >> shift
        b0 = pstart_ref[e] >> shift

        def blk_body(j, c):
            nxte_ref[b0 + j] = nxt
            return c

        lax.fori_loop(0, nb, blk_body, 0)
        return jnp.where(nb > 0, e, nxt)

    lax.fori_loop(0, N_EXPERTS, next_body, jnp.int32(-1))

    def dest_body(e, dest):
        return dest + jnp.where(eid_ref[...] == e, pstart_ref[e], 0)

    dest_ref[...] = lax.fori_loop(0, N_EXPERTS, dest_body, rank_ref[...])


def _dest(cnt, eid, rank, nblk):
    t = eid.shape[1]
    kern = functools.partial(_dest_kernel, nblk=nblk)
    return pl.pallas_call(
        kern,
        out_shape=(
            jax.ShapeDtypeStruct((TOPK_IN_GROUP, t), I32),
            jax.ShapeDtypeStruct((nblk,), I32),
            jax.ShapeDtypeStruct((nblk,), I32),
            jax.ShapeDtypeStruct((1,), I32),
        ),
        in_specs=[
            pl.BlockSpec(memory_space=pltpu.SMEM),
            pl.BlockSpec(memory_space=pltpu.VMEM),
            pl.BlockSpec(memory_space=pltpu.VMEM),
        ],
        out_specs=(
            pl.BlockSpec(memory_space=pltpu.VMEM),
            pl.BlockSpec(memory_space=pltpu.SMEM),
            pl.BlockSpec(memory_space=pltpu.SMEM),
            pl.BlockSpec(memory_space=pltpu.SMEM),
        ),
        scratch_shapes=[pltpu.SMEM((N_EXPERTS,), I32)],
        name="moe_dest",
    )(cnt, eid, rank)


def _row_copy(src_ref, src_row, dst_ref, dst_row, sem):
    return pltpu.make_async_copy(src_ref.at[pl.ds(src_row, 1)], dst_ref.at[pl.ds(dst_row, 1)], sem)


def _wait_rows(hbm_ref, vmem_rows_ref, sem):
    n = vmem_rows_ref.shape[0]
    pltpu.make_async_copy(hbm_ref.at[pl.ds(0, n)], vmem_rows_ref, sem).wait()


def _pack_bf16_pairs(x):
    half = x.shape[1] // 2
    hi = pltpu.bitcast(x[:, :half].astype(BF16).astype(F32), U32)
    lo = pltpu.bitcast(x[:, half:].astype(BF16).astype(F32), U32)
    return hi | (lo >> 16)


def _unpack_bf16_pairs(u):
    hi = pltpu.bitcast(u & jnp.uint32(0xFFFF0000), F32).astype(BF16)
    lo = pltpu.bitcast(u << 16, F32).astype(BF16)
    return jnp.concatenate([hi, lo], axis=1)


def _scatter_kernel(dest_ref, x_ref, xs_in_ref, xs_ref, stage_ref, sem, *, tr, t, nsteps):
    del xs_in_ref
    i = pl.program_id(0)
    slot = i % 2
    base = i * tr

    def drain(s):
        for _ in range(TOPK_IN_GROUP):
            _wait_rows(xs_ref, stage_ref.at[s], sem.at[s])

    @pl.when(i >= 2)
    def _():
        drain(slot)

    stage_ref[slot] = _pack_bf16_pairs(x_ref[...])

    def issue(r, c):
        for k in range(TOPK_IN_GROUP):
            _row_copy(stage_ref.at[slot], r, xs_ref, dest_ref[k * t + base + r], sem.at[slot]).start()
        return c

    lax.fori_loop(0, tr, issue, 0, unroll=4)

    @pl.when(i == nsteps - 1)
    def _():
        drain(slot)
        if nsteps > 1:
            drain(1 - slot)


def _scatter(dest_flat, xt, nrows):
    t, d = xt.shape
    tr = 256
    nsteps = t // tr
    kern = functools.partial(_scatter_kernel, tr=tr, t=t, nsteps=nsteps)
    return pl.pallas_call(
        kern,
        out_shape=jax.ShapeDtypeStruct((nrows, d // 2), U32),
        grid_spec=pltpu.PrefetchScalarGridSpec(
            num_scalar_prefetch=1,
            grid=(t // tr,),
            in_specs=[
                pl.BlockSpec((tr, d), lambda i, dest: (i, 0)),
                pl.BlockSpec(memory_space=pl.ANY),
            ],
            out_specs=pl.BlockSpec(memory_space=pl.ANY),
            scratch_shapes=[pltpu.VMEM((2, tr, d // 2), U32), pltpu.SemaphoreType.DMA((2,))],
        ),
        input_output_aliases={2: 0},
        compiler_params=_cparams(("arbitrary",), 32),
        name="moe_scatter_rows",
    )(dest_flat, xt, jnp.zeros((nrows, d // 2), U32))


def _gmm_kernel(blke_ref, nxte_ref, meta_ref, xs_ref, w1_ref, w3_ref, w2_ref, ys_ref,
                wf1_ref, wf3_ref, wf2_ref, w1b_ref, w3b_ref, w2b_ref, slot_ref, sem, *, layer):
    nb = pl.program_id(0)

    def weight_copies(e, s):
        return [pltpu.make_async_copy(w_ref.at[layer, e], wf_ref.at[s], sem.at[s])
                for w_ref, wf_ref in ((w1_ref, wf1_ref), (w3_ref, wf3_ref), (w2_ref, wf2_ref))]

    @pl.when(nb < meta_ref[0])
    def _():
        @pl.when(nb == 0)
        def _():
            slot_ref[0] = 0
            for cp in weight_copies(blke_ref[0], 0):
                cp.start()

        @pl.when((nb == 0) | (blke_ref[nb] != blke_ref[jnp.maximum(nb - 1, 0)]))
        def _():
            s = slot_ref[0]
            for cp in weight_copies(blke_ref[nb], s):
                cp.wait()
            nxt = nxte_ref[nb]

            @pl.when(nxt >= 0)
            def _():
                for cp in weight_copies(nxt, 1 - s):
                    cp.start()

            w1b_ref[...] = wf1_ref[s].astype(BF16)
            w3b_ref[...] = wf3_ref[s].astype(BF16)
            w2b_ref[...] = wf2_ref[s].astype(BF16)
            slot_ref[0] = 1 - s

        xb = _unpack_bf16_pairs(xs_ref[...])
        h1 = _dot(xb, w1b_ref[...])
        h3 = _dot(xb, w3b_ref[...])
        hh = (h1 * jax.nn.sigmoid(h1) * h3).astype(BF16)
        ys_ref[...] = _dot(hh, w2b_ref[...])

    @pl.when(nb >= meta_ref[0])
    def _():
        ys_ref[...] = jnp.zeros_like(ys_ref)


def _gmm(blke, nxte, meta, xs, w1, w3, w2, layer):
    nrows = xs.shape[0]
    d, de = w1.shape[-2:]
    nblk = nrows // MOE_BLOCK

    def row_map(i, blke, nxte, meta):
        return (jnp.minimum(i, meta[0] - 1), 0)

    kern = functools.partial(_gmm_kernel, layer=layer)
    return pl.pallas_call(
        kern,
        out_shape=jax.ShapeDtypeStruct((nrows, d), F32),
        grid_spec=pltpu.PrefetchScalarGridSpec(
            num_scalar_prefetch=3,
            grid=(nblk,),
            in_specs=[
                pl.BlockSpec((MOE_BLOCK, d // 2), row_map),
                pl.BlockSpec(memory_space=pl.ANY),
                pl.BlockSpec(memory_space=pl.ANY),
                pl.BlockSpec(memory_space=pl.ANY),
            ],
            out_specs=pl.BlockSpec((MOE_BLOCK, d), lambda i, blke, nxte, meta: (i, 0)),
            scratch_shapes=[pltpu.VMEM((2, d, de), F32), pltpu.VMEM((2, d, de), F32),
                            pltpu.VMEM((2, de, d), F32),
                            pltpu.VMEM((d, de), BF16), pltpu.VMEM((d, de), BF16),
                            pltpu.VMEM((de, d), BF16),
                            pltpu.SMEM((1,), I32), pltpu.SemaphoreType.DMA((2,))],
        ),
        compiler_params=_cparams(("arbitrary",), 32),
        name="moe_experts",
    )(blke, nxte, meta, xs, w1, w3, w2)


def _combine_kernel(dest_ref, ys_ref, x_ref, gate_ref, g_ref, b_ref, o_ref, buf_ref, sem,
                    *, tr, t, nsteps, alpha):
    i = pl.program_id(0)
    slot = i % 2

    def gather(step, s):
        def issue(r, c):
            for k in range(TOPK_IN_GROUP):
                _row_copy(ys_ref, dest_ref[k * t + step * tr + r], buf_ref.at[s], k * tr + r,
                          sem.at[s]).start()
            return c

        lax.fori_loop(0, tr, issue, 0, unroll=4)

    @pl.when(i == 0)
    def _():
        gather(0, 0)

    @pl.when(i + 1 < nsteps)
    def _():
        gather(i + 1, 1 - slot)

    _wait_rows(ys_ref, buf_ref.at[slot], sem.at[slot])
    gate = gate_ref[...]
    f = buf_ref[slot, 0:tr] * gate[:, 0:1] + buf_ref[slot, tr:2 * tr] * gate[:, 1:2]
    o_ref[...] = _layer_norm(alpha * x_ref[...] + f, g_ref[...], b_ref[...])


def _combine(dest_flat, ys, xt, gate_t, g, b, alpha):
    t, d = xt.shape
    tr = 256
    nsteps = t // tr
    kern = functools.partial(_combine_kernel, tr=tr, t=t, nsteps=nsteps, alpha=alpha)
    return pl.pallas_call(
        kern,
        out_shape=jax.ShapeDtypeStruct((t, d), F32),
        grid_spec=pltpu.PrefetchScalarGridSpec(
            num_scalar_prefetch=1,
            grid=(t // tr,),
            in_specs=[
                pl.BlockSpec(memory_space=pl.ANY),
                pl.BlockSpec((tr, d), lambda i, dest: (i, 0)),
                pl.BlockSpec((tr, TOPK_IN_GROUP), lambda i, dest: (i, 0)),
                pl.BlockSpec((1, d), lambda i, dest: (0, 0)),
                pl.BlockSpec((1, d), lambda i, dest: (0, 0)),
            ],
            out_specs=pl.BlockSpec((tr, d), lambda i, dest: (i, 0)),
            scratch_shapes=[pltpu.VMEM((2, TOPK_IN_GROUP * tr, d), F32),
                            pltpu.SemaphoreType.DMA((2,))],
        ),
        compiler_params=_cparams(("arbitrary",), 32),
        name="moe_combine_ln",
    )(dest_flat, ys, xt, gate_t, g.reshape(1, d), b.reshape(1, d))


def _moe_layer(xt, wg, bg, we, be, w1, w3, w2, layer, g, b, alpha):
    t, d = xt.shape
    nblk = -(-t * TOPK_IN_GROUP // MOE_BLOCK) + N_EXPERTS
    eid, gate, rank, cnt = _router(xt, wg, bg, we, be)
    dest, blke, nxte, meta = _dest(cnt[:, 0], eid, rank, nblk)
    dest_flat = dest.reshape(-1)
    xs = _scatter(dest_flat, xt, nblk * MOE_BLOCK)
    ys = _gmm(blke, nxte, meta, xs, w1, w3, w2, layer)
    return _combine(dest_flat, ys, xt, gate.T, g, b, alpha)


def kernel(x, conv_w_in, conv_k, conv_w_out, attn_w_in, kv_norm_g, kidx_ln_g, kidx_ln_b, attn_w_out, rel_bias, router_wg, router_bg, router_we, router_be, exp_w1, exp_w3, exp_w2, ln1_g, ln1_b, ln2_g, ln2_b):
    bsz, s, d = x.shape
    depth = ln1_g.shape[0]
    n_mixers = 2
    alpha = (2.0 * depth) ** 0.25
    bias_t = _bias_tiles(rel_bias)
    for i in range(depth):
        j = i // n_mixers
        if i % n_mixers == 0:
            x = _conv_layer(x, conv_w_in[j], conv_k[j], conv_w_out[j], ln1_g[i], ln1_b[i], alpha)
        else:
            x = _attn_layer(x, attn_w_in[j], kv_norm_g[j], kidx_ln_g[j], kidx_ln_b[j],
                            attn_w_out[j], bias_t, ln1_g[i], ln1_b[i], alpha)
        xt = _moe_layer(x.reshape(bsz * s, d), router_wg[i], router_bg[i], router_we[i],
                        router_be[i], exp_w1, exp_w3, exp_w2, i, ln2_g[i], ln2_b[i], alpha)
        x = xt.reshape(bsz, s, d)
    return x
```

```python
import functools
import math

import jax
import jax.numpy as jnp
from jax import lax
from jax.experimental import pallas as pl
from jax.experimental.pallas import tpu as pltpu

CONV_WIDTH = 3
N_HEADS = 16
KV_LATENT = 128
N_IDX_HEADS = 8
IDX_DIM = 64
TOPK_MAX = 256
Q_BLOCK = 128
N_BUCKETS = 32
MAX_DISTANCE = 128
N_GROUPS = 8
EXPERTS_PER_GROUP = 8
N_EXPERTS = N_GROUPS * EXPERTS_PER_GROUP
TOPK_IN_GROUP = 2
MOE_BLOCK = 256
LN_EPS = 1e-5
RMS_EPS = 1e-6

V7X_LANES = 128
V7X_SUBLANES = 8
V7X_VMEM_BYTES = 64 * 1024 * 1024

F32 = jnp.float32
BF16 = jnp.bfloat16
I32 = jnp.int32
U32 = jnp.uint32
NEG_INF = float("-inf")
INT32_MIN = -(2 ** 31)
LOG2E = math.log2(math.e)
LOGIT_SCALE2 = (KV_LATENT ** -0.5) * LOG2E
BIAS_ROWS = 4 * Q_BLOCK

_NT = (((1,), (1,)), ((), ()))


def _dot(a, b):
    return jnp.dot(a, b, preferred_element_type=F32)


def _dot_nt(a, b):
    return lax.dot_general(a, b, _NT, preferred_element_type=F32)


def _layer_norm(z, g, b):
    mu = jnp.mean(z, axis=-1, keepdims=True)
    zc = z - mu
    var = jnp.mean(zc * zc, axis=-1, keepdims=True)
    return zc * lax.rsqrt(var + LN_EPS) * g + b


def _cparams(semantics, vmem_mib):
    assert vmem_mib * 1024 * 1024 < V7X_VMEM_BYTES
    return pltpu.CompilerParams(dimension_semantics=semantics,
                                vmem_limit_bytes=vmem_mib * 1024 * 1024)


def _conv_kernel(x_ref, win_ref, ck_ref, wout_ref, g_ref, b_ref, o_ref, carry_ref, gbuf_ref,
                 *, ts, d, cw, alpha):
    @pl.when(pl.program_id(1) == 0)
    def _():
        carry_ref[...] = jnp.zeros_like(carry_ref)

    x = x_ref[...]
    xb = x.astype(BF16)
    row = lax.broadcasted_iota(I32, (ts, cw), 0)
    for c in range(d // cw):
        lo, hi = c * cw, (c + 1) * cw
        bg = _dot(xb, win_ref[:, lo:hi])
        cg = _dot(xb, win_ref[:, d + lo:d + hi])
        hh = _dot(xb, win_ref[:, 2 * d + lo:2 * d + hi])
        u = cg * hh
        prev = carry_ref[:, lo:hi]
        u1 = jnp.where(row == 0, prev[7:8], pltpu.roll(u, 1, 0))
        u2 = jnp.where(row == 0, prev[6:7], jnp.where(row == 1, prev[7:8], pltpu.roll(u, 2, 0)))
        k = ck_ref[:, lo:hi]
        conv = u2 * k[0:1] + u1 * k[1:2] + u * k[2:3]
        gbuf_ref[:, lo:hi] = (bg * conv).astype(BF16)
        carry_ref[:, lo:hi] = u[ts - V7X_SUBLANES:ts]
    y = _dot(gbuf_ref[...], wout_ref[...])
    o_ref[...] = _layer_norm(alpha * x + y, g_ref[...], b_ref[...])


def _conv_layer(x, w_in, conv_k, w_out, g, b, alpha):
    bsz, s, d = x.shape
    ts, cw = 512, 512
    kern = functools.partial(_conv_kernel, ts=ts, d=d, cw=cw, alpha=alpha)
    return pl.pallas_call(
        kern,
        out_shape=jax.ShapeDtypeStruct((bsz, s, d), F32),
        grid=(bsz, s // ts),
        in_specs=[
            pl.BlockSpec((None, ts, d), lambda i, j: (i, j, 0)),
            pl.BlockSpec((d, 3 * d), lambda i, j: (0, 0)),
            pl.BlockSpec((CONV_WIDTH, d), lambda i, j: (0, 0)),
            pl.BlockSpec((d, d), lambda i, j: (0, 0)),
            pl.BlockSpec((1, d), lambda i, j: (0, 0)),
            pl.BlockSpec((1, d), lambda i, j: (0, 0)),
        ],
        out_specs=pl.BlockSpec((None, ts, d), lambda i, j: (i, j, 0)),
        scratch_shapes=[pltpu.VMEM((V7X_SUBLANES, d), F32), pltpu.VMEM((ts, d), BF16)],
        compiler_params=_cparams(("arbitrary", "arbitrary"), 48),
        name="conv_mixer_ln",
    )(x, w_in.astype(BF16), conv_k, w_out.astype(BF16), g.reshape(1, d), b.reshape(1, d))


def _proj_kernel(x_ref, wq_ref, ws_ref, ww_ref, kvg_ref, lng_ref, lnb_ref,
                 q_ref, ckv_ref, ckvt_ref, qidx_ref, kidx_ref, widxt_ref, *, idx_scale):
    xb = x_ref[...].astype(BF16)
    q = _dot(xb, wq_ref[...]) * LOGIT_SCALE2
    for h in range(N_HEADS):
        q_ref[h] = q[:, h * KV_LATENT:(h + 1) * KV_LATENT].astype(BF16)
    sm = _dot(xb, ws_ref[...])
    ckv = sm[:, :KV_LATENT]
    ckv = ckv * lax.rsqrt(jnp.mean(ckv * ckv, axis=-1, keepdims=True) + RMS_EPS) * kvg_ref[...]
    ckv_ref[...] = ckv.astype(BF16)
    ckvt_ref[...] = ckv.T.astype(BF16)
    nq = N_IDX_HEADS * IDX_DIM
    qidx_ref[...] = sm[:, KV_LATENT:KV_LATENT + nq].astype(BF16)
    kidx = sm[:, KV_LATENT + nq:KV_LATENT + nq + IDX_DIM]
    kidx_ref[...] = _layer_norm(kidx, lng_ref[...], lnb_ref[...]).astype(BF16)
    widxt_ref[...] = _dot_nt(ww_ref[...], xb) * idx_scale


def _attn_proj(x, w_in, kv_g, ln_g, ln_b):
    bsz, s, d = x.shape
    ts = 512
    hq = N_HEADS * KV_LATENT
    nq = N_IDX_HEADS * IDX_DIM
    small = KV_LATENT + nq + IDX_DIM
    small_pad = -(-small // V7X_LANES) * V7X_LANES
    wq = w_in[:, :hq].astype(BF16)
    ws = jnp.pad(w_in[:, hq:hq + small], ((0, 0), (0, small_pad - small))).astype(BF16)
    ww = w_in[:, hq + small:].T.astype(BF16)
    idx_scale = (N_IDX_HEADS ** -0.5) * (IDX_DIM ** -0.5)
    kern = functools.partial(_proj_kernel, idx_scale=idx_scale)
    return pl.pallas_call(
        kern,
        out_shape=(
            jax.ShapeDtypeStruct((bsz, N_HEADS, s, KV_LATENT), BF16),
            jax.ShapeDtypeStruct((bsz, s, KV_LATENT), BF16),
            jax.ShapeDtypeStruct((bsz, KV_LATENT, s), BF16),
            jax.ShapeDtypeStruct((bsz, s, nq), BF16),
            jax.ShapeDtypeStruct((bsz, s, IDX_DIM), BF16),
            jax.ShapeDtypeStruct((bsz, N_IDX_HEADS, s), F32),
        ),
        grid=(bsz, s // ts),
        in_specs=[
            pl.BlockSpec((None, ts, d), lambda i, j: (i, j, 0)),
            pl.BlockSpec((d, hq), lambda i, j: (0, 0)),
            pl.BlockSpec((d, small_pad), lambda i, j: (0, 0)),
            pl.BlockSpec((N_IDX_HEADS, d), lambda i, j: (0, 0)),
            pl.BlockSpec((1, KV_LATENT), lambda i, j: (0, 0)),
            pl.BlockSpec((1, IDX_DIM), lambda i, j: (0, 0)),
            pl.BlockSpec((1, IDX_DIM), lambda i, j: (0, 0)),
        ],
        out_specs=(
            pl.BlockSpec((None, N_HEADS, ts, KV_LATENT), lambda i, j: (i, 0, j, 0)),
            pl.BlockSpec((None, ts, KV_LATENT), lambda i, j: (i, j, 0)),
            pl.BlockSpec((None, KV_LATENT, ts), lambda i, j: (i, 0, j)),
            pl.BlockSpec((None, ts, nq), lambda i, j: (i, j, 0)),
            pl.BlockSpec((None, ts, IDX_DIM), lambda i, j: (i, j, 0)),
            pl.BlockSpec((None, N_IDX_HEADS, ts), lambda i, j: (i, 0, j)),
        ),
        compiler_params=_cparams(("arbitrary", "arbitrary"), 48),
        name="attn_proj",
    )(x, wq, ws, ww, kv_g.reshape(1, -1), ln_g.reshape(1, -1), ln_b.reshape(1, -1))


def _bias_kernel(rb_ref, o_ref):
    rows = o_ref.shape[1]
    j = lax.broadcasted_iota(I32, (rows, Q_BLOCK), 0)
    r = lax.broadcasted_iota(I32, (rows, Q_BLOCK), 1)
    dist = 2 * Q_BLOCK + r - j
    dpos = jnp.maximum(dist, 0)
    max_exact = N_BUCKETS // 2
    d_f = jnp.maximum(dpos, 1).astype(F32)
    large = max_exact + (jnp.log(d_f / max_exact) / math.log(MAX_DISTANCE / max_exact)
                         * (N_BUCKETS - max_exact)).astype(I32)
    large = jnp.minimum(large, N_BUCKETS - 1)
    bucket = jnp.where(dpos < max_exact, dpos, large)
    for h in range(N_HEADS):
        acc = jnp.zeros((rows, Q_BLOCK), F32)
        for bk in range(N_BUCKETS):
            acc = jnp.where(bucket == bk, rb_ref[bk, h], acc)
        o_ref[h] = jnp.where(dist >= 0, (acc - rb_ref[N_BUCKETS - 1, h]) * LOG2E, 0.0)


def _bias_tiles(rel_bias):
    return pl.pallas_call(
        _bias_kernel,
        out_shape=jax.ShapeDtypeStruct((N_HEADS, BIAS_ROWS, Q_BLOCK), F32),
        in_specs=[pl.BlockSpec(memory_space=pltpu.SMEM)],
        out_specs=pl.BlockSpec(memory_space=pltpu.VMEM),
        name="rel_bias_tiles",
    )(rel_bias)


def _rows_reduce(parts, op):
    accs = [None, None]
    for c, p in enumerate(parts):
        accs[c % 2] = p if accs[c % 2] is None else op(accs[c % 2], p)
    return accs[0] if accs[1] is None else op(accs[0], accs[1])


def _attn_body(qi, q_ref, qidx_ref, widxt_ref, kidx_ref, ckv_ref, ckvt_ref, bias_ref, x_ref,
               wout_ref, g_ref, b_ref, o_ref, key_ref, madd_ref, lg_ref, eb_ref, obuf_ref,
               *, sk, first_bias_chunk, topk, alpha):
    ch = Q_BLOCK
    n = sk // ch
    t_abs = qi * Q_BLOCK + lax.broadcasted_iota(I32, (1, Q_BLOCK), 1)
    s_abs = lax.broadcasted_iota(I32, (sk, Q_BLOCK), 0)
    valid = s_abs <= t_abs

    kidx = kidx_ref[0:sk, :]
    score = jnp.zeros((sk, Q_BLOCK), F32)
    for h in range(N_IDX_HEADS):
        sh = _dot_nt(kidx, qidx_ref[:, h * IDX_DIM:(h + 1) * IDX_DIM])
        score = score + jnp.maximum(sh, 0.0) * widxt_ref[h:h + 1, :]
    score = jnp.where(score == 0.0, 0.0, score)
    score = jnp.where(valid, score, NEG_INF)
    bits = pltpu.bitcast(score, I32)
    key_ref[0:sk] = bits ^ ((bits >> 31) & 0x7FFFFFFF)

    def count(pred):
        parts = [jnp.where(pred(key_ref[c * ch:(c + 1) * ch]), 1.0, 0.0) for c in range(n)]
        return jnp.sum(_rows_reduce(parts, jnp.add), axis=0, keepdims=True)

    def bit_body(it, thr):
        cand = thr + lax.shift_left(jnp.int32(1), 31 - it)
        return jnp.where(count(lambda k: k >= cand) >= topk, cand, thr)

    thr = lax.fori_loop(0, 32, bit_body, jnp.full((1, Q_BLOCK), INT32_MIN, I32))
    cnt_ge = count(lambda k: k >= thr)
    madd_ref[0:sk] = jnp.where((key_ref[0:sk] >= thr) & valid, 0.0, NEG_INF)

    tied = jnp.where((cnt_ge > topk) & (t_abs >= topk - 1), 1.0, 0.0)

    @pl.when(jnp.max(tied) > 0.0)
    def _():
        chunk = ch
        need = topk - count(lambda k: k > thr)
        tri = jnp.where(lax.broadcasted_iota(I32, (chunk, chunk), 0)
                        >= lax.broadcasted_iota(I32, (chunk, chunk), 1), 1.0, 0.0).astype(BF16)
        run = jnp.zeros((1, Q_BLOCK), F32)
        for c in range(sk // chunk):
            kc = key_ref[c * chunk:(c + 1) * chunk]
            tie = kc == thr
            pre = _dot(tri, jnp.where(tie, 1.0, 0.0).astype(BF16)) + run
            run = pre[chunk - 1:chunk]
            vc = (c * chunk + lax.broadcasted_iota(I32, (chunk, Q_BLOCK), 0)) <= t_abs
            sel = ((kc > thr) | (tie & (pre <= need))) & vc
            madd_ref[c * chunk:(c + 1) * chunk] = jnp.where(sel, 0.0, NEG_INF)

    last_tile = BIAS_ROWS // ch - 1
    cr = 64
    nsteps = sk // cr
    half = sk // 2
    assert half % cr == 0
    npairs = N_HEADS // 2

    def stage_steps(logits, exps, m_exp, pv, sum_pv):
        p_logits, s_logits = logits or (None, None)
        p_exp, s_exp = exps or (None, None)
        p_pv, s_pv = pv or (None, None)
        lg_w = lg_r = eb_w = macc = sacc = logit = ot = None
        if p_pv is not None:
            eb_r = eb_ref.at[s_pv]
            ot = (_dot(ckvt_ref[:, 0:half], eb_r[0:half])
                  + _dot(ckvt_ref[:, half:sk], eb_r[half:sk]))
        if p_logits is not None:
            q2 = q_ref[pl.ds(2 * p_logits, 2)].reshape(2 * Q_BLOCK, KV_LATENT)
            logit = [_dot_nt(ckv_ref[0:half, :], q2), _dot_nt(ckv_ref[half:sk, :], q2)]
            lg_w = lg_ref.at[s_logits]
        if p_exp is not None:
            lg_r = lg_ref.at[s_exp]
            eb_w = eb_ref.at[s_exp]
        for c in range(nsteps):
            rows = slice(c * cr, (c + 1) * cr)
            if p_logits is not None:
                madd = madd_ref[rows]
                r0 = c * cr - (c * cr // half) * half
                v = logit[c * cr // half][r0:r0 + cr] + jnp.concatenate([madd, madd], axis=1)
                blk = (c * cr) // ch
                if blk >= first_bias_chunk:
                    off = jnp.clip(2 - qi + blk, 0, last_tile) * ch + (c * cr) % ch
                    off = pl.multiple_of(off, cr)
                    v = v + jnp.concatenate([bias_ref[2 * p_logits, pl.ds(off, cr), :],
                                             bias_ref[2 * p_logits + 1, pl.ds(off, cr), :]], axis=1)
                lg_w[rows] = v
                macc = v if macc is None else jnp.maximum(macc, v)
            if p_exp is not None:
                e = jnp.exp2(lg_r[rows] - m_exp)
                eb_w[rows] = e.astype(BF16)
                sacc = e if sacc is None else sacc + e
        if p_pv is not None:
            ot = ot / sum_pv
            obuf_ref[2 * p_pv] = ot[:, :Q_BLOCK].T.astype(BF16)
            obuf_ref[2 * p_pv + 1] = ot[:, Q_BLOCK:].T.astype(BF16)
        m_new = None if macc is None else jnp.max(macc, axis=0, keepdims=True)
        s_new = None if sacc is None else jnp.sum(sacc, axis=0, keepdims=True)
        return m_new, s_new

    assert npairs % 2 == 0 and npairs >= 4
    m0, _ = stage_steps((0, 0), None, None, None, None)
    m1, s0 = stage_steps((1, 1), (0, 0), m0, None, None)

    def pipe_body(j, carry):
        m_odd, s_even = carry
        m_even, s_odd = stage_steps((2 * j + 2, 0), (2 * j + 1, 1), m_odd, (2 * j, 0), s_even)
        return stage_steps((2 * j + 3, 1), (2 * j + 2, 0), m_even, (2 * j + 1, 1), s_odd)

    m_last, s_prev = lax.fori_loop(0, npairs // 2 - 1, pipe_body, (m1, s0))
    _, s_last = stage_steps(None, (npairs - 1, 1), m_last, (npairs - 2, 0), s_prev)
    stage_steps(None, None, None, (npairs - 1, 1), s_last)
    o_all = jnp.concatenate([obuf_ref[h] for h in range(N_HEADS)], axis=1)
    acc = _dot(o_all, wout_ref[...])
    o_ref[...] = _layer_norm(alpha * x_ref[...] + acc, g_ref[...], b_ref[...])


def _attn_kernel(*refs, nv, per, topk, alpha):
    qi = pl.program_id(1)
    for v in range(nv):
        body = functools.partial(_attn_body, qi, *refs, sk=(v + 1) * per * Q_BLOCK,
                                 first_bias_chunk=v * per - 1, topk=topk, alpha=alpha)
        pl.when(qi // per == v)(body)


def _attn_layer(x, w_in, kv_g, ln_g, ln_b, w_out, bias_t, g, b, alpha):
    bsz, s, d = x.shape
    nq = s // Q_BLOCK
    topk = min(TOPK_MAX, s // 4)
    per = -(-topk // Q_BLOCK)
    assert nq % per == 0
    nv = nq // per
    q, ckv, ckvt, qidx, kidx, widxt = _attn_proj(x, w_in, kv_g, ln_g, ln_b)
    kern = functools.partial(_attn_kernel, nv=nv, per=per, topk=topk, alpha=alpha)
    return pl.pallas_call(
        kern,
        out_shape=jax.ShapeDtypeStruct((bsz, s, d), F32),
        grid=(bsz, nq),
        in_specs=[
            pl.BlockSpec((None, N_HEADS, Q_BLOCK, KV_LATENT), lambda i, j: (i, 0, j, 0)),
            pl.BlockSpec((None, Q_BLOCK, N_IDX_HEADS * IDX_DIM), lambda i, j: (i, j, 0)),
            pl.BlockSpec((None, N_IDX_HEADS, Q_BLOCK), lambda i, j: (i, 0, j)),
            pl.BlockSpec((None, s, IDX_DIM), lambda i, j: (i, 0, 0)),
            pl.BlockSpec((None, s, KV_LATENT), lambda i, j: (i, 0, 0)),
            pl.BlockSpec((None, KV_LATENT, s), lambda i, j: (i, 0, 0)),
            pl.BlockSpec((N_HEADS, BIAS_ROWS, Q_BLOCK), lambda i, j: (0, 0, 0)),
            pl.BlockSpec((None, Q_BLOCK, d), lambda i, j: (i, j, 0)),
            pl.BlockSpec((N_HEADS * KV_LATENT, d), lambda i, j: (0, 0)),
            pl.BlockSpec((1, d), lambda i, j: (0, 0)),
            pl.BlockSpec((1, d), lambda i, j: (0, 0)),
        ],
        out_specs=pl.BlockSpec((None, Q_BLOCK, d), lambda i, j: (i, j, 0)),
        scratch_shapes=[
            pltpu.VMEM((s, Q_BLOCK), I32),
            pltpu.VMEM((s, Q_BLOCK), F32),
            pltpu.VMEM((2, s, 2 * Q_BLOCK), F32),
            pltpu.VMEM((2, s, 2 * Q_BLOCK), BF16),
            pltpu.VMEM((N_HEADS, Q_BLOCK, KV_LATENT), BF16),
        ],
        compiler_params=_cparams(("arbitrary", "arbitrary"), 48),
        name="dsa_attention_ln",
    )(q, qidx, widxt, kidx, ckv, ckvt, bias_t, x,
      w_out.astype(BF16), g.reshape(1, d), b.reshape(1, d))


def _split_bf16(a):
    hi = a.astype(BF16)
    lo = (a - hi.astype(F32)).astype(BF16)
    return hi, lo


def _router_kernel(x_ref, w_ref, b_ref, eid_ref, gate_ref, rank_ref, cnt_ref, base_ref, u_ref, *, tt):
    @pl.when(pl.program_id(0) == 0)
    def _():
        base_ref[...] = jnp.zeros_like(base_ref)
        u_ref[...] = jnp.where(lax.broadcasted_iota(I32, (tt, tt), 0)
                               < lax.broadcasted_iota(I32, (tt, tt), 1), 1.0, 0.0).astype(BF16)

    xh, xl = _split_bf16(x_ref[...])
    wh, wl = _split_bf16(w_ref[...])
    lt = _dot_nt(wh, xh) + (_dot_nt(wh, xl) + _dot_nt(wl, xh)) + b_ref[...]

    ng, ne = N_GROUPS, EXPERTS_PER_GROUP
    gl = lt[0:ng]
    iog = lax.broadcasted_iota(I32, (ng, tt), 0).astype(F32)
    gmax = jnp.max(gl, axis=0, keepdims=True)
    gidx = jnp.min(jnp.where(gl == gmax, iog, float(ng)), axis=0, keepdims=True)
    g_gate = 1.0 / jnp.sum(jnp.exp(gl - gmax), axis=0, keepdims=True)

    el = jnp.zeros((ne, tt), F32)
    for gi in range(ng):
        el = jnp.where(gidx == float(gi), lt[ng + gi * ne:ng + (gi + 1) * ne], el)
    ioe = lax.broadcasted_iota(I32, (ne, tt), 0).astype(F32)
    m1 = jnp.max(el, axis=0, keepdims=True)
    i1 = jnp.min(jnp.where(el == m1, ioe, float(ne)), axis=0, keepdims=True)
    el2 = jnp.where(ioe == i1, NEG_INF, el)
    m2 = jnp.max(el2, axis=0, keepdims=True)
    i2 = jnp.min(jnp.where(el2 == m2, ioe, float(ne)), axis=0, keepdims=True)
    ex = jnp.exp(m2 - m1)
    p1 = 1.0 / (1.0 + ex)
    gate_ref[0:1, :] = p1 * g_gate
    gate_ref[1:2, :] = ex * p1 * g_gate
    e1 = gidx * float(ne) + i1
    e2 = gidx * float(ne) + i2
    eid_ref[0:1, :] = e1.astype(I32)
    eid_ref[1:2, :] = e2.astype(I32)

    iox = lax.broadcasted_iota(I32, (N_EXPERTS, tt), 0).astype(F32)
    oh1 = jnp.where(iox == e1, 1.0, 0.0)
    oh2 = jnp.where(iox == e2, 1.0, 0.0)
    pre1 = _dot(oh1.astype(BF16), u_ref[...])
    pre2 = _dot(oh2.astype(BF16), u_ref[...])
    tot1 = jnp.sum(oh1, axis=1, keepdims=True)
    tot2 = jnp.sum(oh2, axis=1, keepdims=True)
    base = base_ref[...]
    rank_ref[0:1, :] = jnp.sum(oh1 * (base + pre1), axis=0, keepdims=True).astype(I32)
    rank_ref[1:2, :] = jnp.sum(oh2 * (base + tot1 + pre2), axis=0, keepdims=True).astype(I32)
    base = base + tot1 + tot2
    base_ref[...] = base
    cnt_ref[...] = jnp.broadcast_to(base, cnt_ref.shape).astype(I32)


def _router(xt, wg, bg, we, be):
    t, d = xt.shape
    tt = 512
    rows = V7X_LANES
    wcat = jnp.pad(jnp.concatenate([wg, we], axis=1).T, ((0, rows - N_GROUPS - N_EXPERTS), (0, 0)))
    bcat = jnp.pad(jnp.concatenate([bg, be]), (0, rows - N_GROUPS - N_EXPERTS)).reshape(rows, 1)
    kern = functools.partial(_router_kernel, tt=tt)
    return pl.pallas_call(
        kern,
        out_shape=(
            jax.ShapeDtypeStruct((TOPK_IN_GROUP, t), I32),
            jax.ShapeDtypeStruct((TOPK_IN_GROUP, t), F32),
            jax.ShapeDtypeStruct((TOPK_IN_GROUP, t), I32),
            jax.ShapeDtypeStruct((N_EXPERTS, V7X_LANES), I32),
        ),
        grid=(t // tt,),
        in_specs=[
            pl.BlockSpec((tt, d), lambda i: (i, 0)),
            pl.BlockSpec((rows, d), lambda i: (0, 0)),
            pl.BlockSpec((rows, 1), lambda i: (0, 0)),
        ],
        out_specs=(
            pl.BlockSpec((TOPK_IN_GROUP, tt), lambda i: (0, i)),
            pl.BlockSpec((TOPK_IN_GROUP, tt), lambda i: (0, i)),
            pl.BlockSpec((TOPK_IN_GROUP, tt), lambda i: (0, i)),
            pl.BlockSpec((N_EXPERTS, V7X_LANES), lambda i: (0, 0)),
        ),
        scratch_shapes=[pltpu.VMEM((N_EXPERTS, 1), F32), pltpu.VMEM((tt, tt), BF16)],
        compiler_params=_cparams(("arbitrary",), 32),
        name="moe_router",
    )(xt, wcat, bcat)


def _dest_kernel(cnt_ref, eid_ref, rank_ref, dest_ref, blke_ref, nxte_ref, meta_ref, pstart_ref, *, nblk):
    shift = MOE_BLOCK.bit_length() - 1

    def expert_body(e, acc):
        nb = (cnt_ref[e] + (MOE_BLOCK - 1)) >> shift
        pstart_ref[e] = acc
        b0 = acc >> shift

        def blk_body(j, c):
            blke_ref[b0 + j] = e
            return c

        lax.fori_loop(0, nb, blk_body, 0)
        return acc + (nb << shift)

    total = lax.fori_loop(0, N_EXPERTS, expert_body, jnp.int32(0))
    nused = total >> shift
    meta_ref[0] = nused
    last_e = blke_ref[nused - 1]

    def tail_body(j, c):
        blke_ref[j] = last_e
        nxte_ref[j] = -1
        return c

    lax.fori_loop(nused, nblk, tail_body, 0)

    def next_body(i, nxt):
        e = N_EXPERTS - 1 - i
        nb = (cnt_ref[e] + (MOE_BLOCK - 1)) >> shift
        b0 = pstart_ref[e] >> shift

        def blk_body(j, c):
            nxte_ref[b0 + j] = nxt
            return c

        lax.fori_loop(0, nb, blk_body, 0)
        return jnp.where(nb > 0, e, nxt)

    lax.fori_loop(0, N_EXPERTS, next_body, jnp.int32(-1))

    def dest_body(e, dest):
        return dest + jnp.where(eid_ref[...] == e, pstart_ref[e], 0)

    dest_ref[...] = lax.fori_loop(0, N_EXPERTS, dest_body, rank_ref[...])


def _dest(cnt, eid, rank, nblk):
    t = eid.shape[1]
    kern = functools.partial(_dest_kernel, nblk=nblk)
    return pl.pallas_call(
        kern,
        out_shape=(
            jax.ShapeDtypeStruct((TOPK_IN_GROUP, t), I32),
            jax.ShapeDtypeStruct((nblk,), I32),
            jax.ShapeDtypeStruct((nblk,), I32),
            jax.ShapeDtypeStruct((1,), I32),
        ),
        in_specs=[
            pl.BlockSpec(memory_space=pltpu.SMEM),
            pl.BlockSpec(memory_space=pltpu.VMEM),
            pl.BlockSpec(memory_space=pltpu.VMEM),
        ],
        out_specs=(
            pl.BlockSpec(memory_space=pltpu.VMEM),
            pl.BlockSpec(memory_space=pltpu.SMEM),
            pl.BlockSpec(memory_space=pltpu.SMEM),
            pl.BlockSpec(memory_space=pltpu.SMEM),
        ),
        scratch_shapes=[pltpu.SMEM((N_EXPERTS,), I32)],
        name="moe_dest",
    )(cnt, eid, rank)


def _row_copy(src_ref, src_row, dst_ref, dst_row, sem):
    return pltpu.make_async_copy(src_ref.at[pl.ds(src_row, 1)], dst_ref.at[pl.ds(dst_row, 1)], sem)


def _wait_rows(hbm_ref, vmem_rows_ref, sem):
    n = vmem_rows_ref.shape[0]
    pltpu.make_async_copy(hbm_ref.at[pl.ds(0, n)], vmem_rows_ref, sem).wait()


def _pack_bf16_pairs(x):
    half = x.shape[1] // 2
    hi = pltpu.bitcast(x[:, :half].astype(BF16).astype(F32), U32)
    lo = pltpu.bitcast(x[:, half:].astype(BF16).astype(F32), U32)
    return hi | (lo >> 16)


def _unpack_bf16_pairs(u):
    hi = pltpu.bitcast(u & jnp.uint32(0xFFFF0000), F32).astype(BF16)
    lo = pltpu.bitcast(u << 16, F32).astype(BF16)
    return jnp.concatenate([hi, lo], axis=1)


def _scatter_kernel(dest_ref, x_ref, xs_in_ref, xs_ref, stage_ref, sem, *, tr, t, nsteps):
    del xs_in_ref
    i = pl.program_id(0)
    slot = i % 2
    base = i * tr

    def drain(s):
        for _ in range(TOPK_IN_GROUP):
            _wait_rows(xs_ref, stage_ref.at[s], sem.at[s])

    @pl.when(i >= 2)
    def _():
        drain(slot)

    stage_ref[slot] = _pack_bf16_pairs(x_ref[...])

    def issue(r, c):
        for k in range(TOPK_IN_GROUP):
            _row_copy(stage_ref.at[slot], r, xs_ref, dest_ref[k * t + base + r], sem.at[slot]).start()
        return c

    lax.fori_loop(0, tr, issue, 0, unroll=4)

    @pl.when(i == nsteps - 1)
    def _():
        drain(slot)
        if nsteps > 1:
            drain(1 - slot)


def _scatter(dest_flat, xt, nrows):
    t, d = xt.shape
    tr = 256
    nsteps = t // tr
    kern = functools.partial(_scatter_kernel, tr=tr, t=t, nsteps=nsteps)
    return pl.pallas_call(
        kern,
        out_shape=jax.ShapeDtypeStruct((nrows, d // 2), U32),
        grid_spec=pltpu.PrefetchScalarGridSpec(
            num_scalar_prefetch=1,
            grid=(t // tr,),
            in_specs=[
                pl.BlockSpec((tr, d), lambda i, dest: (i, 0)),
                pl.BlockSpec(memory_space=pl.ANY),
            ],
            out_specs=pl.BlockSpec(memory_space=pl.ANY),
            scratch_shapes=[pltpu.VMEM((2, tr, d // 2), U32), pltpu.SemaphoreType.DMA((2,))],
        ),
        input_output_aliases={2: 0},
        compiler_params=_cparams(("arbitrary",), 32),
        name="moe_scatter_rows",
    )(dest_flat, xt, jnp.zeros((nrows, d // 2), U32))


def _gmm_kernel(blke_ref, nxte_ref, meta_ref, xs_ref, w1_ref, w3_ref, w2_ref, ys_ref,
                wf1_ref, wf3_ref, wf2_ref, w1b_ref, w3b_ref, w2b_ref, slot_ref, sem, *, layer):
    nb = pl.program_id(0)

    def weight_copies(e, s):
        return [pltpu.make_async_copy(w_ref.at[layer, e], wf_ref.at[s], sem.at[s])
                for w_ref, wf_ref in ((w1_ref, wf1_ref), (w3_ref, wf3_ref), (w2_ref, wf2_ref))]

    @pl.when(nb < meta_ref[0])
    def _():
        @pl.when(nb == 0)
        def _():
            slot_ref[0] = 0
            for cp in weight_copies(blke_ref[0], 0):
                cp.start()

        @pl.when((nb == 0) | (blke_ref[nb] != blke_ref[jnp.maximum(nb - 1, 0)]))
        def _():
            s = slot_ref[0]
            for cp in weight_copies(blke_ref[nb], s):
                cp.wait()
            nxt = nxte_ref[nb]

            @pl.when(nxt >= 0)
            def _():
                for cp in weight_copies(nxt, 1 - s):
                    cp.start()

            w1b_ref[...] = wf1_ref[s].astype(BF16)
            w3b_ref[...] = wf3_ref[s].astype(BF16)
            w2b_ref[...] = wf2_ref[s].astype(BF16)
            slot_ref[0] = 1 - s

        xb = _unpack_bf16_pairs(xs_ref[...])
        h1 = _dot(xb, w1b_ref[...])
        h3 = _dot(xb, w3b_ref[...])
        hh = (h1 * jax.nn.sigmoid(h1) * h3).astype(BF16)
        ys_ref[...] = _dot(hh, w2b_ref[...])

    @pl.when(nb >= meta_ref[0])
    def _():
        ys_ref[...] = jnp.zeros_like(ys_ref)


def _gmm(blke, nxte, meta, xs, w1, w3, w2, layer):
    nrows = xs.shape[0]
    d, de = w1.shape[-2:]
    nblk = nrows // MOE_BLOCK

    def row_map(i, blke, nxte, meta):
        return (jnp.minimum(i, meta[0] - 1), 0)

    kern = functools.partial(_gmm_kernel, layer=layer)
    return pl.pallas_call(
        kern,
        out_shape=jax.ShapeDtypeStruct((nrows, d), F32),
        grid_spec=pltpu.PrefetchScalarGridSpec(
            num_scalar_prefetch=3,
            grid=(nblk,),
            in_specs=[
                pl.BlockSpec((MOE_BLOCK, d // 2), row_map),
                pl.BlockSpec(memory_space=pl.ANY),
                pl.BlockSpec(memory_space=pl.ANY),
                pl.BlockSpec(memory_space=pl.ANY),
            ],
            out_specs=pl.BlockSpec((MOE_BLOCK, d), lambda i, blke, nxte, meta: (i, 0)),
            scratch_shapes=[pltpu.VMEM((2, d, de), F32), pltpu.VMEM((2, d, de), F32),
                            pltpu.VMEM((2, de, d), F32),
                            pltpu.VMEM((d, de), BF16), pltpu.VMEM((d, de), BF16),
                            pltpu.VMEM((de, d), BF16),
                            pltpu.SMEM((1,), I32), pltpu.SemaphoreType.DMA((2,))],
        ),
        compiler_params=_cparams(("arbitrary",), 32),
        name="moe_experts",
    )(blke, nxte, meta, xs, w1, w3, w2)


def _combine_kernel(dest_ref, ys_ref, x_ref, gate_ref, g_ref, b_ref, o_ref, buf_ref, sem,
                    *, tr, t, nsteps, alpha):
    i = pl.program_id(0)
    slot = i % 2

    def gather(step, s):
        def issue(r, c):
            for k in range(TOPK_IN_GROUP):
                _row_copy(ys_ref, dest_ref[k * t + step * tr + r], buf_ref.at[s], k * tr + r,
                          sem.at[s]).start()
            return c

        lax.fori_loop(0, tr, issue, 0, unroll=4)

    @pl.when(i == 0)
    def _():
        gather(0, 0)

    @pl.when(i + 1 < nsteps)
    def _():
        gather(i + 1, 1 - slot)

    _wait_rows(ys_ref, buf_ref.at[slot], sem.at[slot])
    gate = gate_ref[...]
    f = buf_ref[slot, 0:tr] * gate[:, 0:1] + buf_ref[slot, tr:2 * tr] * gate[:, 1:2]
    o_ref[...] = _layer_norm(alpha * x_ref[...] + f, g_ref[...], b_ref[...])


def _combine(dest_flat, ys, xt, gate_t, g, b, alpha):
    t, d = xt.shape
    tr = 256
    nsteps = t // tr
    kern = functools.partial(_combine_kernel, tr=tr, t=t, nsteps=nsteps, alpha=alpha)
    return pl.pallas_call(
        kern,
        out_shape=jax.ShapeDtypeStruct((t, d), F32),
        grid_spec=pltpu.PrefetchScalarGridSpec(
            num_scalar_prefetch=1,
            grid=(t // tr,),
            in_specs=[
                pl.BlockSpec(memory_space=pl.ANY),
                pl.BlockSpec((tr, d), lambda i, dest: (i, 0)),
                pl.BlockSpec((tr, TOPK_IN_GROUP), lambda i, dest: (i, 0)),
                pl.BlockSpec((1, d), lambda i, dest: (0, 0)),
                pl.BlockSpec((1, d), lambda i, dest: (0, 0)),
            ],
            out_specs=pl.BlockSpec((tr, d), lambda i, dest: (i, 0)),
            scratch_shapes=[pltpu.VMEM((2, TOPK_IN_GROUP * tr, d), F32),
                            pltpu.SemaphoreType.DMA((2,))],
        ),
        compiler_params=_cparams(("arbitrary",), 32),
        name="moe_combine_ln",
    )(dest_flat, ys, xt, gate_t, g.reshape(1, d), b.reshape(1, d))


def _moe_layer(xt, wg, bg, we, be, w1, w3, w2, layer, g, b, alpha):
    t, d = xt.shape
    nblk = -(-t * TOPK_IN_GROUP // MOE_BLOCK) + N_EXPERTS
    eid, gate, rank, cnt = _router(xt, wg, bg, we, be)
    dest, blke, nxte, meta = _dest(cnt[:, 0], eid, rank, nblk)
    dest_flat = dest.reshape(-1)
    xs = _scatter(dest_flat, xt, nblk * MOE_BLOCK)
    ys = _gmm(blke, nxte, meta, xs, w1, w3, w2, layer)
    return _combine(dest_flat, ys, xt, gate.T, g, b, alpha)


def kernel(x, conv_w_in, conv_k, conv_w_out, attn_w_in, kv_norm_g, kidx_ln_g, kidx_ln_b, attn_w_out, rel_bias, router_wg, router_bg, router_we, router_be, exp_w1, exp_w3, exp_w2, ln1_g, ln1_b, ln2_g, ln2_b):
    bsz, s, d = x.shape
    depth = ln1_g.shape[0]
    n_mixers = 2
    alpha = (2.0 * depth) ** 0.25
    bias_t = _bias_tiles(rel_bias)
    for i in range(depth):
        j = i // n_mixers
        if i % n_mixers == 0:
            x = _conv_layer(x, conv_w_in[j], conv_k[j], conv_w_out[j], ln1_g[i], ln1_b[i], alpha)
        else:
            x = _attn_layer(x, attn_w_in[j], kv_norm_g[j], kidx_ln_g[j], kidx_ln_b[j],
                            attn_w_out[j], bias_t, ln1_g[i], ln1_b[i], alpha)
        xt = _moe_layer(x.reshape(bsz * s, d), router_wg[i], router_bg[i], router_we[i],
                        router_be[i], exp_w1, exp_w3, exp_w2, i, ln2_g[i], ln2_b[i], alpha)
        x = xt.reshape(bsz, s, d)
    return x
```

```python
import functools
import math

import jax
import jax.numpy as jnp
from jax import lax
from jax.experimental import pallas as pl
from jax.experimental.pallas import tpu as pltpu

CONV_WIDTH = 3
N_HEADS = 16
KV_LATENT = 128
N_IDX_HEADS = 8
IDX_DIM = 64
TOPK_MAX = 256
Q_BLOCK = 128
N_BUCKETS = 32
MAX_DISTANCE = 128
N_GROUPS = 8
EXPERTS_PER_GROUP = 8
N_EXPERTS = N_GROUPS * EXPERTS_PER_GROUP
TOPK_IN_GROUP = 2
MOE_BLOCK = 256
MOE_TOKEN_TILE = 512
LN_EPS = 1e-5
RMS_EPS = 1e-6

V7X_LANES = 128
V7X_SUBLANES = 8
V7X_VMEM_BYTES = 64 * 1024 * 1024

F32 = jnp.float32
BF16 = jnp.bfloat16
I32 = jnp.int32
U32 = jnp.uint32
NEG_INF = float("-inf")
INT32_MIN = -(2 ** 31)
LOG2E = math.log2(math.e)
LOGIT_SCALE2 = (KV_LATENT ** -0.5) * LOG2E
BIAS_ROWS = 4 * Q_BLOCK

_NT = (((1,), (1,)), ((), ()))


def _dot(a, b):
    return jnp.dot(a, b, preferred_element_type=F32)


def _dot_nt(a, b):
    return lax.dot_general(a, b, _NT, preferred_element_type=F32)


def _layer_norm(z, g, b):
    mu = jnp.mean(z, axis=-1, keepdims=True)
    zc = z - mu
    var = jnp.mean(zc * zc, axis=-1, keepdims=True)
    return zc * lax.rsqrt(var + LN_EPS) * g + b


def _cparams(semantics, vmem_mib):
    assert vmem_mib * 1024 * 1024 < V7X_VMEM_BYTES
    return pltpu.CompilerParams(dimension_semantics=semantics,
                                vmem_limit_bytes=vmem_mib * 1024 * 1024)


def _conv_kernel(x_ref, win_ref, ck_ref, wout_ref, g_ref, b_ref, o_ref, carry_ref, gbuf_ref,
                 *, ts, d, cw, alpha):
    @pl.when(pl.program_id(1) == 0)
    def _():
        carry_ref[...] = jnp.zeros_like(carry_ref)

    x = x_ref[...]
    xb = x.astype(BF16)
    row = lax.broadcasted_iota(I32, (ts, cw), 0)
    for c in range(d // cw):
        lo, hi = c * cw, (c + 1) * cw
        bg = _dot(xb, win_ref[:, lo:hi])
        cg = _dot(xb, win_ref[:, d + lo:d + hi])
        hh = _dot(xb, win_ref[:, 2 * d + lo:2 * d + hi])
        u = cg * hh
        prev = carry_ref[:, lo:hi]
        u1 = jnp.where(row == 0, prev[7:8], pltpu.roll(u, 1, 0))
        u2 = jnp.where(row == 0, prev[6:7], jnp.where(row == 1, prev[7:8], pltpu.roll(u, 2, 0)))
        k = ck_ref[:, lo:hi]
        conv = u2 * k[0:1] + u1 * k[1:2] + u * k[2:3]
        gbuf_ref[:, lo:hi] = (bg * conv).astype(BF16)
        carry_ref[:, lo:hi] = u[ts - V7X_SUBLANES:ts]
    y = _dot(gbuf_ref[...], wout_ref[...])
    o_ref[...] = _layer_norm(alpha * x + y, g_ref[...], b_ref[...])


def _conv_layer(x, w_in, conv_k, w_out, g, b, alpha):
    bsz, s, d = x.shape
    ts, cw = 512, 512
    kern = functools.partial(_conv_kernel, ts=ts, d=d, cw=cw, alpha=alpha)
    return pl.pallas_call(
        kern,
        out_shape=jax.ShapeDtypeStruct((bsz, s, d), F32),
        grid=(bsz, s // ts),
        in_specs=[
            pl.BlockSpec((None, ts, d), lambda i, j: (i, j, 0)),
            pl.BlockSpec((d, 3 * d), lambda i, j: (0, 0)),
            pl.BlockSpec((CONV_WIDTH, d), lambda i, j: (0, 0)),
            pl.BlockSpec((d, d), lambda i, j: (0, 0)),
            pl.BlockSpec((1, d), lambda i, j: (0, 0)),
            pl.BlockSpec((1, d), lambda i, j: (0, 0)),
        ],
        out_specs=pl.BlockSpec((None, ts, d), lambda i, j: (i, j, 0)),
        scratch_shapes=[pltpu.VMEM((V7X_SUBLANES, d), F32), pltpu.VMEM((ts, d), BF16)],
        compiler_params=_cparams(("arbitrary", "arbitrary"), 48),
        name="conv_mixer_ln",
    )(x, w_in.astype(BF16), conv_k, w_out.astype(BF16), g.reshape(1, d), b.reshape(1, d))


def _proj_kernel(x_ref, wq_ref, ws_ref, ww_ref, kvg_ref, lng_ref, lnb_ref,
                 q_ref, ckv_ref, ckvt_ref, qidx_ref, kidx_ref, widxt_ref, *, idx_scale):
    xb = x_ref[...].astype(BF16)
    q = _dot(xb, wq_ref[...]) * LOGIT_SCALE2
    for h in range(N_HEADS):
        q_ref[h] = q[:, h * KV_LATENT:(h + 1) * KV_LATENT].astype(BF16)
    sm = _dot(xb, ws_ref[...])
    ckv = sm[:, :KV_LATENT]
    ckv = ckv * lax.rsqrt(jnp.mean(ckv * ckv, axis=-1, keepdims=True) + RMS_EPS) * kvg_ref[...]
    ckv_ref[...] = ckv.astype(BF16)
    ckvt_ref[...] = ckv.T.astype(BF16)
    nq = N_IDX_HEADS * IDX_DIM
    qidx_ref[...] = sm[:, KV_LATENT:KV_LATENT + nq].astype(BF16)
    kidx = sm[:, KV_LATENT + nq:KV_LATENT + nq + IDX_DIM]
    kidx_ref[...] = _layer_norm(kidx, lng_ref[...], lnb_ref[...]).astype(BF16)
    widxt_ref[...] = _dot_nt(ww_ref[...], xb) * idx_scale


def _attn_proj(x, w_in, kv_g, ln_g, ln_b):
    bsz, s, d = x.shape
    ts = 512
    hq = N_HEADS * KV_LATENT
    nq = N_IDX_HEADS * IDX_DIM
    small = KV_LATENT + nq + IDX_DIM
    small_pad = -(-small // V7X_LANES) * V7X_LANES
    wq = w_in[:, :hq].astype(BF16)
    ws = jnp.pad(w_in[:, hq:hq + small], ((0, 0), (0, small_pad - small))).astype(BF16)
    ww = w_in[:, hq + small:].T.astype(BF16)
    idx_scale = (N_IDX_HEADS ** -0.5) * (IDX_DIM ** -0.5)
    kern = functools.partial(_proj_kernel, idx_scale=idx_scale)
    return pl.pallas_call(
        kern,
        out_shape=(
            jax.ShapeDtypeStruct((bsz, N_HEADS, s, KV_LATENT), BF16),
            jax.ShapeDtypeStruct((bsz, s, KV_LATENT), BF16),
            jax.ShapeDtypeStruct((bsz, KV_LATENT, s), BF16),
            jax.ShapeDtypeStruct((bsz, s, nq), BF16),
            jax.ShapeDtypeStruct((bsz, s, IDX_DIM), BF16),
            jax.ShapeDtypeStruct((bsz, N_IDX_HEADS, s), F32),
        ),
        grid=(bsz, s // ts),
        in_specs=[
            pl.BlockSpec((None, ts, d), lambda i, j: (i, j, 0)),
            pl.BlockSpec((d, hq), lambda i, j: (0, 0)),
            pl.BlockSpec((d, small_pad), lambda i, j: (0, 0)),
            pl.BlockSpec((N_IDX_HEADS, d), lambda i, j: (0, 0)),
            pl.BlockSpec((1, KV_LATENT), lambda i, j: (0, 0)),
            pl.BlockSpec((1, IDX_DIM), lambda i, j: (0, 0)),
            pl.BlockSpec((1, IDX_DIM), lambda i, j: (0, 0)),
        ],
        out_specs=(
            pl.BlockSpec((None, N_HEADS, ts, KV_LATENT), lambda i, j: (i, 0, j, 0)),
            pl.BlockSpec((None, ts, KV_LATENT), lambda i, j: (i, j, 0)),
            pl.BlockSpec((None, KV_LATENT, ts), lambda i, j: (i, 0, j)),
            pl.BlockSpec((None, ts, nq), lambda i, j: (i, j, 0)),
            pl.BlockSpec((None, ts, IDX_DIM), lambda i, j: (i, j, 0)),
            pl.BlockSpec((None, N_IDX_HEADS, ts), lambda i, j: (i, 0, j)),
        ),
        compiler_params=_cparams(("arbitrary", "arbitrary"), 48),
        name="attn_proj",
    )(x, wq, ws, ww, kv_g.reshape(1, -1), ln_g.reshape(1, -1), ln_b.reshape(1, -1))


def _bias_kernel(rb_ref, o_ref):
    rows = o_ref.shape[1]
    j = lax.broadcasted_iota(I32, (rows, Q_BLOCK), 0)
    r = lax.broadcasted_iota(I32, (rows, Q_BLOCK), 1)
    dist = 2 * Q_BLOCK + r - j
    dpos = jnp.maximum(dist, 0)
    max_exact = N_BUCKETS // 2
    d_f = jnp.maximum(dpos, 1).astype(F32)
    large = max_exact + (jnp.log(d_f / max_exact) / math.log(MAX_DISTANCE / max_exact)
                         * (N_BUCKETS - max_exact)).astype(I32)
    large = jnp.minimum(large, N_BUCKETS - 1)
    bucket = jnp.where(dpos < max_exact, dpos, large)
    for h in range(N_HEADS):
        acc = jnp.zeros((rows, Q_BLOCK), F32)
        for bk in range(N_BUCKETS):
            acc = jnp.where(bucket == bk, rb_ref[bk, h], acc)
        o_ref[h] = jnp.where(dist >= 0, (acc - rb_ref[N_BUCKETS - 1, h]) * LOG2E, 0.0)


def _bias_tiles(rel_bias):
    return pl.pallas_call(
        _bias_kernel,
        out_shape=jax.ShapeDtypeStruct((N_HEADS, BIAS_ROWS, Q_BLOCK), F32),
        in_specs=[pl.BlockSpec(memory_space=pltpu.SMEM)],
        out_specs=pl.BlockSpec(memory_space=pltpu.VMEM),
        name="rel_bias_tiles",
    )(rel_bias)


def _rows_reduce(parts, op):
    accs = [None, None]
    for c, p in enumerate(parts):
        accs[c % 2] = p if accs[c % 2] is None else op(accs[c % 2], p)
    return accs[0] if accs[1] is None else op(accs[0], accs[1])


def _attn_body(qi, q_ref, qidx_ref, widxt_ref, kidx_ref, ckv_ref, ckvt_ref, bias_ref, x_ref,
               wout_ref, g_ref, b_ref, o_ref, key_ref, madd_ref, lg_ref, eb_ref, obuf_ref,
               *, sk, first_bias_chunk, topk, alpha):
    ch = Q_BLOCK
    n = sk // ch
    t_abs = qi * Q_BLOCK + lax.broadcasted_iota(I32, (1, Q_BLOCK), 1)
    s_abs = lax.broadcasted_iota(I32, (sk, Q_BLOCK), 0)
    valid = s_abs <= t_abs

    kidx = kidx_ref[0:sk, :]
    score = jnp.zeros((sk, Q_BLOCK), F32)
    for h in range(N_IDX_HEADS):
        sh = _dot_nt(kidx, qidx_ref[:, h * IDX_DIM:(h + 1) * IDX_DIM])
        score = score + jnp.maximum(sh, 0.0) * widxt_ref[h:h + 1, :]
    score = jnp.where(score == 0.0, 0.0, score)
    score = jnp.where(valid, score, NEG_INF)
    bits = pltpu.bitcast(score, I32)
    key_ref[0:sk] = bits ^ ((bits >> 31) & 0x7FFFFFFF)

    def count(pred):
        parts = [jnp.where(pred(key_ref[c * ch:(c + 1) * ch]), 1.0, 0.0) for c in range(n)]
        return jnp.sum(_rows_reduce(parts, jnp.add), axis=0, keepdims=True)

    def bit_body(it, thr):
        cand = thr + lax.shift_left(jnp.int32(1), 31 - it)
        return jnp.where(count(lambda k: k >= cand) >= topk, cand, thr)

    thr = lax.fori_loop(0, 32, bit_body, jnp.full((1, Q_BLOCK), INT32_MIN, I32))
    cnt_ge = count(lambda k: k >= thr)
    madd_ref[0:sk] = jnp.where((key_ref[0:sk] >= thr) & valid, 0.0, NEG_INF)

    tied = jnp.where((cnt_ge > topk) & (t_abs >= topk - 1), 1.0, 0.0)

    @pl.when(jnp.max(tied) > 0.0)
    def _():
        chunk = ch
        need = topk - count(lambda k: k > thr)
        tri = jnp.where(lax.broadcasted_iota(I32, (chunk, chunk), 0)
                        >= lax.broadcasted_iota(I32, (chunk, chunk), 1), 1.0, 0.0).astype(BF16)
        run = jnp.zeros((1, Q_BLOCK), F32)
        for c in range(sk // chunk):
            kc = key_ref[c * chunk:(c + 1) * chunk]
            tie = kc == thr
            pre = _dot(tri, jnp.where(tie, 1.0, 0.0).astype(BF16)) + run
            run = pre[chunk - 1:chunk]
            vc = (c * chunk + lax.broadcasted_iota(I32, (chunk, Q_BLOCK), 0)) <= t_abs
            sel = ((kc > thr) | (tie & (pre <= need))) & vc
            madd_ref[c * chunk:(c + 1) * chunk] = jnp.where(sel, 0.0, NEG_INF)

    last_tile = BIAS_ROWS // ch - 1
    cr = 64
    nsteps = sk // cr
    half = sk // 2
    assert half % cr == 0
    npairs = N_HEADS // 2

    def stage_steps(logits, exps, m_exp, pv, sum_pv):
        p_logits, s_logits = logits or (None, None)
        p_exp, s_exp = exps or (None, None)
        p_pv, s_pv = pv or (None, None)
        lg_w = lg_r = eb_w = macc = sacc = logit = ot = None
        if p_pv is not None:
            eb_r = eb_ref.at[s_pv]
            ot = (_dot(ckvt_ref[:, 0:half], eb_r[0:half])
                  + _dot(ckvt_ref[:, half:sk], eb_r[half:sk]))
        if p_logits is not None:
            q2 = q_ref[pl.ds(2 * p_logits, 2)].reshape(2 * Q_BLOCK, KV_LATENT)
            logit = [_dot_nt(ckv_ref[0:half, :], q2), _dot_nt(ckv_ref[half:sk, :], q2)]
            lg_w = lg_ref.at[s_logits]
        if p_exp is not None:
            lg_r = lg_ref.at[s_exp]
            eb_w = eb_ref.at[s_exp]
        for c in range(nsteps):
            rows = slice(c * cr, (c + 1) * cr)
            if p_logits is not None:
                madd = madd_ref[rows]
                r0 = c * cr - (c * cr // half) * half
                v = logit[c * cr // half][r0:r0 + cr] + jnp.concatenate([madd, madd], axis=1)
                blk = (c * cr) // ch
                if blk >= first_bias_chunk:
                    off = jnp.clip(2 - qi + blk, 0, last_tile) * ch + (c * cr) % ch
                    off = pl.multiple_of(off, cr)
                    v = v + jnp.concatenate([bias_ref[2 * p_logits, pl.ds(off, cr), :],
                                             bias_ref[2 * p_logits + 1, pl.ds(off, cr), :]], axis=1)
                lg_w[rows] = v
                macc = v if macc is None else jnp.maximum(macc, v)
            if p_exp is not None:
                e = jnp.exp2(lg_r[rows] - m_exp)
                eb_w[rows] = e.astype(BF16)
                sacc = e if sacc is None else sacc + e
        if p_pv is not None:
            ot = ot / sum_pv
            obuf_ref[2 * p_pv] = ot[:, :Q_BLOCK].T.astype(BF16)
            obuf_ref[2 * p_pv + 1] = ot[:, Q_BLOCK:].T.astype(BF16)
        m_new = None if macc is None else jnp.max(macc, axis=0, keepdims=True)
        s_new = None if sacc is None else jnp.sum(sacc, axis=0, keepdims=True)
        return m_new, s_new

    assert npairs % 2 == 0 and npairs >= 4
    m0, _ = stage_steps((0, 0), None, None, None, None)
    m1, s0 = stage_steps((1, 1), (0, 0), m0, None, None)

    def pipe_body(j, carry):
        m_odd, s_even = carry
        m_even, s_odd = stage_steps((2 * j + 2, 0), (2 * j + 1, 1), m_odd, (2 * j, 0), s_even)
        return stage_steps((2 * j + 3, 1), (2 * j + 2, 0), m_even, (2 * j + 1, 1), s_odd)

    m_last, s_prev = lax.fori_loop(0, npairs // 2 - 1, pipe_body, (m1, s0))
    _, s_last = stage_steps(None, (npairs - 1, 1), m_last, (npairs - 2, 0), s_prev)
    stage_steps(None, None, None, (npairs - 1, 1), s_last)
    o_all = jnp.concatenate([obuf_ref[h] for h in range(N_HEADS)], axis=1)
    acc = _dot(o_all, wout_ref[...])
    o_ref[...] = _layer_norm(alpha * x_ref[...] + acc, g_ref[...], b_ref[...])


def _attn_kernel(*refs, nv, per, topk, alpha):
    qi = pl.program_id(1)
    for v in range(nv):
        body = functools.partial(_attn_body, qi, *refs, sk=(v + 1) * per * Q_BLOCK,
                                 first_bias_chunk=v * per - 1, topk=topk, alpha=alpha)
        pl.when(qi // per == v)(body)


def _attn_layer(x, w_in, kv_g, ln_g, ln_b, w_out, bias_t, g, b, alpha):
    bsz, s, d = x.shape
    nq = s // Q_BLOCK
    topk = min(TOPK_MAX, s // 4)
    per = -(-topk // Q_BLOCK)
    assert nq % per == 0
    nv = nq // per
    q, ckv, ckvt, qidx, kidx, widxt = _attn_proj(x, w_in, kv_g, ln_g, ln_b)
    kern = functools.partial(_attn_kernel, nv=nv, per=per, topk=topk, alpha=alpha)
    return pl.pallas_call(
        kern,
        out_shape=jax.ShapeDtypeStruct((bsz, s, d), F32),
        grid=(bsz, nq),
        in_specs=[
            pl.BlockSpec((None, N_HEADS, Q_BLOCK, KV_LATENT), lambda i, j: (i, 0, j, 0)),
            pl.BlockSpec((None, Q_BLOCK, N_IDX_HEADS * IDX_DIM), lambda i, j: (i, j, 0)),
            pl.BlockSpec((None, N_IDX_HEADS, Q_BLOCK), lambda i, j: (i, 0, j)),
            pl.BlockSpec((None, s, IDX_DIM), lambda i, j: (i, 0, 0)),
            pl.BlockSpec((None, s, KV_LATENT), lambda i, j: (i, 0, 0)),
            pl.BlockSpec((None, KV_LATENT, s), lambda i, j: (i, 0, 0)),
            pl.BlockSpec((N_HEADS, BIAS_ROWS, Q_BLOCK), lambda i, j: (0, 0, 0)),
            pl.BlockSpec((None, Q_BLOCK, d), lambda i, j: (i, j, 0)),
            pl.BlockSpec((N_HEADS * KV_LATENT, d), lambda i, j: (0, 0)),
            pl.BlockSpec((1, d), lambda i, j: (0, 0)),
            pl.BlockSpec((1, d), lambda i, j: (0, 0)),
        ],
        out_specs=pl.BlockSpec((None, Q_BLOCK, d), lambda i, j: (i, j, 0)),
        scratch_shapes=[
            pltpu.VMEM((s, Q_BLOCK), I32),
            pltpu.VMEM((s, Q_BLOCK), F32),
            pltpu.VMEM((2, s, 2 * Q_BLOCK), F32),
            pltpu.VMEM((2, s, 2 * Q_BLOCK), BF16),
            pltpu.VMEM((N_HEADS, Q_BLOCK, KV_LATENT), BF16),
        ],
        compiler_params=_cparams(("arbitrary", "arbitrary"), 48),
        name="dsa_attention_ln",
    )(q, qidx, widxt, kidx, ckv, ckvt, bias_t, x,
      w_out.astype(BF16), g.reshape(1, d), b.reshape(1, d))


def _split_bf16(a):
    hi = a.astype(BF16)
    lo = (a - hi.astype(F32)).astype(BF16)
    return hi, lo


def _router_kernel(x_ref, w_ref, b_ref, eid_ref, gate_ref, rank_ref, cnt_ref, base_ref, u_ref, *, tt):
    @pl.when(pl.program_id(0) == 0)
    def _():
        base_ref[...] = jnp.zeros_like(base_ref)
        u_ref[...] = jnp.where(lax.broadcasted_iota(I32, (tt, tt), 0)
                               < lax.broadcasted_iota(I32, (tt, tt), 1), 1.0, 0.0).astype(BF16)

    xh, xl = _split_bf16(x_ref[...])
    wh, wl = _split_bf16(w_ref[...])
    lt = _dot_nt(wh, xh) + (_dot_nt(wh, xl) + _dot_nt(wl, xh)) + b_ref[...]

    ng, ne = N_GROUPS, EXPERTS_PER_GROUP
    gl = lt[0:ng]
    iog = lax.broadcasted_iota(I32, (ng, tt), 0).astype(F32)
    gmax = jnp.max(gl, axis=0, keepdims=True)
    gidx = jnp.min(jnp.where(gl == gmax, iog, float(ng)), axis=0, keepdims=True)
    g_gate = 1.0 / jnp.sum(jnp.exp(gl - gmax), axis=0, keepdims=True)

    el = jnp.zeros((ne, tt), F32)
    for gi in range(ng):
        el = jnp.where(gidx == float(gi), lt[ng + gi * ne:ng + (gi + 1) * ne], el)
    ioe = lax.broadcasted_iota(I32, (ne, tt), 0).astype(F32)
    m1 = jnp.max(el, axis=0, keepdims=True)
    i1 = jnp.min(jnp.where(el == m1, ioe, float(ne)), axis=0, keepdims=True)
    el2 = jnp.where(ioe == i1, NEG_INF, el)
    m2 = jnp.max(el2, axis=0, keepdims=True)
    i2 = jnp.min(jnp.where(el2 == m2, ioe, float(ne)), axis=0, keepdims=True)
    ex = jnp.exp(m2 - m1)
    p1 = 1.0 / (1.0 + ex)
    gate_ref[0:1, :] = p1 * g_gate
    gate_ref[1:2, :] = ex * p1 * g_gate
    e1 = gidx * float(ne) + i1
    e2 = gidx * float(ne) + i2
    eid_ref[0:1, :] = e1.astype(I32)
    eid_ref[1:2, :] = e2.astype(I32)

    iox = lax.broadcasted_iota(I32, (N_EXPERTS, tt), 0).astype(F32)
    oh1 = jnp.where(iox == e1, 1.0, 0.0)
    oh2 = jnp.where(iox == e2, 1.0, 0.0)
    pre1 = _dot(oh1.astype(BF16), u_ref[...])
    pre2 = _dot(oh2.astype(BF16), u_ref[...])
    tot1 = jnp.sum(oh1, axis=1, keepdims=True)
    tot2 = jnp.sum(oh2, axis=1, keepdims=True)
    base = base_ref[...]
    rank_ref[0:1, :] = jnp.sum(oh1 * (base + pre1), axis=0, keepdims=True).astype(I32)
    rank_ref[1:2, :] = jnp.sum(oh2 * (base + tot1 + pre2), axis=0, keepdims=True).astype(I32)
    base = base + tot1 + tot2
    base_ref[...] = base
    cnt_ref[...] = jnp.broadcast_to(base, cnt_ref.shape).astype(I32)


def _router(xt, wg, bg, we, be):
    t, d = xt.shape
    tt = 512
    rows = V7X_LANES
    wcat = jnp.pad(jnp.concatenate([wg, we], axis=1).T, ((0, rows - N_GROUPS - N_EXPERTS), (0, 0)))
    bcat = jnp.pad(jnp.concatenate([bg, be]), (0, rows - N_GROUPS - N_EXPERTS)).reshape(rows, 1)
    kern = functools.partial(_router_kernel, tt=tt)
    return pl.pallas_call(
        kern,
        out_shape=(
            jax.ShapeDtypeStruct((TOPK_IN_GROUP, t), I32),
            jax.ShapeDtypeStruct((TOPK_IN_GROUP, t), F32),
            jax.ShapeDtypeStruct((TOPK_IN_GROUP, t), I32),
            jax.ShapeDtypeStruct((N_EXPERTS, V7X_LANES), I32),
        ),
        grid=(t // tt,),
        in_specs=[
            pl.BlockSpec((tt, d), lambda i: (i, 0)),
            pl.BlockSpec((rows, d), lambda i: (0, 0)),
            pl.BlockSpec((rows, 1), lambda i: (0, 0)),
        ],
        out_specs=(
            pl.BlockSpec((TOPK_IN_GROUP, tt), lambda i: (0, i)),
            pl.BlockSpec((TOPK_IN_GROUP, tt), lambda i: (0, i)),
            pl.BlockSpec((TOPK_IN_GROUP, tt), lambda i: (0, i)),
            pl.BlockSpec((N_EXPERTS, V7X_LANES), lambda i: (0, 0)),
        ),
        scratch_shapes=[pltpu.VMEM((N_EXPERTS, 1), F32), pltpu.VMEM((tt, tt), BF16)],
        compiler_params=_cparams(("arbitrary",), 32),
        name="moe_router",
    )(xt, wcat, bcat)


def _dest_kernel(cnt_ref, eid_ref, rank_ref, dest_ref, blke_ref, nxte_ref, meta_ref, pstart_ref, *, nblk):
    shift = MOE_BLOCK.bit_length() - 1

    def expert_body(e, acc):
        nb = (cnt_ref[e] + (MOE_BLOCK - 1)) >> shift
        pstart_ref[e] = acc
        b0 = acc >> shift

        def blk_body(j, c):
            blke_ref[b0 + j] = e
            return c

        lax.fori_loop(0, nb, blk_body, 0)
        return acc + (nb << shift)

    total = lax.fori_loop(0, N_EXPERTS, expert_body, jnp.int32(0))
    nused = total >> shift
    meta_ref[0] = nused
    last_e = blke_ref[nused - 1]

    def tail_body(j, c):
        blke_ref[j] = last_e
        nxte_ref[j] = -1
        return c

    lax.fori_loop(nused, nblk, tail_body, 0)

    def next_body(i, nxt):
        e = N_EXPERTS - 1 - i
        nb = (cnt_ref[e] + (MOE_BLOCK - 1)) >> shift
        b0 = pstart_ref[e] >> shift

        def blk_body(j, c):
            nxte_ref[b0 + j] = nxt
            return c

        lax.fori_loop(0, nb, blk_body, 0)
        return jnp.where(nb > 0, e, nxt)

    lax.fori_loop(0, N_EXPERTS, next_body, jnp.int32(-1))

    def dest_body(e, dest):
        return dest + jnp.where(eid_ref[...] == e, pstart_ref[e], 0)

    dest_ref[...] = lax.fori_loop(0, N_EXPERTS, dest_body, rank_ref[...])


def _dest(cnt, eid, rank, nblk):
    t = eid.shape[1]
    kern = functools.partial(_dest_kernel, nblk=nblk)
    return pl.pallas_call(
        kern,
        out_shape=(
            jax.ShapeDtypeStruct((TOPK_IN_GROUP, t), I32),
            jax.ShapeDtypeStruct((nblk,), I32),
            jax.ShapeDtypeStruct((nblk,), I32),
            jax.ShapeDtypeStruct((1,), I32),
        ),
        in_specs=[
            pl.BlockSpec(memory_space=pltpu.SMEM),
            pl.BlockSpec(memory_space=pltpu.VMEM),
            pl.BlockSpec(memory_space=pltpu.VMEM),
        ],
        out_specs=(
            pl.BlockSpec(memory_space=pltpu.VMEM),
            pl.BlockSpec(memory_space=pltpu.SMEM),
            pl.BlockSpec(memory_space=pltpu.SMEM),
            pl.BlockSpec(memory_space=pltpu.SMEM),
        ),
        scratch_shapes=[pltpu.SMEM((N_EXPERTS,), I32)],
        name="moe_dest",
    )(cnt, eid, rank)


def _row_copy(src_ref, src_row, dst_ref, dst_row, sem):
    return pltpu.make_async_copy(src_ref.at[pl.ds(src_row, 1)], dst_ref.at[pl.ds(dst_row, 1)], sem)


def _wait_rows(hbm_ref, vmem_rows_ref, sem):
    n = vmem_rows_ref.shape[0]
    pltpu.make_async_copy(hbm_ref.at[pl.ds(0, n)], vmem_rows_ref, sem).wait()


def _pack_bf16_pairs(x):
    half = x.shape[1] // 2
    hi = pltpu.bitcast(x[:, :half].astype(BF16).astype(F32), U32)
    lo = pltpu.bitcast(x[:, half:].astype(BF16).astype(F32), U32)
    return hi | (lo >> 16)


def _unpack_bf16_pairs(u):
    hi = pltpu.bitcast(u & jnp.uint32(0xFFFF0000), F32).astype(BF16)
    lo = pltpu.bitcast(u << 16, F32).astype(BF16)
    return jnp.concatenate([hi, lo], axis=1)


def _scatter_kernel(dest_ref, x_ref, xs_in_ref, xs_ref, stage_ref, sem, *, tr, t, nsteps):
    del xs_in_ref
    i = pl.program_id(0)
    slot = i % 2
    base = i * tr

    def drain(s):
        for _ in range(TOPK_IN_GROUP):
            _wait_rows(xs_ref, stage_ref.at[s], sem.at[s])

    @pl.when(i >= 2)
    def _():
        drain(slot)

    stage_ref[slot] = _pack_bf16_pairs(x_ref[...])

    def issue(j, c):
        r8 = pl.multiple_of(j * V7X_SUBLANES, V7X_SUBLANES)
        for u in range(V7X_SUBLANES):
            for k in range(TOPK_IN_GROUP):
                _row_copy(stage_ref.at[slot], r8 + u, xs_ref, dest_ref[k * t + base + r8 + u],
                          sem.at[slot]).start()
        return c

    lax.fori_loop(0, tr // V7X_SUBLANES, issue, 0)

    @pl.when(i == nsteps - 1)
    def _():
        drain(slot)
        if nsteps > 1:
            drain(1 - slot)


def _scatter(dest_flat, xt, nrows):
    t, d = xt.shape
    tr = MOE_TOKEN_TILE
    nsteps = t // tr
    kern = functools.partial(_scatter_kernel, tr=tr, t=t, nsteps=nsteps)
    return pl.pallas_call(
        kern,
        out_shape=jax.ShapeDtypeStruct((nrows, d // 2), U32),
        grid_spec=pltpu.PrefetchScalarGridSpec(
            num_scalar_prefetch=1,
            grid=(t // tr,),
            in_specs=[
                pl.BlockSpec((tr, d), lambda i, dest: (i, 0)),
                pl.BlockSpec(memory_space=pl.ANY),
            ],
            out_specs=pl.BlockSpec(memory_space=pl.ANY),
            scratch_shapes=[pltpu.VMEM((2, tr, d // 2), U32), pltpu.SemaphoreType.DMA((2,))],
        ),
        input_output_aliases={2: 0},
        compiler_params=_cparams(("arbitrary",), 32),
        name="moe_scatter_rows",
    )(dest_flat, xt, jnp.zeros((nrows, d // 2), U32))


def _gmm_kernel(blke_ref, nxte_ref, meta_ref, xs_ref, w1_ref, w3_ref, w2_ref, ys_ref,
                wf1_ref, wf3_ref, wf2_ref, w1b_ref, w3b_ref, w2b_ref, slot_ref, sem, *, layer):
    nb = pl.program_id(0)

    def weight_copies(e, s):
        return [pltpu.make_async_copy(w_ref.at[layer, e], wf_ref.at[s], sem.at[s])
                for w_ref, wf_ref in ((w1_ref, wf1_ref), (w3_ref, wf3_ref), (w2_ref, wf2_ref))]

    @pl.when(nb < meta_ref[0])
    def _():
        @pl.when(nb == 0)
        def _():
            slot_ref[0] = 0
            for cp in weight_copies(blke_ref[0], 0):
                cp.start()

        @pl.when((nb == 0) | (blke_ref[nb] != blke_ref[jnp.maximum(nb - 1, 0)]))
        def _():
            s = slot_ref[0]
            for cp in weight_copies(blke_ref[nb], s):
                cp.wait()
            nxt = nxte_ref[nb]

            @pl.when(nxt >= 0)
            def _():
                for cp in weight_copies(nxt, 1 - s):
                    cp.start()

            w1b_ref[...] = wf1_ref[s].astype(BF16)
            w3b_ref[...] = wf3_ref[s].astype(BF16)
            w2b_ref[...] = wf2_ref[s].astype(BF16)
            slot_ref[0] = 1 - s

        xb = _unpack_bf16_pairs(xs_ref[...])
        h1 = _dot(xb, w1b_ref[...])
        h3 = _dot(xb, w3b_ref[...])
        hh = (h1 * jax.nn.sigmoid(h1) * h3).astype(BF16)
        ys_ref[...] = _dot(hh, w2b_ref[...])

    @pl.when(nb >= meta_ref[0])
    def _():
        ys_ref[...] = jnp.zeros_like(ys_ref)


def _gmm(blke, nxte, meta, xs, w1, w3, w2, layer):
    nrows = xs.shape[0]
    d, de = w1.shape[-2:]
    nblk = nrows // MOE_BLOCK

    def row_map(i, blke, nxte, meta):
        return (jnp.minimum(i, meta[0] - 1), 0)

    kern = functools.partial(_gmm_kernel, layer=layer)
    return pl.pallas_call(
        kern,
        out_shape=jax.ShapeDtypeStruct((nrows, d), F32),
        grid_spec=pltpu.PrefetchScalarGridSpec(
            num_scalar_prefetch=3,
            grid=(nblk,),
            in_specs=[
                pl.BlockSpec((MOE_BLOCK, d // 2), row_map),
                pl.BlockSpec(memory_space=pl.ANY),
                pl.BlockSpec(memory_space=pl.ANY),
                pl.BlockSpec(memory_space=pl.ANY),
            ],
            out_specs=pl.BlockSpec((MOE_BLOCK, d), lambda i, blke, nxte, meta: (i, 0)),
            scratch_shapes=[pltpu.VMEM((2, d, de), F32), pltpu.VMEM((2, d, de), F32),
                            pltpu.VMEM((2, de, d), F32),
                            pltpu.VMEM((d, de), BF16), pltpu.VMEM((d, de), BF16),
                            pltpu.VMEM((de, d), BF16),
                            pltpu.SMEM((1,), I32), pltpu.SemaphoreType.DMA((2,))],
        ),
        compiler_params=_cparams(("arbitrary",), 32),
        name="moe_experts",
    )(blke, nxte, meta, xs, w1, w3, w2)


def _combine_kernel(dest_ref, ys_ref, x_ref, gate_ref, g_ref, b_ref, o_ref, buf_ref, sem,
                    *, tr, t, nsteps, alpha):
    i = pl.program_id(0)
    slot = i % 2

    def gather(step, s):
        def issue(j, c):
            r8 = pl.multiple_of(j * V7X_SUBLANES, V7X_SUBLANES)
            for u in range(V7X_SUBLANES):
                for k in range(TOPK_IN_GROUP):
                    _row_copy(ys_ref, dest_ref[k * t + step * tr + r8 + u], buf_ref.at[s],
                              k * tr + r8 + u, sem.at[s]).start()
            return c

        lax.fori_loop(0, tr // V7X_SUBLANES, issue, 0)

    @pl.when(i == 0)
    def _():
        gather(0, 0)

    @pl.when(i + 1 < nsteps)
    def _():
        gather(i + 1, 1 - slot)

    _wait_rows(ys_ref, buf_ref.at[slot], sem.at[slot])
    gate = gate_ref[...]
    f = buf_ref[slot, 0:tr] * gate[:, 0:1] + buf_ref[slot, tr:2 * tr] * gate[:, 1:2]
    o_ref[...] = _layer_norm(alpha * x_ref[...] + f, g_ref[...], b_ref[...])


def _combine(dest_flat, ys, xt, gate_t, g, b, alpha):
    t, d = xt.shape
    tr = MOE_TOKEN_TILE
    nsteps = t // tr
    kern = functools.partial(_combine_kernel, tr=tr, t=t, nsteps=nsteps, alpha=alpha)
    return pl.pallas_call(
        kern,
        out_shape=jax.ShapeDtypeStruct((t, d), F32),
        grid_spec=pltpu.PrefetchScalarGridSpec(
            num_scalar_prefetch=1,
            grid=(t // tr,),
            in_specs=[
                pl.BlockSpec(memory_space=pl.ANY),
                pl.BlockSpec((tr, d), lambda i, dest: (i, 0)),
                pl.BlockSpec((tr, TOPK_IN_GROUP), lambda i, dest: (i, 0)),
                pl.BlockSpec((1, d), lambda i, dest: (0, 0)),
                pl.BlockSpec((1, d), lambda i, dest: (0, 0)),
            ],
            out_specs=pl.BlockSpec((tr, d), lambda i, dest: (i, 0)),
            scratch_shapes=[pltpu.VMEM((2, TOPK_IN_GROUP * tr, d), F32),
                            pltpu.SemaphoreType.DMA((2,))],
        ),
        compiler_params=_cparams(("arbitrary",), 32),
        name="moe_combine_ln",
    )(dest_flat, ys, xt, gate_t, g.reshape(1, d), b.reshape(1, d))


def _moe_layer(xt, wg, bg, we, be, w1, w3, w2, layer, g, b, alpha):
    t, d = xt.shape
    nblk = -(-t * TOPK_IN_GROUP // MOE_BLOCK) + N_EXPERTS
    eid, gate, rank, cnt = _router(xt, wg, bg, we, be)
    dest, blke, nxte, meta = _dest(cnt[:, 0], eid, rank, nblk)
    dest_flat = dest.reshape(-1)
    xs = _scatter(dest_flat, xt, nblk * MOE_BLOCK)
    ys = _gmm(blke, nxte, meta, xs, w1, w3, w2, layer)
    return _combine(dest_flat, ys, xt, gate.T, g, b, alpha)


def kernel(x, conv_w_in, conv_k, conv_w_out, attn_w_in, kv_norm_g, kidx_ln_g, kidx_ln_b, attn_w_out, rel_bias, router_wg, router_bg, router_we, router_be, exp_w1, exp_w3, exp_w2, ln1_g, ln1_b, ln2_g, ln2_b):
    bsz, s, d = x.shape
    depth = ln1_g.shape[0]
    n_mixers = 2
    alpha = (2.0 * depth) ** 0.25
    bias_t = _bias_tiles(rel_bias)
    for i in range(depth):
        j = i // n_mixers
        if i % n_mixers == 0:
            x = _conv_layer(x, conv_w_in[j], conv_k[j], conv_w_out[j], ln1_g[i], ln1_b[i], alpha)
        else:
            x = _attn_layer(x, attn_w_in[j], kv_norm_g[j], kidx_ln_g[j], kidx_ln_b[j],
                            attn_w_out[j], bias_t, ln1_g[i], ln1_b[i], alpha)
        xt = _moe_layer(x.reshape(bsz * s, d), router_wg[i], router_bg[i], router_we[i],
                        router_be[i], exp_w1, exp_w3, exp_w2, i, ln2_g[i], ln2_b[i], alpha)
        x = xt.reshape(bsz, s, d)
    return x
```

```python
import functools
import math

import jax
import jax.numpy as jnp
from jax import lax
from jax.experimental import pallas as pl
from jax.experimental.pallas import tpu as pltpu

CONV_WIDTH = 3
N_HEADS = 16
KV_LATENT = 128
N_IDX_HEADS = 8
IDX_DIM = 64
TOPK_MAX = 256
Q_BLOCK = 128
N_BUCKETS = 32
MAX_DISTANCE = 128
N_GROUPS = 8
EXPERTS_PER_GROUP = 8
N_EXPERTS = N_GROUPS * EXPERTS_PER_GROUP
TOPK_IN_GROUP = 2
MOE_BLOCK = 256
MOE_TOKEN_TILE = 1024
LN_EPS = 1e-5
RMS_EPS = 1e-6

V7X_LANES = 128
V7X_SUBLANES = 8
V7X_VMEM_BYTES = 64 * 1024 * 1024

F32 = jnp.float32
BF16 = jnp.bfloat16
I32 = jnp.int32
U32 = jnp.uint32
NEG_INF = float("-inf")
INT32_MIN = -(2 ** 31)
LOG2E = math.log2(math.e)
LOGIT_SCALE2 = (KV_LATENT ** -0.5) * LOG2E
BIAS_ROWS = 4 * Q_BLOCK

_NT = (((1,), (1,)), ((), ()))


def _dot(a, b):
    return jnp.dot(a, b, preferred_element_type=F32)


def _dot_nt(a, b):
    return lax.dot_general(a, b, _NT, preferred_element_type=F32)


def _layer_norm(z, g, b):
    mu = jnp.mean(z, axis=-1, keepdims=True)
    zc = z - mu
    var = jnp.mean(zc * zc, axis=-1, keepdims=True)
    return zc * lax.rsqrt(var + LN_EPS) * g + b


def _cparams(semantics, vmem_mib):
    assert vmem_mib * 1024 * 1024 < V7X_VMEM_BYTES
    return pltpu.CompilerParams(dimension_semantics=semantics,
                                vmem_limit_bytes=vmem_mib * 1024 * 1024)


def _conv_kernel(x_ref, win_ref, ck_ref, wout_ref, g_ref, b_ref, o_ref, carry_ref, gbuf_ref,
                 *, ts, d, cw, alpha):
    @pl.when(pl.program_id(1) == 0)
    def _():
        carry_ref[...] = jnp.zeros_like(carry_ref)

    x = x_ref[...]
    xb = x.astype(BF16)
    row = lax.broadcasted_iota(I32, (ts, cw), 0)
    for c in range(d // cw):
        lo, hi = c * cw, (c + 1) * cw
        bg = _dot(xb, win_ref[:, lo:hi])
        cg = _dot(xb, win_ref[:, d + lo:d + hi])
        hh = _dot(xb, win_ref[:, 2 * d + lo:2 * d + hi])
        u = cg * hh
        prev = carry_ref[:, lo:hi]
        u1 = jnp.where(row == 0, prev[7:8], pltpu.roll(u, 1, 0))
        u2 = jnp.where(row == 0, prev[6:7], jnp.where(row == 1, prev[7:8], pltpu.roll(u, 2, 0)))
        k = ck_ref[:, lo:hi]
        conv = u2 * k[0:1] + u1 * k[1:2] + u * k[2:3]
        gbuf_ref[:, lo:hi] = (bg * conv).astype(BF16)
        carry_ref[:, lo:hi] = u[ts - V7X_SUBLANES:ts]
    y = _dot(gbuf_ref[...], wout_ref[...])
    o_ref[...] = _layer_norm(alpha * x + y, g_ref[...], b_ref[...])


def _conv_layer(x, w_in, conv_k, w_out, g, b, alpha):
    bsz, s, d = x.shape
    ts, cw = 512, 512
    kern = functools.partial(_conv_kernel, ts=ts, d=d, cw=cw, alpha=alpha)
    return pl.pallas_call(
        kern,
        out_shape=jax.ShapeDtypeStruct((bsz, s, d), F32),
        grid=(bsz, s // ts),
        in_specs=[
            pl.BlockSpec((None, ts, d), lambda i, j: (i, j, 0)),
            pl.BlockSpec((d, 3 * d), lambda i, j: (0, 0)),
            pl.BlockSpec((CONV_WIDTH, d), lambda i, j: (0, 0)),
            pl.BlockSpec((d, d), lambda i, j: (0, 0)),
            pl.BlockSpec((1, d), lambda i, j: (0, 0)),
            pl.BlockSpec((1, d), lambda i, j: (0, 0)),
        ],
        out_specs=pl.BlockSpec((None, ts, d), lambda i, j: (i, j, 0)),
        scratch_shapes=[pltpu.VMEM((V7X_SUBLANES, d), F32), pltpu.VMEM((ts, d), BF16)],
        compiler_params=_cparams(("arbitrary", "arbitrary"), 48),
        name="conv_mixer_ln",
    )(x, w_in.astype(BF16), conv_k, w_out.astype(BF16), g.reshape(1, d), b.reshape(1, d))


def _proj_kernel(x_ref, wq_ref, ws_ref, ww_ref, kvg_ref, lng_ref, lnb_ref,
                 q_ref, ckv_ref, ckvt_ref, qidx_ref, kidx_ref, widxt_ref, *, idx_scale):
    xb = x_ref[...].astype(BF16)
    q = _dot(xb, wq_ref[...]) * LOGIT_SCALE2
    for h in range(N_HEADS):
        q_ref[h] = q[:, h * KV_LATENT:(h + 1) * KV_LATENT].astype(BF16)
    sm = _dot(xb, ws_ref[...])
    ckv = sm[:, :KV_LATENT]
    ckv = ckv * lax.rsqrt(jnp.mean(ckv * ckv, axis=-1, keepdims=True) + RMS_EPS) * kvg_ref[...]
    ckv_ref[...] = ckv.astype(BF16)
    ckvt_ref[...] = ckv.T.astype(BF16)
    nq = N_IDX_HEADS * IDX_DIM
    qidx_ref[...] = sm[:, KV_LATENT:KV_LATENT + nq].astype(BF16)
    kidx = sm[:, KV_LATENT + nq:KV_LATENT + nq + IDX_DIM]
    kidx_ref[...] = _layer_norm(kidx, lng_ref[...], lnb_ref[...]).astype(BF16)
    widxt_ref[...] = _dot_nt(ww_ref[...], xb) * idx_scale


def _attn_proj(x, w_in, kv_g, ln_g, ln_b):
    bsz, s, d = x.shape
    ts = 512
    hq = N_HEADS * KV_LATENT
    nq = N_IDX_HEADS * IDX_DIM
    small = KV_LATENT + nq + IDX_DIM
    small_pad = -(-small // V7X_LANES) * V7X_LANES
    wq = w_in[:, :hq].astype(BF16)
    ws = jnp.pad(w_in[:, hq:hq + small], ((0, 0), (0, small_pad - small))).astype(BF16)
    ww = w_in[:, hq + small:].T.astype(BF16)
    idx_scale = (N_IDX_HEADS ** -0.5) * (IDX_DIM ** -0.5)
    kern = functools.partial(_proj_kernel, idx_scale=idx_scale)
    return pl.pallas_call(
        kern,
        out_shape=(
            jax.ShapeDtypeStruct((bsz, N_HEADS, s, KV_LATENT), BF16),
            jax.ShapeDtypeStruct((bsz, s, KV_LATENT), BF16),
            jax.ShapeDtypeStruct((bsz, KV_LATENT, s), BF16),
            jax.ShapeDtypeStruct((bsz, s, nq), BF16),
            jax.ShapeDtypeStruct((bsz, s, IDX_DIM), BF16),
            jax.ShapeDtypeStruct((bsz, N_IDX_HEADS, s), F32),
        ),
        grid=(bsz, s // ts),
        in_specs=[
            pl.BlockSpec((None, ts, d), lambda i, j: (i, j, 0)),
            pl.BlockSpec((d, hq), lambda i, j: (0, 0)),
            pl.BlockSpec((d, small_pad), lambda i, j: (0, 0)),
            pl.BlockSpec((N_IDX_HEADS, d), lambda i, j: (0, 0)),
            pl.BlockSpec((1, KV_LATENT), lambda i, j: (0, 0)),
            pl.BlockSpec((1, IDX_DIM), lambda i, j: (0, 0)),
            pl.BlockSpec((1, IDX_DIM), lambda i, j: (0, 0)),
        ],
        out_specs=(
            pl.BlockSpec((None, N_HEADS, ts, KV_LATENT), lambda i, j: (i, 0, j, 0)),
            pl.BlockSpec((None, ts, KV_LATENT), lambda i, j: (i, j, 0)),
            pl.BlockSpec((None, KV_LATENT, ts), lambda i, j: (i, 0, j)),
            pl.BlockSpec((None, ts, nq), lambda i, j: (i, j, 0)),
            pl.BlockSpec((None, ts, IDX_DIM), lambda i, j: (i, j, 0)),
            pl.BlockSpec((None, N_IDX_HEADS, ts), lambda i, j: (i, 0, j)),
        ),
        compiler_params=_cparams(("arbitrary", "arbitrary"), 48),
        name="attn_proj",
    )(x, wq, ws, ww, kv_g.reshape(1, -1), ln_g.reshape(1, -1), ln_b.reshape(1, -1))


def _bias_kernel(rb_ref, o_ref):
    rows = o_ref.shape[1]
    j = lax.broadcasted_iota(I32, (rows, Q_BLOCK), 0)
    r = lax.broadcasted_iota(I32, (rows, Q_BLOCK), 1)
    dist = 2 * Q_BLOCK + r - j
    dpos = jnp.maximum(dist, 0)
    max_exact = N_BUCKETS // 2
    d_f = jnp.maximum(dpos, 1).astype(F32)
    large = max_exact + (jnp.log(d_f / max_exact) / math.log(MAX_DISTANCE / max_exact)
                         * (N_BUCKETS - max_exact)).astype(I32)
    large = jnp.minimum(large, N_BUCKETS - 1)
    bucket = jnp.where(dpos < max_exact, dpos, large)
    for h in range(N_HEADS):
        acc = jnp.zeros((rows, Q_BLOCK), F32)
        for bk in range(N_BUCKETS):
            acc = jnp.where(bucket == bk, rb_ref[bk, h], acc)
        o_ref[h] = jnp.where(dist >= 0, (acc - rb_ref[N_BUCKETS - 1, h]) * LOG2E, 0.0)


def _bias_tiles(rel_bias):
    return pl.pallas_call(
        _bias_kernel,
        out_shape=jax.ShapeDtypeStruct((N_HEADS, BIAS_ROWS, Q_BLOCK), F32),
        in_specs=[pl.BlockSpec(memory_space=pltpu.SMEM)],
        out_specs=pl.BlockSpec(memory_space=pltpu.VMEM),
        name="rel_bias_tiles",
    )(rel_bias)


def _rows_reduce(parts, op):
    accs = [None, None]
    for c, p in enumerate(parts):
        accs[c % 2] = p if accs[c % 2] is None else op(accs[c % 2], p)
    return accs[0] if accs[1] is None else op(accs[0], accs[1])


def _attn_body(qi, q_ref, qidx_ref, widxt_ref, kidx_ref, ckv_ref, ckvt_ref, bias_ref, x_ref,
               wout_ref, g_ref, b_ref, o_ref, key_ref, madd_ref, lg_ref, eb_ref, obuf_ref,
               *, sk, first_bias_chunk, topk, alpha):
    ch = Q_BLOCK
    n = sk // ch
    t_abs = qi * Q_BLOCK + lax.broadcasted_iota(I32, (1, Q_BLOCK), 1)
    s_abs = lax.broadcasted_iota(I32, (sk, Q_BLOCK), 0)
    valid = s_abs <= t_abs

    kidx = kidx_ref[0:sk, :]
    score = jnp.zeros((sk, Q_BLOCK), F32)
    for h in range(N_IDX_HEADS):
        sh = _dot_nt(kidx, qidx_ref[:, h * IDX_DIM:(h + 1) * IDX_DIM])
        score = score + jnp.maximum(sh, 0.0) * widxt_ref[h:h + 1, :]
    score = jnp.where(score == 0.0, 0.0, score)
    score = jnp.where(valid, score, NEG_INF)
    bits = pltpu.bitcast(score, I32)
    key_ref[0:sk] = bits ^ ((bits >> 31) & 0x7FFFFFFF)

    def count(pred):
        parts = [jnp.where(pred(key_ref[c * ch:(c + 1) * ch]), 1.0, 0.0) for c in range(n)]
        return jnp.sum(_rows_reduce(parts, jnp.add), axis=0, keepdims=True)

    def bit_body(it, thr):
        cand = thr + lax.shift_left(jnp.int32(1), 31 - it)
        return jnp.where(count(lambda k: k >= cand) >= topk, cand, thr)

    thr = lax.fori_loop(0, 32, bit_body, jnp.full((1, Q_BLOCK), INT32_MIN, I32))
    cnt_ge = count(lambda k: k >= thr)
    madd_ref[0:sk] = jnp.where((key_ref[0:sk] >= thr) & valid, 0.0, NEG_INF)

    tied = jnp.where((cnt_ge > topk) & (t_abs >= topk - 1), 1.0, 0.0)

    @pl.when(jnp.max(tied) > 0.0)
    def _():
        chunk = ch
        need = topk - count(lambda k: k > thr)
        tri = jnp.where(lax.broadcasted_iota(I32, (chunk, chunk), 0)
                        >= lax.broadcasted_iota(I32, (chunk, chunk), 1), 1.0, 0.0).astype(BF16)
        run = jnp.zeros((1, Q_BLOCK), F32)
        for c in range(sk // chunk):
            kc = key_ref[c * chunk:(c + 1) * chunk]
            tie = kc == thr
            pre = _dot(tri, jnp.where(tie, 1.0, 0.0).astype(BF16)) + run
            run = pre[chunk - 1:chunk]
            vc = (c * chunk + lax.broadcasted_iota(I32, (chunk, Q_BLOCK), 0)) <= t_abs
            sel = ((kc > thr) | (tie & (pre <= need))) & vc
            madd_ref[c * chunk:(c + 1) * chunk] = jnp.where(sel, 0.0, NEG_INF)

    last_tile = BIAS_ROWS // ch - 1
    cr = 64
    nsteps = sk // cr
    half = sk // 2
    assert half % cr == 0
    npairs = N_HEADS // 2

    def stage_steps(logits, exps, m_exp, pv, sum_pv):
        p_logits, s_logits = logits or (None, None)
        p_exp, s_exp = exps or (None, None)
        p_pv, s_pv = pv or (None, None)
        lg_w = lg_r = eb_w = macc = sacc = logit = ot = None
        if p_pv is not None:
            eb_r = eb_ref.at[s_pv]
            ot = (_dot(ckvt_ref[:, 0:half], eb_r[0:half])
                  + _dot(ckvt_ref[:, half:sk], eb_r[half:sk]))
        if p_logits is not None:
            q2 = q_ref[pl.ds(2 * p_logits, 2)].reshape(2 * Q_BLOCK, KV_LATENT)
            logit = [_dot_nt(ckv_ref[0:half, :], q2), _dot_nt(ckv_ref[half:sk, :], q2)]
            lg_w = lg_ref.at[s_logits]
        if p_exp is not None:
            lg_r = lg_ref.at[s_exp]
            eb_w = eb_ref.at[s_exp]
        for c in range(nsteps):
            rows = slice(c * cr, (c + 1) * cr)
            if p_logits is not None:
                madd = madd_ref[rows]
                r0 = c * cr - (c * cr // half) * half
                v = logit[c * cr // half][r0:r0 + cr] + jnp.concatenate([madd, madd], axis=1)
                blk = (c * cr) // ch
                if blk >= first_bias_chunk:
                    off = jnp.clip(2 - qi + blk, 0, last_tile) * ch + (c * cr) % ch
                    off = pl.multiple_of(off, cr)
                    v = v + jnp.concatenate([bias_ref[2 * p_logits, pl.ds(off, cr), :],
                                             bias_ref[2 * p_logits + 1, pl.ds(off, cr), :]], axis=1)
                lg_w[rows] = v
                macc = v if macc is None else jnp.maximum(macc, v)
            if p_exp is not None:
                e = jnp.exp2(lg_r[rows] - m_exp)
                eb_w[rows] = e.astype(BF16)
                sacc = e if sacc is None else sacc + e
        if p_pv is not None:
            ot = ot / sum_pv
            obuf_ref[2 * p_pv] = ot[:, :Q_BLOCK].T.astype(BF16)
            obuf_ref[2 * p_pv + 1] = ot[:, Q_BLOCK:].T.astype(BF16)
        m_new = None if macc is None else jnp.max(macc, axis=0, keepdims=True)
        s_new = None if sacc is None else jnp.sum(sacc, axis=0, keepdims=True)
        return m_new, s_new

    assert npairs % 2 == 0 and npairs >= 4
    m0, _ = stage_steps((0, 0), None, None, None, None)
    m1, s0 = stage_steps((1, 1), (0, 0), m0, None, None)

    def pipe_body(j, carry):
        m_odd, s_even = carry
        m_even, s_odd = stage_steps((2 * j + 2, 0), (2 * j + 1, 1), m_odd, (2 * j, 0), s_even)
        return stage_steps((2 * j + 3, 1), (2 * j + 2, 0), m_even, (2 * j + 1, 1), s_odd)

    m_last, s_prev = lax.fori_loop(0, npairs // 2 - 1, pipe_body, (m1, s0))
    _, s_last = stage_steps(None, (npairs - 1, 1), m_last, (npairs - 2, 0), s_prev)
    stage_steps(None, None, None, (npairs - 1, 1), s_last)
    o_all = jnp.concatenate([obuf_ref[h] for h in range(N_HEADS)], axis=1)
    acc = _dot(o_all, wout_ref[...])
    o_ref[...] = _layer_norm(alpha * x_ref[...] + acc, g_ref[...], b_ref[...])


def _attn_kernel(*refs, nv, per, topk, alpha):
    qi = pl.program_id(1)
    for v in range(nv):
        body = functools.partial(_attn_body, qi, *refs, sk=(v + 1) * per * Q_BLOCK,
                                 first_bias_chunk=v * per - 1, topk=topk, alpha=alpha)
        pl.when(qi // per == v)(body)


def _attn_layer(x, w_in, kv_g, ln_g, ln_b, w_out, bias_t, g, b, alpha):
    bsz, s, d = x.shape
    nq = s // Q_BLOCK
    topk = min(TOPK_MAX, s // 4)
    per = -(-topk // Q_BLOCK)
    assert nq % per == 0
    nv = nq // per
    q, ckv, ckvt, qidx, kidx, widxt = _attn_proj(x, w_in, kv_g, ln_g, ln_b)
    kern = functools.partial(_attn_kernel, nv=nv, per=per, topk=topk, alpha=alpha)
    return pl.pallas_call(
        kern,
        out_shape=jax.ShapeDtypeStruct((bsz, s, d), F32),
        grid=(bsz, nq),
        in_specs=[
            pl.BlockSpec((None, N_HEADS, Q_BLOCK, KV_LATENT), lambda i, j: (i, 0, j, 0)),
            pl.BlockSpec((None, Q_BLOCK, N_IDX_HEADS * IDX_DIM), lambda i, j: (i, j, 0)),
            pl.BlockSpec((None, N_IDX_HEADS, Q_BLOCK), lambda i, j: (i, 0, j)),
            pl.BlockSpec((None, s, IDX_DIM), lambda i, j: (i, 0, 0)),
            pl.BlockSpec((None, s, KV_LATENT), lambda i, j: (i, 0, 0)),
            pl.BlockSpec((None, KV_LATENT, s), lambda i, j: (i, 0, 0)),
            pl.BlockSpec((N_HEADS, BIAS_ROWS, Q_BLOCK), lambda i, j: (0, 0, 0)),
            pl.BlockSpec((None, Q_BLOCK, d), lambda i, j: (i, j, 0)),
            pl.BlockSpec((N_HEADS * KV_LATENT, d), lambda i, j: (0, 0)),
            pl.BlockSpec((1, d), lambda i, j: (0, 0)),
            pl.BlockSpec((1, d), lambda i, j: (0, 0)),
        ],
        out_specs=pl.BlockSpec((None, Q_BLOCK, d), lambda i, j: (i, j, 0)),
        scratch_shapes=[
            pltpu.VMEM((s, Q_BLOCK), I32),
            pltpu.VMEM((s, Q_BLOCK), F32),
            pltpu.VMEM((2, s, 2 * Q_BLOCK), F32),
            pltpu.VMEM((2, s, 2 * Q_BLOCK), BF16),
            pltpu.VMEM((N_HEADS, Q_BLOCK, KV_LATENT), BF16),
        ],
        compiler_params=_cparams(("arbitrary", "arbitrary"), 48),
        name="dsa_attention_ln",
    )(q, qidx, widxt, kidx, ckv, ckvt, bias_t, x,
      w_out.astype(BF16), g.reshape(1, d), b.reshape(1, d))


def _split_bf16(a):
    hi = a.astype(BF16)
    lo = (a - hi.astype(F32)).astype(BF16)
    return hi, lo


def _router_kernel(x_ref, w_ref, b_ref, eid_ref, gate_ref, rank_ref, cnt_ref, base_ref, u_ref, *, tt):
    @pl.when(pl.program_id(0) == 0)
    def _():
        base_ref[...] = jnp.zeros_like(base_ref)
        u_ref[...] = jnp.where(lax.broadcasted_iota(I32, (tt, tt), 0)
                               < lax.broadcasted_iota(I32, (tt, tt), 1), 1.0, 0.0).astype(BF16)

    xh, xl = _split_bf16(x_ref[...])
    wh, wl = _split_bf16(w_ref[...])
    lt = _dot_nt(wh, xh) + (_dot_nt(wh, xl) + _dot_nt(wl, xh)) + b_ref[...]

    ng, ne = N_GROUPS, EXPERTS_PER_GROUP
    gl = lt[0:ng]
    iog = lax.broadcasted_iota(I32, (ng, tt), 0).astype(F32)
    gmax = jnp.max(gl, axis=0, keepdims=True)
    gidx = jnp.min(jnp.where(gl == gmax, iog, float(ng)), axis=0, keepdims=True)
    g_gate = 1.0 / jnp.sum(jnp.exp(gl - gmax), axis=0, keepdims=True)

    el = jnp.zeros((ne, tt), F32)
    for gi in range(ng):
        el = jnp.where(gidx == float(gi), lt[ng + gi * ne:ng + (gi + 1) * ne], el)
    ioe = lax.broadcasted_iota(I32, (ne, tt), 0).astype(F32)
    m1 = jnp.max(el, axis=0, keepdims=True)
    i1 = jnp.min(jnp.where(el == m1, ioe, float(ne)), axis=0, keepdims=True)
    el2 = jnp.where(ioe == i1, NEG_INF, el)
    m2 = jnp.max(el2, axis=0, keepdims=True)
    i2 = jnp.min(jnp.where(el2 == m2, ioe, float(ne)), axis=0, keepdims=True)
    ex = jnp.exp(m2 - m1)
    p1 = 1.0 / (1.0 + ex)
    gate_ref[0:1, :] = p1 * g_gate
    gate_ref[1:2, :] = ex * p1 * g_gate
    e1 = gidx * float(ne) + i1
    e2 = gidx * float(ne) + i2
    eid_ref[0:1, :] = e1.astype(I32)
    eid_ref[1:2, :] = e2.astype(I32)

    iox = lax.broadcasted_iota(I32, (N_EXPERTS, tt), 0).astype(F32)
    oh1 = jnp.where(iox == e1, 1.0, 0.0)
    oh2 = jnp.where(iox == e2, 1.0, 0.0)
    pre1 = _dot(oh1.astype(BF16), u_ref[...])
    pre2 = _dot(oh2.astype(BF16), u_ref[...])
    tot1 = jnp.sum(oh1, axis=1, keepdims=True)
    tot2 = jnp.sum(oh2, axis=1, keepdims=True)
    base = base_ref[...]
    rank_ref[0:1, :] = jnp.sum(oh1 * (base + pre1), axis=0, keepdims=True).astype(I32)
    rank_ref[1:2, :] = jnp.sum(oh2 * (base + tot1 + pre2), axis=0, keepdims=True).astype(I32)
    base = base + tot1 + tot2
    base_ref[...] = base
    cnt_ref[...] = jnp.broadcast_to(base, cnt_ref.shape).astype(I32)


def _router(xt, wg, bg, we, be):
    t, d = xt.shape
    tt = 512
    rows = V7X_LANES
    wcat = jnp.pad(jnp.concatenate([wg, we], axis=1).T, ((0, rows - N_GROUPS - N_EXPERTS), (0, 0)))
    bcat = jnp.pad(jnp.concatenate([bg, be]), (0, rows - N_GROUPS - N_EXPERTS)).reshape(rows, 1)
    kern = functools.partial(_router_kernel, tt=tt)
    return pl.pallas_call(
        kern,
        out_shape=(
            jax.ShapeDtypeStruct((TOPK_IN_GROUP, t), I32),
            jax.ShapeDtypeStruct((TOPK_IN_GROUP, t), F32),
            jax.ShapeDtypeStruct((TOPK_IN_GROUP, t), I32),
            jax.ShapeDtypeStruct((N_EXPERTS, V7X_LANES), I32),
        ),
        grid=(t // tt,),
        in_specs=[
            pl.BlockSpec((tt, d), lambda i: (i, 0)),
            pl.BlockSpec((rows, d), lambda i: (0, 0)),
            pl.BlockSpec((rows, 1), lambda i: (0, 0)),
        ],
        out_specs=(
            pl.BlockSpec((TOPK_IN_GROUP, tt), lambda i: (0, i)),
            pl.BlockSpec((TOPK_IN_GROUP, tt), lambda i: (0, i)),
            pl.BlockSpec((TOPK_IN_GROUP, tt), lambda i: (0, i)),
            pl.BlockSpec((N_EXPERTS, V7X_LANES), lambda i: (0, 0)),
        ),
        scratch_shapes=[pltpu.VMEM((N_EXPERTS, 1), F32), pltpu.VMEM((tt, tt), BF16)],
        compiler_params=_cparams(("arbitrary",), 32),
        name="moe_router",
    )(xt, wcat, bcat)


def _dest_kernel(cnt_ref, eid_ref, rank_ref, dest_ref, blke_ref, nxte_ref, meta_ref, pstart_ref, *, nblk):
    shift = MOE_BLOCK.bit_length() - 1

    def expert_body(e, acc):
        nb = (cnt_ref[e] + (MOE_BLOCK - 1)) >> shift
        pstart_ref[e] = acc
        b0 = acc >> shift

        def blk_body(j, c):
            blke_ref[b0 + j] = e
            return c

        lax.fori_loop(0, nb, blk_body, 0)
        return acc + (nb << shift)

    total = lax.fori_loop(0, N_EXPERTS, expert_body, jnp.int32(0))
    nused = total >> shift
    meta_ref[0] = nused
    last_e = blke_ref[nused - 1]

    def tail_body(j, c):
        blke_ref[j] = last_e
        nxte_ref[j] = -1
        return c

    lax.fori_loop(nused, nblk, tail_body, 0)

    def next_body(i, nxt):
        e = N_EXPERTS - 1 - i
        nb = (cnt_ref[e] + (MOE_BLOCK - 1)) >> shift
        b0 = pstart_ref[e] >> shift

        def blk_body(j, c):
            nxte_ref[b0 + j] = nxt
            return c

        lax.fori_loop(0, nb, blk_body, 0)
        return jnp.where(nb > 0, e, nxt)

    lax.fori_loop(0, N_EXPERTS, next_body, jnp.int32(-1))

    def dest_body(e, dest):
        return dest + jnp.where(eid_ref[...] == e, pstart_ref[e], 0)

    dest_ref[...] = lax.fori_loop(0, N_EXPERTS, dest_body, rank_ref[...])


def _dest(cnt, eid, rank, nblk):
    t = eid.shape[1]
    kern = functools.partial(_dest_kernel, nblk=nblk)
    return pl.pallas_call(
        kern,
        out_shape=(
            jax.ShapeDtypeStruct((TOPK_IN_GROUP, t), I32),
            jax.ShapeDtypeStruct((nblk,), I32),
            jax.ShapeDtypeStruct((nblk,), I32),
            jax.ShapeDtypeStruct((1,), I32),
        ),
        in_specs=[
            pl.BlockSpec(memory_space=pltpu.SMEM),
            pl.BlockSpec(memory_space=pltpu.VMEM),
            pl.BlockSpec(memory_space=pltpu.VMEM),
        ],
        out_specs=(
            pl.BlockSpec(memory_space=pltpu.VMEM),
            pl.BlockSpec(memory_space=pltpu.SMEM),
            pl.BlockSpec(memory_space=pltpu.SMEM),
            pl.BlockSpec(memory_space=pltpu.SMEM),
        ),
        scratch_shapes=[pltpu.SMEM((N_EXPERTS,), I32)],
        name="moe_dest",
    )(cnt, eid, rank)


def _row_copy(src_ref, src_row, dst_ref, dst_row, sem):
    return pltpu.make_async_copy(src_ref.at[pl.ds(src_row, 1)], dst_ref.at[pl.ds(dst_row, 1)], sem)


def _wait_rows(hbm_ref, vmem_rows_ref, sem):
    n = vmem_rows_ref.shape[0]
    pltpu.make_async_copy(hbm_ref.at[pl.ds(0, n)], vmem_rows_ref, sem).wait()


def _pack_bf16_pairs(x):
    half = x.shape[1] // 2
    hi = pltpu.bitcast(x[:, :half].astype(BF16).astype(F32), U32)
    lo = pltpu.bitcast(x[:, half:].astype(BF16).astype(F32), U32)
    return hi | (lo >> 16)


def _unpack_bf16_pairs(u):
    hi = pltpu.bitcast(u & jnp.uint32(0xFFFF0000), F32).astype(BF16)
    lo = pltpu.bitcast(u << 16, F32).astype(BF16)
    return jnp.concatenate([hi, lo], axis=1)


def _scatter_kernel(dest_ref, x_ref, xs_in_ref, xs_ref, stage_ref, sem, *, tr, t, nsteps):
    del xs_in_ref
    i = pl.program_id(0)
    slot = i % 2
    base = i * tr

    def drain(s):
        for _ in range(TOPK_IN_GROUP):
            _wait_rows(xs_ref, stage_ref.at[s], sem.at[s])

    @pl.when(i >= 2)
    def _():
        drain(slot)

    stage_ref[slot] = _pack_bf16_pairs(x_ref[...])

    def issue(j, c):
        r8 = pl.multiple_of(j * V7X_SUBLANES, V7X_SUBLANES)
        for u in range(V7X_SUBLANES):
            for k in range(TOPK_IN_GROUP):
                _row_copy(stage_ref.at[slot], r8 + u, xs_ref, dest_ref[k * t + base + r8 + u],
                          sem.at[slot]).start()
        return c

    lax.fori_loop(0, tr // V7X_SUBLANES, issue, 0)

    @pl.when(i == nsteps - 1)
    def _():
        drain(slot)
        if nsteps > 1:
            drain(1 - slot)


def _scatter(dest_flat, xt, nrows):
    t, d = xt.shape
    tr = MOE_TOKEN_TILE
    nsteps = t // tr
    kern = functools.partial(_scatter_kernel, tr=tr, t=t, nsteps=nsteps)
    return pl.pallas_call(
        kern,
        out_shape=jax.ShapeDtypeStruct((nrows, d // 2), U32),
        grid_spec=pltpu.PrefetchScalarGridSpec(
            num_scalar_prefetch=1,
            grid=(t // tr,),
            in_specs=[
                pl.BlockSpec((tr, d), lambda i, dest: (i, 0)),
                pl.BlockSpec(memory_space=pl.ANY),
            ],
            out_specs=pl.BlockSpec(memory_space=pl.ANY),
            scratch_shapes=[pltpu.VMEM((2, tr, d // 2), U32), pltpu.SemaphoreType.DMA((2,))],
        ),
        input_output_aliases={2: 0},
        compiler_params=_cparams(("arbitrary",), 32),
        name="moe_scatter_rows",
    )(dest_flat, xt, jnp.zeros((nrows, d // 2), U32))


def _gmm_kernel(blke_ref, nxte_ref, meta_ref, xs_ref, w1_ref, w3_ref, w2_ref, ys_ref,
                wf1_ref, wf3_ref, wf2_ref, w1b_ref, w3b_ref, w2b_ref, slot_ref, sem, *, layer):
    nb = pl.program_id(0)

    def weight_copies(e, s):
        return [pltpu.make_async_copy(w_ref.at[layer, e], wf_ref.at[s], sem.at[s])
                for w_ref, wf_ref in ((w1_ref, wf1_ref), (w3_ref, wf3_ref), (w2_ref, wf2_ref))]

    @pl.when(nb < meta_ref[0])
    def _():
        @pl.when(nb == 0)
        def _():
            slot_ref[0] = 0
            for cp in weight_copies(blke_ref[0], 0):
                cp.start()

        @pl.when((nb == 0) | (blke_ref[nb] != blke_ref[jnp.maximum(nb - 1, 0)]))
        def _():
            s = slot_ref[0]
            for cp in weight_copies(blke_ref[nb], s):
                cp.wait()
            nxt = nxte_ref[nb]

            @pl.when(nxt >= 0)
            def _():
                for cp in weight_copies(nxt, 1 - s):
                    cp.start()

            w1b_ref[...] = wf1_ref[s].astype(BF16)
            w3b_ref[...] = wf3_ref[s].astype(BF16)
            w2b_ref[...] = wf2_ref[s].astype(BF16)
            slot_ref[0] = 1 - s

        xb = _unpack_bf16_pairs(xs_ref[...])
        h1 = _dot(xb, w1b_ref[...])
        h3 = _dot(xb, w3b_ref[...])
        hh = (h1 * jax.nn.sigmoid(h1) * h3).astype(BF16)
        ys_ref[...] = _dot(hh, w2b_ref[...])

    @pl.when(nb >= meta_ref[0])
    def _():
        ys_ref[...] = jnp.zeros_like(ys_ref)


def _gmm(blke, nxte, meta, xs, w1, w3, w2, layer):
    nrows = xs.shape[0]
    d, de = w1.shape[-2:]
    nblk = nrows // MOE_BLOCK

    def row_map(i, blke, nxte, meta):
        return (jnp.minimum(i, meta[0] - 1), 0)

    kern = functools.partial(_gmm_kernel, layer=layer)
    return pl.pallas_call(
        kern,
        out_shape=jax.ShapeDtypeStruct((nrows, d), F32),
        grid_spec=pltpu.PrefetchScalarGridSpec(
            num_scalar_prefetch=3,
            grid=(nblk,),
            in_specs=[
                pl.BlockSpec((MOE_BLOCK, d // 2), row_map),
                pl.BlockSpec(memory_space=pl.ANY),
                pl.BlockSpec(memory_space=pl.ANY),
                pl.BlockSpec(memory_space=pl.ANY),
            ],
            out_specs=pl.BlockSpec((MOE_BLOCK, d), lambda i, blke, nxte, meta: (i, 0)),
            scratch_shapes=[pltpu.VMEM((2, d, de), F32), pltpu.VMEM((2, d, de), F32),
                            pltpu.VMEM((2, de, d), F32),
                            pltpu.VMEM((d, de), BF16), pltpu.VMEM((d, de), BF16),
                            pltpu.VMEM((de, d), BF16),
                            pltpu.SMEM((1,), I32), pltpu.SemaphoreType.DMA((2,))],
        ),
        compiler_params=_cparams(("arbitrary",), 32),
        name="moe_experts",
    )(blke, nxte, meta, xs, w1, w3, w2)


def _combine_kernel(dest_ref, ys_ref, x_ref, gate_ref, g_ref, b_ref, o_ref, buf_ref, sem,
                    *, tr, t, nsteps, alpha):
    i = pl.program_id(0)
    slot = i % 2

    def gather(step, s):
        def issue(j, c):
            r8 = pl.multiple_of(j * V7X_SUBLANES, V7X_SUBLANES)
            for u in range(V7X_SUBLANES):
                for k in range(TOPK_IN_GROUP):
                    _row_copy(ys_ref, dest_ref[k * t + step * tr + r8 + u], buf_ref.at[s],
                              k * tr + r8 + u, sem.at[s]).start()
            return c

        lax.fori_loop(0, tr // V7X_SUBLANES, issue, 0)

    @pl.when(i == 0)
    def _():
        gather(0, 0)

    @pl.when(i + 1 < nsteps)
    def _():
        gather(i + 1, 1 - slot)

    _wait_rows(ys_ref, buf_ref.at[slot], sem.at[slot])
    gate = gate_ref[...]
    f = buf_ref[slot, 0:tr] * gate[:, 0:1] + buf_ref[slot, tr:2 * tr] * gate[:, 1:2]
    o_ref[...] = _layer_norm(alpha * x_ref[...] + f, g_ref[...], b_ref[...])


def _combine(dest_flat, ys, xt, gate_t, g, b, alpha):
    t, d = xt.shape
    tr = MOE_TOKEN_TILE
    nsteps = t // tr
    kern = functools.partial(_combine_kernel, tr=tr, t=t, nsteps=nsteps, alpha=alpha)
    return pl.pallas_call(
        kern,
        out_shape=jax.ShapeDtypeStruct((t, d), F32),
        grid_spec=pltpu.PrefetchScalarGridSpec(
            num_scalar_prefetch=1,
            grid=(t // tr,),
            in_specs=[
                pl.BlockSpec(memory_space=pl.ANY),
                pl.BlockSpec((tr, d), lambda i, dest: (i, 0)),
                pl.BlockSpec((tr, TOPK_IN_GROUP), lambda i, dest: (i, 0)),
                pl.BlockSpec((1, d), lambda i, dest: (0, 0)),
                pl.BlockSpec((1, d), lambda i, dest: (0, 0)),
            ],
            out_specs=pl.BlockSpec((tr, d), lambda i, dest: (i, 0)),
            scratch_shapes=[pltpu.VMEM((2, TOPK_IN_GROUP * tr, d), F32),
                            pltpu.SemaphoreType.DMA((2,))],
        ),
        compiler_params=_cparams(("arbitrary",), 48),
        name="moe_combine_ln",
    )(dest_flat, ys, xt, gate_t, g.reshape(1, d), b.reshape(1, d))


def _moe_layer(xt, wg, bg, we, be, w1, w3, w2, layer, g, b, alpha):
    t, d = xt.shape
    nblk = -(-t * TOPK_IN_GROUP // MOE_BLOCK) + N_EXPERTS
    eid, gate, rank, cnt = _router(xt, wg, bg, we, be)
    dest, blke, nxte, meta = _dest(cnt[:, 0], eid, rank, nblk)
    dest_flat = dest.reshape(-1)
    xs = _scatter(dest_flat, xt, nblk * MOE_BLOCK)
    ys = _gmm(blke, nxte, meta, xs, w1, w3, w2, layer)
    return _combine(dest_flat, ys, xt, gate.T, g, b, alpha)


def kernel(x, conv_w_in, conv_k, conv_w_out, attn_w_in, kv_norm_g, kidx_ln_g, kidx_ln_b, attn_w_out, rel_bias, router_wg, router_bg, router_we, router_be, exp_w1, exp_w3, exp_w2, ln1_g, ln1_b, ln2_g, ln2_b):
    bsz, s, d = x.shape
    depth = ln1_g.shape[0]
    n_mixers = 2
    alpha = (2.0 * depth) ** 0.25
    bias_t = _bias_tiles(rel_bias)
    for i in range(depth):
        j = i // n_mixers
        if i % n_mixers == 0:
            x = _conv_layer(x, conv_w_in[j], conv_k[j], conv_w_out[j], ln1_g[i], ln1_b[i], alpha)
        else:
            x = _attn_layer(x, attn_w_in[j], kv_norm_g[j], kidx_ln_g[j], kidx_ln_b[j],
                            attn_w_out[j], bias_t, ln1_g[i], ln1_b[i], alpha)
        xt = _moe_layer(x.reshape(bsz * s, d), router_wg[i], router_bg[i], router_we[i],
                        router_be[i], exp_w1, exp_w3, exp_w2, i, ln2_g[i], ln2_b[i], alpha)
        x = xt.reshape(bsz, s, d)
    return x
```

```python
import functools
import math

import jax
import jax.numpy as jnp
from jax import lax
from jax.experimental import pallas as pl
from jax.experimental.pallas import tpu as pltpu

CONV_WIDTH = 3
N_HEADS = 16
KV_LATENT = 128
N_IDX_HEADS = 8
IDX_DIM = 64
TOPK_MAX = 256
Q_BLOCK = 128
N_BUCKETS = 32
MAX_DISTANCE = 128
N_GROUPS = 8
EXPERTS_PER_GROUP = 8
N_EXPERTS = N_GROUPS * EXPERTS_PER_GROUP
TOPK_IN_GROUP = 2
MOE_BLOCK = 256
MOE_TOKEN_TILE = 512
LN_EPS = 1e-5
RMS_EPS = 1e-6

V7X_LANES = 128
V7X_SUBLANES = 8
V7X_VMEM_BYTES = 64 * 1024 * 1024

F32 = jnp.float32
BF16 = jnp.bfloat16
I32 = jnp.int32
U32 = jnp.uint32
NEG_INF = float("-inf")
INT32_MIN = -(2 ** 31)
LOG2E = math.log2(math.e)
LOGIT_SCALE2 = (KV_LATENT ** -0.5) * LOG2E
BIAS_ROWS = 4 * Q_BLOCK

_NT = (((1,), (1,)), ((), ()))


def _dot(a, b):
    return jnp.dot(a, b, preferred_element_type=F32)


def _dot_nt(a, b):
    return lax.dot_general(a, b, _NT, preferred_element_type=F32)


def _layer_norm(z, g, b):
    mu = jnp.mean(z, axis=-1, keepdims=True)
    zc = z - mu
    var = jnp.mean(zc * zc, axis=-1, keepdims=True)
    return zc * lax.rsqrt(var + LN_EPS) * g + b


def _cparams(semantics, vmem_mib):
    assert vmem_mib * 1024 * 1024 < V7X_VMEM_BYTES
    return pltpu.CompilerParams(dimension_semantics=semantics,
                                vmem_limit_bytes=vmem_mib * 1024 * 1024)


def _conv_kernel(x_ref, win_ref, ck_ref, wout_ref, g_ref, b_ref, o_ref, carry_ref, gbuf_ref,
                 *, ts, d, cw, alpha):
    @pl.when(pl.program_id(1) == 0)
    def _():
        carry_ref[...] = jnp.zeros_like(carry_ref)

    x = x_ref[...]
    xb = x.astype(BF16)
    row = lax.broadcasted_iota(I32, (ts, cw), 0)
    for c in range(d // cw):
        lo, hi = c * cw, (c + 1) * cw
        bg = _dot(xb, win_ref[:, lo:hi])
        cg = _dot(xb, win_ref[:, d + lo:d + hi])
        hh = _dot(xb, win_ref[:, 2 * d + lo:2 * d + hi])
        u = cg * hh
        prev = carry_ref[:, lo:hi]
        u1 = jnp.where(row == 0, prev[7:8], pltpu.roll(u, 1, 0))
        u2 = jnp.where(row == 0, prev[6:7], jnp.where(row == 1, prev[7:8], pltpu.roll(u, 2, 0)))
        k = ck_ref[:, lo:hi]
        conv = u2 * k[0:1] + u1 * k[1:2] + u * k[2:3]
        gbuf_ref[:, lo:hi] = (bg * conv).astype(BF16)
        carry_ref[:, lo:hi] = u[ts - V7X_SUBLANES:ts]
    y = _dot(gbuf_ref[...], wout_ref[...])
    o_ref[...] = _layer_norm(alpha * x + y, g_ref[...], b_ref[...])


def _conv_layer(x, w_in, conv_k, w_out, g, b, alpha):
    bsz, s, d = x.shape
    ts, cw = 512, 512
    kern = functools.partial(_conv_kernel, ts=ts, d=d, cw=cw, alpha=alpha)
    return pl.pallas_call(
        kern,
        out_shape=jax.ShapeDtypeStruct((bsz, s, d), F32),
        grid=(bsz, s // ts),
        in_specs=[
            pl.BlockSpec((None, ts, d), lambda i, j: (i, j, 0)),
            pl.BlockSpec((d, 3 * d), lambda i, j: (0, 0)),
            pl.BlockSpec((CONV_WIDTH, d), lambda i, j: (0, 0)),
            pl.BlockSpec((d, d), lambda i, j: (0, 0)),
            pl.BlockSpec((1, d), lambda i, j: (0, 0)),
            pl.BlockSpec((1, d), lambda i, j: (0, 0)),
        ],
        out_specs=pl.BlockSpec((None, ts, d), lambda i, j: (i, j, 0)),
        scratch_shapes=[pltpu.VMEM((V7X_SUBLANES, d), F32), pltpu.VMEM((ts, d), BF16)],
        compiler_params=_cparams(("arbitrary", "arbitrary"), 48),
        name="conv_mixer_ln",
    )(x, w_in.astype(BF16), conv_k, w_out.astype(BF16), g.reshape(1, d), b.reshape(1, d))


def _proj_kernel(x_ref, wq_ref, ws_ref, ww_ref, kvg_ref, lng_ref, lnb_ref,
                 q_ref, ckv_ref, ckvt_ref, qidx_ref, kidx_ref, widxt_ref, *, idx_scale):
    xb = x_ref[...].astype(BF16)
    q = _dot(xb, wq_ref[...]) * LOGIT_SCALE2
    for h in range(N_HEADS):
        q_ref[h] = q[:, h * KV_LATENT:(h + 1) * KV_LATENT].astype(BF16)
    sm = _dot(xb, ws_ref[...])
    ckv = sm[:, :KV_LATENT]
    ckv = ckv * lax.rsqrt(jnp.mean(ckv * ckv, axis=-1, keepdims=True) + RMS_EPS) * kvg_ref[...]
    ckv_ref[...] = ckv.astype(BF16)
    ckvt_ref[...] = ckv.T.astype(BF16)
    nq = N_IDX_HEADS * IDX_DIM
    qidx_ref[...] = sm[:, KV_LATENT:KV_LATENT + nq].astype(BF16)
    kidx = sm[:, KV_LATENT + nq:KV_LATENT + nq + IDX_DIM]
    kidx_ref[...] = _layer_norm(kidx, lng_ref[...], lnb_ref[...]).astype(BF16)
    widxt_ref[...] = _dot_nt(ww_ref[...], xb) * idx_scale


def _attn_proj(x, w_in, kv_g, ln_g, ln_b):
    bsz, s, d = x.shape
    ts = 512
    hq = N_HEADS * KV_LATENT
    nq = N_IDX_HEADS * IDX_DIM
    small = KV_LATENT + nq + IDX_DIM
    small_pad = -(-small // V7X_LANES) * V7X_LANES
    wq = w_in[:, :hq].astype(BF16)
    ws = jnp.pad(w_in[:, hq:hq + small], ((0, 0), (0, small_pad - small))).astype(BF16)
    ww = w_in[:, hq + small:].T.astype(BF16)
    idx_scale = (N_IDX_HEADS ** -0.5) * (IDX_DIM ** -0.5)
    kern = functools.partial(_proj_kernel, idx_scale=idx_scale)
    return pl.pallas_call(
        kern,
        out_shape=(
            jax.ShapeDtypeStruct((bsz, N_HEADS, s, KV_LATENT), BF16),
            jax.ShapeDtypeStruct((bsz, s, KV_LATENT), BF16),
            jax.ShapeDtypeStruct((bsz, KV_LATENT, s), BF16),
            jax.ShapeDtypeStruct((bsz, s, nq), BF16),
            jax.ShapeDtypeStruct((bsz, s, IDX_DIM), BF16),
            jax.ShapeDtypeStruct((bsz, N_IDX_HEADS, s), F32),
        ),
        grid=(bsz, s // ts),
        in_specs=[
            pl.BlockSpec((None, ts, d), lambda i, j: (i, j, 0)),
            pl.BlockSpec((d, hq), lambda i, j: (0, 0)),
            pl.BlockSpec((d, small_pad), lambda i, j: (0, 0)),
            pl.BlockSpec((N_IDX_HEADS, d), lambda i, j: (0, 0)),
            pl.BlockSpec((1, KV_LATENT), lambda i, j: (0, 0)),
            pl.BlockSpec((1, IDX_DIM), lambda i, j: (0, 0)),
            pl.BlockSpec((1, IDX_DIM), lambda i, j: (0, 0)),
        ],
        out_specs=(
            pl.BlockSpec((None, N_HEADS, ts, KV_LATENT), lambda i, j: (i, 0, j, 0)),
            pl.BlockSpec((None, ts, KV_LATENT), lambda i, j: (i, j, 0)),
            pl.BlockSpec((None, KV_LATENT, ts), lambda i, j: (i, 0, j)),
            pl.BlockSpec((None, ts, nq), lambda i, j: (i, j, 0)),
            pl.BlockSpec((None, ts, IDX_DIM), lambda i, j: (i, j, 0)),
            pl.BlockSpec((None, N_IDX_HEADS, ts), lambda i, j: (i, 0, j)),
        ),
        compiler_params=_cparams(("arbitrary", "arbitrary"), 48),
        name="attn_proj",
    )(x, wq, ws, ww, kv_g.reshape(1, -1), ln_g.reshape(1, -1), ln_b.reshape(1, -1))


def _bias_kernel(rb_ref, o_ref):
    rows = o_ref.shape[1]
    j = lax.broadcasted_iota(I32, (rows, Q_BLOCK), 0)
    r = lax.broadcasted_iota(I32, (rows, Q_BLOCK), 1)
    dist = 2 * Q_BLOCK + r - j
    dpos = jnp.maximum(dist, 0)
    max_exact = N_BUCKETS // 2
    d_f = jnp.maximum(dpos, 1).astype(F32)
    large = max_exact + (jnp.log(d_f / max_exact) / math.log(MAX_DISTANCE / max_exact)
                         * (N_BUCKETS - max_exact)).astype(I32)
    large = jnp.minimum(large, N_BUCKETS - 1)
    bucket = jnp.where(dpos < max_exact, dpos, large)
    for h in range(N_HEADS):
        acc = jnp.zeros((rows, Q_BLOCK), F32)
        for bk in range(N_BUCKETS):
            acc = jnp.where(bucket == bk, rb_ref[bk, h], acc)
        o_ref[h] = jnp.where(dist >= 0, (acc - rb_ref[N_BUCKETS - 1, h]) * LOG2E, 0.0)


def _bias_tiles(rel_bias):
    return pl.pallas_call(
        _bias_kernel,
        out_shape=jax.ShapeDtypeStruct((N_HEADS, BIAS_ROWS, Q_BLOCK), F32),
        in_specs=[pl.BlockSpec(memory_space=pltpu.SMEM)],
        out_specs=pl.BlockSpec(memory_space=pltpu.VMEM),
        name="rel_bias_tiles",
    )(rel_bias)


def _rows_reduce(parts, op):
    accs = [None, None]
    for c, p in enumerate(parts):
        accs[c % 2] = p if accs[c % 2] is None else op(accs[c % 2], p)
    return accs[0] if accs[1] is None else op(accs[0], accs[1])


def _attn_body(qi, q_ref, qidx_ref, widxt_ref, kidx_ref, ckv_ref, ckvt_ref, bias_ref, x_ref,
               wout_ref, g_ref, b_ref, o_ref, key_ref, madd_ref, lg_ref, eb_ref, obuf_ref,
               *, sk, first_bias_chunk, topk, alpha):
    ch = Q_BLOCK
    n = sk // ch
    t_abs = qi * Q_BLOCK + lax.broadcasted_iota(I32, (1, Q_BLOCK), 1)
    s_abs = lax.broadcasted_iota(I32, (sk, Q_BLOCK), 0)
    valid = s_abs <= t_abs

    kidx = kidx_ref[0:sk, :]
    score = jnp.zeros((sk, Q_BLOCK), F32)
    for h in range(N_IDX_HEADS):
        sh = _dot_nt(kidx, qidx_ref[:, h * IDX_DIM:(h + 1) * IDX_DIM])
        score = score + jnp.maximum(sh, 0.0) * widxt_ref[h:h + 1, :]
    score = jnp.where(score == 0.0, 0.0, score)
    score = jnp.where(valid, score, NEG_INF)
    bits = pltpu.bitcast(score, I32)
    key_ref[0:sk] = bits ^ ((bits >> 31) & 0x7FFFFFFF)

    def count(pred):
        parts = [jnp.where(pred(key_ref[c * ch:(c + 1) * ch]), 1.0, 0.0) for c in range(n)]
        return jnp.sum(_rows_reduce(parts, jnp.add), axis=0, keepdims=True)

    def bit_body(it, thr):
        cand = thr + lax.shift_left(jnp.int32(1), 31 - it)
        return jnp.where(count(lambda k: k >= cand) >= topk, cand, thr)

    thr = lax.fori_loop(0, 32, bit_body, jnp.full((1, Q_BLOCK), INT32_MIN, I32))
    cnt_ge = count(lambda k: k >= thr)
    madd_ref[0:sk] = jnp.where((key_ref[0:sk] >= thr) & valid, 0.0, NEG_INF)

    tied = jnp.where((cnt_ge > topk) & (t_abs >= topk - 1), 1.0, 0.0)

    @pl.when(jnp.max(tied) > 0.0)
    def _():
        chunk = ch
        need = topk - count(lambda k: k > thr)
        tri = jnp.where(lax.broadcasted_iota(I32, (chunk, chunk), 0)
                        >= lax.broadcasted_iota(I32, (chunk, chunk), 1), 1.0, 0.0).astype(BF16)
        run = jnp.zeros((1, Q_BLOCK), F32)
        for c in range(sk // chunk):
            kc = key_ref[c * chunk:(c + 1) * chunk]
            tie = kc == thr
            pre = _dot(tri, jnp.where(tie, 1.0, 0.0).astype(BF16)) + run
            run = pre[chunk - 1:chunk]
            vc = (c * chunk + lax.broadcasted_iota(I32, (chunk, Q_BLOCK), 0)) <= t_abs
            sel = ((kc > thr) | (tie & (pre <= need))) & vc
            madd_ref[c * chunk:(c + 1) * chunk] = jnp.where(sel, 0.0, NEG_INF)

    last_tile = BIAS_ROWS // ch - 1
    cr = 64
    nsteps = sk // cr
    half = sk // 2
    assert half % cr == 0
    npairs = N_HEADS // 2

    def stage_steps(logits, exps, m_exp, pv, sum_pv):
        p_logits, s_logits = logits or (None, None)
        p_exp, s_exp = exps or (None, None)
        p_pv, s_pv = pv or (None, None)
        lg_w = lg_r = eb_w = macc = sacc = logit = ot = None
        if p_pv is not None:
            eb_r = eb_ref.at[s_pv]
            ot = (_dot(ckvt_ref[:, 0:half], eb_r[0:half])
                  + _dot(ckvt_ref[:, half:sk], eb_r[half:sk]))
        if p_logits is not None:
            q2 = q_ref[pl.ds(2 * p_logits, 2)].reshape(2 * Q_BLOCK, KV_LATENT)
            logit = [_dot_nt(ckv_ref[0:half, :], q2), _dot_nt(ckv_ref[half:sk, :], q2)]
            lg_w = lg_ref.at[s_logits]
        if p_exp is not None:
            lg_r = lg_ref.at[s_exp]
            eb_w = eb_ref.at[s_exp]
        for c in range(nsteps):
            rows = slice(c * cr, (c + 1) * cr)
            if p_logits is not None:
                madd = madd_ref[rows]
                r0 = c * cr - (c * cr // half) * half
                v = logit[c * cr // half][r0:r0 + cr] + jnp.concatenate([madd, madd], axis=1)
                blk = (c * cr) // ch
                if blk >= first_bias_chunk:
                    off = jnp.clip(2 - qi + blk, 0, last_tile) * ch + (c * cr) % ch
                    off = pl.multiple_of(off, cr)
                    v = v + jnp.concatenate([bias_ref[2 * p_logits, pl.ds(off, cr), :],
                                             bias_ref[2 * p_logits + 1, pl.ds(off, cr), :]], axis=1)
                lg_w[rows] = v
                macc = v if macc is None else jnp.maximum(macc, v)
            if p_exp is not None:
                e = jnp.exp2(lg_r[rows] - m_exp)
                eb_w[rows] = e.astype(BF16)
                sacc = e if sacc is None else sacc + e
        if p_pv is not None:
            ot = ot / sum_pv
            obuf_ref[2 * p_pv] = ot[:, :Q_BLOCK].T.astype(BF16)
            obuf_ref[2 * p_pv + 1] = ot[:, Q_BLOCK:].T.astype(BF16)
        m_new = None if macc is None else jnp.max(macc, axis=0, keepdims=True)
        s_new = None if sacc is None else jnp.sum(sacc, axis=0, keepdims=True)
        return m_new, s_new

    assert npairs % 2 == 0 and npairs >= 4
    m0, _ = stage_steps((0, 0), None, None, None, None)
    m1, s0 = stage_steps((1, 1), (0, 0), m0, None, None)

    def pipe_body(j, carry):
        m_odd, s_even = carry
        m_even, s_odd = stage_steps((2 * j + 2, 0), (2 * j + 1, 1), m_odd, (2 * j, 0), s_even)
        return stage_steps((2 * j + 3, 1), (2 * j + 2, 0), m_even, (2 * j + 1, 1), s_odd)

    m_last, s_prev = lax.fori_loop(0, npairs // 2 - 1, pipe_body, (m1, s0))
    _, s_last = stage_steps(None, (npairs - 1, 1), m_last, (npairs - 2, 0), s_prev)
    stage_steps(None, None, None, (npairs - 1, 1), s_last)
    o_all = jnp.concatenate([obuf_ref[h] for h in range(N_HEADS)], axis=1)
    acc = _dot(o_all, wout_ref[...])
    o_ref[...] = _layer_norm(alpha * x_ref[...] + acc, g_ref[...], b_ref[...])


def _attn_kernel(*refs, nv, per, topk, alpha):
    qi = pl.program_id(1)
    for v in range(nv):
        body = functools.partial(_attn_body, qi, *refs, sk=(v + 1) * per * Q_BLOCK,
                                 first_bias_chunk=v * per - 1, topk=topk, alpha=alpha)
        pl.when(qi // per == v)(body)


def _attn_layer(x, w_in, kv_g, ln_g, ln_b, w_out, bias_t, g, b, alpha):
    bsz, s, d = x.shape
    nq = s // Q_BLOCK
    topk = min(TOPK_MAX, s // 4)
    per = -(-topk // Q_BLOCK)
    assert nq % per == 0
    nv = nq // per
    q, ckv, ckvt, qidx, kidx, widxt = _attn_proj(x, w_in, kv_g, ln_g, ln_b)
    kern = functools.partial(_attn_kernel, nv=nv, per=per, topk=topk, alpha=alpha)
    return pl.pallas_call(
        kern,
        out_shape=jax.ShapeDtypeStruct((bsz, s, d), F32),
        grid=(bsz, nq),
        in_specs=[
            pl.BlockSpec((None, N_HEADS, Q_BLOCK, KV_LATENT), lambda i, j: (i, 0, j, 0)),
            pl.BlockSpec((None, Q_BLOCK, N_IDX_HEADS * IDX_DIM), lambda i, j: (i, j, 0)),
            pl.BlockSpec((None, N_IDX_HEADS, Q_BLOCK), lambda i, j: (i, 0, j)),
            pl.BlockSpec((None, s, IDX_DIM), lambda i, j: (i, 0, 0)),
            pl.BlockSpec((None, s, KV_LATENT), lambda i, j: (i, 0, 0)),
            pl.BlockSpec((None, KV_LATENT, s), lambda i, j: (i, 0, 0)),
            pl.BlockSpec((N_HEADS, BIAS_ROWS, Q_BLOCK), lambda i, j: (0, 0, 0)),
            pl.BlockSpec((None, Q_BLOCK, d), lambda i, j: (i, j, 0)),
            pl.BlockSpec((N_HEADS * KV_LATENT, d), lambda i, j: (0, 0)),
            pl.BlockSpec((1, d), lambda i, j: (0, 0)),
            pl.BlockSpec((1, d), lambda i, j: (0, 0)),
        ],
        out_specs=pl.BlockSpec((None, Q_BLOCK, d), lambda i, j: (i, j, 0)),
        scratch_shapes=[
            pltpu.VMEM((s, Q_BLOCK), I32),
            pltpu.VMEM((s, Q_BLOCK), F32),
            pltpu.VMEM((2, s, 2 * Q_BLOCK), F32),
            pltpu.VMEM((2, s, 2 * Q_BLOCK), BF16),
            pltpu.VMEM((N_HEADS, Q_BLOCK, KV_LATENT), BF16),
        ],
        compiler_params=_cparams(("arbitrary", "arbitrary"), 48),
        name="dsa_attention_ln",
    )(q, qidx, widxt, kidx, ckv, ckvt, bias_t, x,
      w_out.astype(BF16), g.reshape(1, d), b.reshape(1, d))


def _split_bf16(a):
    hi = a.astype(BF16)
    lo = (a - hi.astype(F32)).astype(BF16)
    return hi, lo


def _router_kernel(x_ref, w_ref, b_ref, eid_ref, gate_ref, rank_ref, cnt_ref, base_ref, u_ref, *, tt):
    @pl.when(pl.program_id(0) == 0)
    def _():
        base_ref[...] = jnp.zeros_like(base_ref)
        u_ref[...] = jnp.where(lax.broadcasted_iota(I32, (tt, tt), 0)
                               < lax.broadcasted_iota(I32, (tt, tt), 1), 1.0, 0.0).astype(BF16)

    xh, xl = _split_bf16(x_ref[...])
    wh, wl = _split_bf16(w_ref[...])
    lt = _dot_nt(wh, xh) + (_dot_nt(wh, xl) + _dot_nt(wl, xh)) + b_ref[...]

    ng, ne = N_GROUPS, EXPERTS_PER_GROUP
    gl = lt[0:ng]
    iog = lax.broadcasted_iota(I32, (ng, tt), 0).astype(F32)
    gmax = jnp.max(gl, axis=0, keepdims=True)
    gidx = jnp.min(jnp.where(gl == gmax, iog, float(ng)), axis=0, keepdims=True)
    g_gate = 1.0 / jnp.sum(jnp.exp(gl - gmax), axis=0, keepdims=True)

    el = jnp.zeros((ne, tt), F32)
    for gi in range(ng):
        el = jnp.where(gidx == float(gi), lt[ng + gi * ne:ng + (gi + 1) * ne], el)
    ioe = lax.broadcasted_iota(I32, (ne, tt), 0).astype(F32)
    m1 = jnp.max(el, axis=0, keepdims=True)
    i1 = jnp.min(jnp.where(el == m1, ioe, float(ne)), axis=0, keepdims=True)
    el2 = jnp.where(ioe == i1, NEG_INF, el)
    m2 = jnp.max(el2, axis=0, keepdims=True)
    i2 = jnp.min(jnp.where(el2 == m2, ioe, float(ne)), axis=0, keepdims=True)
    ex = jnp.exp(m2 - m1)
    p1 = 1.0 / (1.0 + ex)
    gate_ref[0:1, :] = p1 * g_gate
    gate_ref[1:2, :] = ex * p1 * g_gate
    e1 = gidx * float(ne) + i1
    e2 = gidx * float(ne) + i2
    eid_ref[0:1, :] = e1.astype(I32)
    eid_ref[1:2, :] = e2.astype(I32)

    iox = lax.broadcasted_iota(I32, (N_EXPERTS, tt), 0).astype(F32)
    oh1 = jnp.where(iox == e1, 1.0, 0.0)
    oh2 = jnp.where(iox == e2, 1.0, 0.0)
    pre1 = _dot(oh1.astype(BF16), u_ref[...])
    pre2 = _dot(oh2.astype(BF16), u_ref[...])
    tot1 = jnp.sum(oh1, axis=1, keepdims=True)
    tot2 = jnp.sum(oh2, axis=1, keepdims=True)
    base = base_ref[...]
    rank_ref[0:1, :] = jnp.sum(oh1 * (base + pre1), axis=0, keepdims=True).astype(I32)
    rank_ref[1:2, :] = jnp.sum(oh2 * (base + tot1 + pre2), axis=0, keepdims=True).astype(I32)
    base = base + tot1 + tot2
    base_ref[...] = base
    cnt_ref[...] = jnp.broadcast_to(base, cnt_ref.shape).astype(I32)


def _router(xt, wg, bg, we, be):
    t, d = xt.shape
    tt = 512
    rows = V7X_LANES
    wcat = jnp.pad(jnp.concatenate([wg, we], axis=1).T, ((0, rows - N_GROUPS - N_EXPERTS), (0, 0)))
    bcat = jnp.pad(jnp.concatenate([bg, be]), (0, rows - N_GROUPS - N_EXPERTS)).reshape(rows, 1)
    kern = functools.partial(_router_kernel, tt=tt)
    return pl.pallas_call(
        kern,
        out_shape=(
            jax.ShapeDtypeStruct((TOPK_IN_GROUP, t), I32),
            jax.ShapeDtypeStruct((TOPK_IN_GROUP, t), F32),
            jax.ShapeDtypeStruct((TOPK_IN_GROUP, t), I32),
            jax.ShapeDtypeStruct((N_EXPERTS, V7X_LANES), I32),
        ),
        grid=(t // tt,),
        in_specs=[
            pl.BlockSpec((tt, d), lambda i: (i, 0)),
            pl.BlockSpec((rows, d), lambda i: (0, 0)),
            pl.BlockSpec((rows, 1), lambda i: (0, 0)),
        ],
        out_specs=(
            pl.BlockSpec((TOPK_IN_GROUP, tt), lambda i: (0, i)),
            pl.BlockSpec((TOPK_IN_GROUP, tt), lambda i: (0, i)),
            pl.BlockSpec((TOPK_IN_GROUP, tt), lambda i: (0, i)),
            pl.BlockSpec((N_EXPERTS, V7X_LANES), lambda i: (0, 0)),
        ),
        scratch_shapes=[pltpu.VMEM((N_EXPERTS, 1), F32), pltpu.VMEM((tt, tt), BF16)],
        compiler_params=_cparams(("arbitrary",), 32),
        name="moe_router",
    )(xt, wcat, bcat)


def _dest_kernel(cnt_ref, eid_ref, rank_ref, dest_ref, blke_ref, nxte_ref, meta_ref, pstart_ref, *, nblk):
    shift = MOE_BLOCK.bit_length() - 1

    def expert_body(e, acc):
        nb = (cnt_ref[e] + (MOE_BLOCK - 1)) >> shift
        pstart_ref[e] = acc
        b0 = acc >> shift

        def blk_body(j, c):
            blke_ref[b0 + j] = e
            return c

        lax.fori_loop(0, nb, blk_body, 0)
        return acc + (nb << shift)

    total = lax.fori_loop(0, N_EXPERTS, expert_body, jnp.int32(0))
    nused = total >> shift
    meta_ref[0] = nused
    last_e = blke_ref[nused - 1]

    def tail_body(j, c):
        blke_ref[j] = last_e
        nxte_ref[j] = -1
        return c

    lax.fori_loop(nused, nblk, tail_body, 0)

    def next_body(i, nxt):
        e = N_EXPERTS - 1 - i
        nb = (cnt_ref[e] + (MOE_BLOCK - 1)) >> shift
        b0 = pstart_ref[e] >> shift

        def blk_body(j, c):
            nxte_ref[b0 + j] = nxt
            return c

        lax.fori_loop(0, nb, blk_body, 0)
        return jnp.where(nb > 0, e, nxt)

    lax.fori_loop(0, N_EXPERTS, next_body, jnp.int32(-1))

    def dest_body(e, dest):
        return dest + jnp.where(eid_ref[...] == e, pstart_ref[e], 0)

    dest_ref[...] = lax.fori_loop(0, N_EXPERTS, dest_body, rank_ref[...])


def _dest(cnt, eid, rank, nblk):
    t = eid.shape[1]
    kern = functools.partial(_dest_kernel, nblk=nblk)
    return pl.pallas_call(
        kern,
        out_shape=(
            jax.ShapeDtypeStruct((TOPK_IN_GROUP, t), I32),
            jax.ShapeDtypeStruct((nblk,), I32),
            jax.ShapeDtypeStruct((nblk,), I32),
            jax.ShapeDtypeStruct((1,), I32),
        ),
        in_specs=[
            pl.BlockSpec(memory_space=pltpu.SMEM),
            pl.BlockSpec(memory_space=pltpu.VMEM),
            pl.BlockSpec(memory_space=pltpu.VMEM),
        ],
        out_specs=(
            pl.BlockSpec(memory_space=pltpu.VMEM),
            pl.BlockSpec(memory_space=pltpu.SMEM),
            pl.BlockSpec(memory_space=pltpu.SMEM),
            pl.BlockSpec(memory_space=pltpu.SMEM),
        ),
        scratch_shapes=[pltpu.SMEM((N_EXPERTS,), I32)],
        name="moe_dest",
    )(cnt, eid, rank)


def _row_copy(src_ref, src_row, dst_ref, dst_row, sem):
    return pltpu.make_async_copy(src_ref.at[pl.ds(src_row, 1)], dst_ref.at[pl.ds(dst_row, 1)], sem)


def _wait_rows(hbm_ref, vmem_rows_ref, sem):
    n = vmem_rows_ref.shape[0]
    pltpu.make_async_copy(hbm_ref.at[pl.ds(0, n)], vmem_rows_ref, sem).wait()


def _pack_bf16_pairs(x):
    half = x.shape[1] // 2
    hi = pltpu.bitcast(x[:, :half].astype(BF16).astype(F32), U32)
    lo = pltpu.bitcast(x[:, half:].astype(BF16).astype(F32), U32)
    return hi | (lo >> 16)


def _unpack_bf16_pairs(u):
    hi = pltpu.bitcast(u & jnp.uint32(0xFFFF0000), F32).astype(BF16)
    lo = pltpu.bitcast(u << 16, F32).astype(BF16)
    return jnp.concatenate([hi, lo], axis=1)


def _scatter_kernel(dest_ref, x_ref, xs_in_ref, xs_ref, stage_ref, sem, *, tr, t, nsteps):
    del xs_in_ref
    i = pl.program_id(0)
    slot = i % 2
    base = i * tr

    def drain(s):
        for _ in range(TOPK_IN_GROUP):
            _wait_rows(xs_ref, stage_ref.at[s], sem.at[s])

    @pl.when(i >= 2)
    def _():
        drain(slot)

    stage_ref[slot] = _pack_bf16_pairs(x_ref[...])

    def issue(j, c):
        r8 = pl.multiple_of(j * V7X_SUBLANES, V7X_SUBLANES)
        for u in range(V7X_SUBLANES):
            for k in range(TOPK_IN_GROUP):
                _row_copy(stage_ref.at[slot], r8 + u, xs_ref, dest_ref[k * t + base + r8 + u],
                          sem.at[slot]).start(priority=(u + k) % 2)
        return c

    lax.fori_loop(0, tr // V7X_SUBLANES, issue, 0)

    @pl.when(i == nsteps - 1)
    def _():
        drain(slot)
        if nsteps > 1:
            drain(1 - slot)


def _scatter(dest_flat, xt, nrows):
    t, d = xt.shape
    tr = MOE_TOKEN_TILE
    nsteps = t // tr
    kern = functools.partial(_scatter_kernel, tr=tr, t=t, nsteps=nsteps)
    return pl.pallas_call(
        kern,
        out_shape=jax.ShapeDtypeStruct((nrows, d // 2), U32),
        grid_spec=pltpu.PrefetchScalarGridSpec(
            num_scalar_prefetch=1,
            grid=(t // tr,),
            in_specs=[
                pl.BlockSpec((tr, d), lambda i, dest: (i, 0)),
                pl.BlockSpec(memory_space=pl.ANY),
            ],
            out_specs=pl.BlockSpec(memory_space=pl.ANY),
            scratch_shapes=[pltpu.VMEM((2, tr, d // 2), U32), pltpu.SemaphoreType.DMA((2,))],
        ),
        input_output_aliases={2: 0},
        compiler_params=_cparams(("arbitrary",), 32),
        name="moe_scatter_rows",
    )(dest_flat, xt, jnp.zeros((nrows, d // 2), U32))


def _gmm_kernel(blke_ref, nxte_ref, meta_ref, xs_ref, w1_ref, w3_ref, w2_ref, ys_ref,
                wf1_ref, wf3_ref, wf2_ref, w1b_ref, w3b_ref, w2b_ref, slot_ref, sem, *, layer):
    nb = pl.program_id(0)

    def weight_copies(e, s):
        return [pltpu.make_async_copy(w_ref.at[layer, e], wf_ref.at[s], sem.at[s])
                for w_ref, wf_ref in ((w1_ref, wf1_ref), (w3_ref, wf3_ref), (w2_ref, wf2_ref))]

    @pl.when(nb < meta_ref[0])
    def _():
        @pl.when(nb == 0)
        def _():
            slot_ref[0] = 0
            for cp in weight_copies(blke_ref[0], 0):
                cp.start()

        @pl.when((nb == 0) | (blke_ref[nb] != blke_ref[jnp.maximum(nb - 1, 0)]))
        def _():
            s = slot_ref[0]
            for cp in weight_copies(blke_ref[nb], s):
                cp.wait()
            nxt = nxte_ref[nb]

            @pl.when(nxt >= 0)
            def _():
                for cp in weight_copies(nxt, 1 - s):
                    cp.start()

            w1b_ref[...] = wf1_ref[s].astype(BF16)
            w3b_ref[...] = wf3_ref[s].astype(BF16)
            w2b_ref[...] = wf2_ref[s].astype(BF16)
            slot_ref[0] = 1 - s

        xb = _unpack_bf16_pairs(xs_ref[...])
        h1 = _dot(xb, w1b_ref[...])
        h3 = _dot(xb, w3b_ref[...])
        hh = (h1 * jax.nn.sigmoid(h1) * h3).astype(BF16)
        ys_ref[...] = _dot(hh, w2b_ref[...])

    @pl.when(nb >= meta_ref[0])
    def _():
        ys_ref[...] = jnp.zeros_like(ys_ref)


def _gmm(blke, nxte, meta, xs, w1, w3, w2, layer):
    nrows = xs.shape[0]
    d, de = w1.shape[-2:]
    nblk = nrows // MOE_BLOCK

    def row_map(i, blke, nxte, meta):
        return (jnp.minimum(i, meta[0] - 1), 0)

    kern = functools.partial(_gmm_kernel, layer=layer)
    return pl.pallas_call(
        kern,
        out_shape=jax.ShapeDtypeStruct((nrows, d), F32),
        grid_spec=pltpu.PrefetchScalarGridSpec(
            num_scalar_prefetch=3,
            grid=(nblk,),
            in_specs=[
                pl.BlockSpec((MOE_BLOCK, d // 2), row_map),
                pl.BlockSpec(memory_space=pl.ANY),
                pl.BlockSpec(memory_space=pl.ANY),
                pl.BlockSpec(memory_space=pl.ANY),
            ],
            out_specs=pl.BlockSpec((MOE_BLOCK, d), lambda i, blke, nxte, meta: (i, 0)),
            scratch_shapes=[pltpu.VMEM((2, d, de), F32), pltpu.VMEM((2, d, de), F32),
                            pltpu.VMEM((2, de, d), F32),
                            pltpu.VMEM((d, de), BF16), pltpu.VMEM((d, de), BF16),
                            pltpu.VMEM((de, d), BF16),
                            pltpu.SMEM((1,), I32), pltpu.SemaphoreType.DMA((2,))],
        ),
        compiler_params=_cparams(("arbitrary",), 32),
        name="moe_experts",
    )(blke, nxte, meta, xs, w1, w3, w2)


def _combine_kernel(dest_ref, ys_ref, x_ref, gate_ref, g_ref, b_ref, o_ref, buf_ref, sem,
                    *, tr, t, nsteps, alpha):
    i = pl.program_id(0)
    slot = i % 2

    def gather(step, s):
        def issue(j, c):
            r8 = pl.multiple_of(j * V7X_SUBLANES, V7X_SUBLANES)
            for u in range(V7X_SUBLANES):
                for k in range(TOPK_IN_GROUP):
                    _row_copy(ys_ref, dest_ref[k * t + step * tr + r8 + u], buf_ref.at[s],
                              k * tr + r8 + u, sem.at[s]).start(priority=(u + k) % 2)
            return c

        lax.fori_loop(0, tr // V7X_SUBLANES, issue, 0)

    @pl.when(i == 0)
    def _():
        gather(0, 0)

    @pl.when(i + 1 < nsteps)
    def _():
        gather(i + 1, 1 - slot)

    _wait_rows(ys_ref, buf_ref.at[slot], sem.at[slot])
    gate = gate_ref[...]
    f = buf_ref[slot, 0:tr] * gate[:, 0:1] + buf_ref[slot, tr:2 * tr] * gate[:, 1:2]
    o_ref[...] = _layer_norm(alpha * x_ref[...] + f, g_ref[...], b_ref[...])


def _combine(dest_flat, ys, xt, gate_t, g, b, alpha):
    t, d = xt.shape
    tr = MOE_TOKEN_TILE
    nsteps = t // tr
    kern = functools.partial(_combine_kernel, tr=tr, t=t, nsteps=nsteps, alpha=alpha)
    return pl.pallas_call(
        kern,
        out_shape=jax.ShapeDtypeStruct((t, d), F32),
        grid_spec=pltpu.PrefetchScalarGridSpec(
            num_scalar_prefetch=1,
            grid=(t // tr,),
            in_specs=[
                pl.BlockSpec(memory_space=pl.ANY),
                pl.BlockSpec((tr, d), lambda i, dest: (i, 0)),
                pl.BlockSpec((tr, TOPK_IN_GROUP), lambda i, dest: (i, 0)),
                pl.BlockSpec((1, d), lambda i, dest: (0, 0)),
                pl.BlockSpec((1, d), lambda i, dest: (0, 0)),
            ],
            out_specs=pl.BlockSpec((tr, d), lambda i, dest: (i, 0)),
            scratch_shapes=[pltpu.VMEM((2, TOPK_IN_GROUP * tr, d), F32),
                            pltpu.SemaphoreType.DMA((2,))],
        ),
        compiler_params=_cparams(("arbitrary",), 32),
        name="moe_combine_ln",
    )(dest_flat, ys, xt, gate_t, g.reshape(1, d), b.reshape(1, d))


def _moe_layer(xt, wg, bg, we, be, w1, w3, w2, layer, g, b, alpha):
    t, d = xt.shape
    nblk = -(-t * TOPK_IN_GROUP // MOE_BLOCK) + N_EXPERTS
    eid, gate, rank, cnt = _router(xt, wg, bg, we, be)
    dest, blke, nxte, meta = _dest(cnt[:, 0], eid, rank, nblk)
    dest_flat = dest.reshape(-1)
    xs = _scatter(dest_flat, xt, nblk * MOE_BLOCK)
    ys = _gmm(blke, nxte, meta, xs, w1, w3, w2, layer)
    return _combine(dest_flat, ys, xt, gate.T, g, b, alpha)


def kernel(x, conv_w_in, conv_k, conv_w_out, attn_w_in, kv_norm_g, kidx_ln_g, kidx_ln_b, attn_w_out, rel_bias, router_wg, router_bg, router_we, router_be, exp_w1, exp_w3, exp_w2, ln1_g, ln1_b, ln2_g, ln2_b):
    bsz, s, d = x.shape
    depth = ln1_g.shape[0]
    n_mixers = 2
    alpha = (2.0 * depth) ** 0.25
    bias_t = _bias_tiles(rel_bias)
    for i in range(depth):
        j = i // n_mixers
        if i % n_mixers == 0:
            x = _conv_layer(x, conv_w_in[j], conv_k[j], conv_w_out[j], ln1_g[i], ln1_b[i], alpha)
        else:
            x = _attn_layer(x, attn_w_in[j], kv_norm_g[j], kidx_ln_g[j], kidx_ln_b[j],
                            attn_w_out[j], bias_t, ln1_g[i], ln1_b[i], alpha)
        xt = _moe_layer(x.reshape(bsz * s, d), router_wg[i], router_bg[i], router_we[i],
                        router_be[i], exp_w1, exp_w3, exp_w2, i, ln2_g[i], ln2_b[i], alpha)
        x = xt.reshape(bsz, s, d)
    return x
```
